```python
import jax, jax.numpy as jnp
from jax import lax
import numpy as np

D_MODEL = 1024
BATCH = 8
SEQ = 2048
DEPTH = 1

GRID_W = 64
CTX_LEN = 256
HEAD_DIM = 128
N_Q_HEADS = D_MODEL // HEAD_DIM
N_KV_HEADS = N_Q_HEADS // 4
GQA_GROUP = N_Q_HEADS // N_KV_HEADS
Q_WIDTH = N_Q_HEADS * HEAD_DIM
KV_WIDTH = N_KV_HEADS * HEAD_DIM
FOURIER_GROUP = 128
N_FOURIER_GROUPS = 4
FOURIER_WIDTH = N_FOURIER_GROUPS * FOURIER_GROUP
N_BRANCHES = 2
IN_WIDTH = Q_WIDTH + 2 * KV_WIDTH + FOURIER_WIDTH + N_BRANCHES * D_MODEL
D_FF = 2816
Q_BLOCK = 128
AXIS_ROPE_DIM = HEAD_DIM // 2
ROPE_THETA = 10000.0
EPS = 1e-6
N_MOD = 9
ATTN_SCALE = HEAD_DIM ** -0.5

kernel_name = 'hybrid_gqa_fourier_macaron_dit_layer'


def _rmsnorm(t, g):
    t32 = t.astype(jnp.float32)
    r = t32 * lax.rsqrt(jnp.mean(t32 * t32, axis=-1, keepdims=True) + EPS)
    return (r * g.astype(jnp.float32)).astype(t.dtype)


def _modulate(h, shift, scale):
    return h * (1.0 + scale) + shift


def _adaln(cond, w, b):
    return jax.nn.silu(cond) @ w + b


def _swiglu(h, w_in, w_out):
    gate, up = jnp.split(h @ w_in, 2, axis=-1)
    return (jax.nn.silu(gate) * up) @ w_out


def _axial_rope_tables(n_tokens, dtype):
    rows = n_tokens // GRID_W
    row_ids = jnp.repeat(jnp.arange(rows, dtype=jnp.float32), GRID_W)
    col_ids = jnp.tile(jnp.arange(GRID_W, dtype=jnp.float32), rows)
    inv_freq = ROPE_THETA ** (-jnp.arange(0, AXIS_ROPE_DIM, 2, dtype=jnp.float32) / AXIS_ROPE_DIM)
    ang = jnp.concatenate([row_ids[:, None] * inv_freq, col_ids[:, None] * inv_freq], axis=-1)
    return jnp.cos(ang).astype(dtype), jnp.sin(ang).astype(dtype)


def _apply_rope(t, cos, sin):
    t1, t2 = t[..., :AXIS_ROPE_DIM], t[..., AXIS_ROPE_DIM:]
    c, s = cos[None, :, None, :], sin[None, :, None, :]
    return jnp.concatenate([t1 * c - t2 * s, t2 * c + t1 * s], axis=-1)


def _heads_norm(t, n_heads, g):
    b, n = t.shape[:2]
    return _rmsnorm(t.reshape(b, n, n_heads, HEAD_DIM), g)


def _split_in(p):
    q, k, v, f, ga, gf = jnp.split(
        p,
        [Q_WIDTH, Q_WIDTH + KV_WIDTH, Q_WIDTH + 2 * KV_WIDTH,
         Q_WIDTH + 2 * KV_WIDTH + FOURIER_WIDTH,
         Q_WIDTH + 2 * KV_WIDTH + FOURIER_WIDTH + D_MODEL],
        axis=-1)
    return q, k, v, f, ga, gf


def _attend(q, k, v):
    s = jnp.einsum('bqkgd,bskd->bkgqs', q, k).astype(jnp.float32) * ATTN_SCALE
    p = jax.nn.softmax(s, axis=-1).astype(v.dtype)
    return jnp.einsum('bkgqs,bskd->bqkgd', p, v)


def _latent_attention(q, k, v, k_c, v_c):
    b, n = q.shape[:2]
    k_all = jnp.concatenate([k_c, k], axis=1)
    v_all = jnp.concatenate([v_c, v], axis=1)
    n_blk = n // Q_BLOCK
    qb = q.reshape(b, n_blk, Q_BLOCK, N_KV_HEADS, GQA_GROUP, HEAD_DIM)
    qb = jnp.moveaxis(qb, 1, 0)
    out = lax.map(lambda qblk: _attend(qblk, k_all, v_all), qb)
    return jnp.moveaxis(out, 0, 1).reshape(b, n, Q_WIDTH)


def _context_attention(q_c, k_c, v_c):
    b, n = q_c.shape[:2]
    qg = q_c.reshape(b, n, N_KV_HEADS, GQA_GROUP, HEAD_DIM)
    return _attend(qg, k_c, v_c).reshape(b, n, Q_WIDTH)


def _fourier_mix(f):
    b, n = f.shape[:2]
    fg = f.reshape(b, n, N_FOURIER_GROUPS, FOURIER_GROUP).astype(jnp.float32)
    y = jnp.fft.fft2(fg, axes=(1, 3), norm='ortho').real
    return y.reshape(b, n, FOURIER_WIDTH).astype(f.dtype)


def _merge(y_attn, y_four, ga, gf, w_ab, w_fb, w_o):
    merged = jax.nn.sigmoid(ga) * (y_attn @ w_ab) + jax.nn.sigmoid(gf) * (y_four @ w_fb)
    return merged @ w_o


def setup_inputs(seed: int = 0) -> dict:
    key = jax.random.key(seed)
    ks = jax.random.split(key, 20)
    L = DEPTH

    def w(k, shape, fan_in):
        return jax.random.normal(k, shape, jnp.float32) * fan_in ** -0.5

    def gain(k, shape):
        return 1.0 + 0.05 * jax.random.normal(k, shape, jnp.float32)

    return {
        'x': jax.random.normal(ks[0], (BATCH, SEQ, D_MODEL), jnp.float32),
        'c': jax.random.normal(ks[1], (BATCH, D_MODEL), jnp.float32),
        'ctx': jax.random.normal(ks[2], (BATCH, CTX_LEN, D_MODEL), jnp.float32),
        'c_ctx': jax.random.normal(ks[3], (D_MODEL,), jnp.float32),
        'w_ada': w(ks[4], (L, D_MODEL, N_MOD * D_MODEL), D_MODEL),
        'b_ada': 0.02 * jax.random.normal(ks[5], (L, N_MOD * D_MODEL), jnp.float32),
        'norm_ffn1': gain(ks[6], (L, D_MODEL)),
        'w_ffn1_in': w(ks[7], (L, D_MODEL, 2 * D_FF), D_MODEL),
        'w_ffn1_out': w(ks[8], (L, D_FF, D_MODEL), D_FF),
        'norm_mix': gain(ks[9], (L, D_MODEL)),
        'w_in': w(ks[10], (L, D_MODEL, IN_WIDTH), D_MODEL),
        'q_norm': gain(ks[11], (L, HEAD_DIM)),
        'k_norm': gain(ks[12], (L, HEAD_DIM)),
        'w_attn_branch': w(ks[13], (L, Q_WIDTH, D_MODEL), Q_WIDTH),
        'w_fourier_branch': w(ks[14], (L, FOURIER_WIDTH, D_MODEL), FOURIER_WIDTH),
        'w_out': w(ks[15], (L, D_MODEL, D_MODEL), D_MODEL),
        'norm_ffn2': gain(ks[16], (L, D_MODEL)),
        'w_ffn2_in': w(ks[17], (L, D_MODEL, 2 * D_FF), D_MODEL),
        'w_ffn2_out': w(ks[18], (L, D_FF, D_MODEL), D_FF),
    }


def reference(x, c, ctx, c_ctx, w_ada, b_ada, norm_ffn1, w_ffn1_in, w_ffn1_out, norm_mix,
              w_in, q_norm, k_norm, w_attn_branch, w_fourier_branch, w_out, norm_ffn2,
              w_ffn2_in, w_ffn2_out):
    n_lat = x.shape[1]
    cos, sin = _axial_rope_tables(n_lat, x.dtype)
    for i in range(DEPTH):
        last = i == DEPTH - 1
        mod = _adaln(c, w_ada[i], b_ada[i])[:, None, :]
        mod_c = _adaln(c_ctx, w_ada[i], b_ada[i])
        sh1, sc1, g1, sh2, sc2, g2, sh3, sc3, g3 = jnp.split(mod, N_MOD, axis=-1)
        csh1, csc1, cg1, csh2, csc2, cg2, csh3, csc3, cg3 = jnp.split(mod_c, N_MOD, axis=-1)

        x = x + 0.5 * g1 * _swiglu(_modulate(_rmsnorm(x, norm_ffn1[i]), sh1, sc1),
                                   w_ffn1_in[i], w_ffn1_out[i])
        ctx = ctx + 0.5 * cg1 * _swiglu(_modulate(_rmsnorm(ctx, norm_ffn1[i]), csh1, csc1),
                                        w_ffn1_in[i], w_ffn1_out[i])

        h = _modulate(_rmsnorm(x, norm_mix[i]), sh2, sc2)
        hc = _modulate(_rmsnorm(ctx, norm_mix[i]), csh2, csc2)
        q, k, v, f, ga, gf = _split_in(h @ w_in[i])
        q = _apply_rope(_heads_norm(q, N_Q_HEADS, q_norm[i]), cos, sin)
        k = _apply_rope(_heads_norm(k, N_KV_HEADS, k_norm[i]), cos, sin)
        v = v.reshape(v.shape[0], n_lat, N_KV_HEADS, HEAD_DIM)
        if last:
            k_c, v_c = jnp.split(hc @ w_in[i][:, Q_WIDTH:Q_WIDTH + 2 * KV_WIDTH], 2, axis=-1)
        else:
            q_c, k_c, v_c, f_c, ga_c, gf_c = _split_in(hc @ w_in[i])
        k_c = _heads_norm(k_c, N_KV_HEADS, k_norm[i])
        v_c = v_c.reshape(v_c.shape[0], v_c.shape[1], N_KV_HEADS, HEAD_DIM)

        y_attn = _latent_attention(q, k, v, k_c, v_c)
        mix = _merge(y_attn, _fourier_mix(f), ga, gf,
                     w_attn_branch[i], w_fourier_branch[i], w_out[i])
        if not last:
            q_c = _heads_norm(q_c, N_Q_HEADS, q_norm[i])
            mix_c = _merge(_context_attention(q_c, k_c, v_c), _fourier_mix(f_c), ga_c, gf_c,
                           w_attn_branch[i], w_fourier_branch[i], w_out[i])
            ctx = ctx + cg2 * mix_c
            ctx = ctx + 0.5 * cg3 * _swiglu(_modulate(_rmsnorm(ctx, norm_ffn2[i]), csh3, csc3),
                                            w_ffn2_in[i], w_ffn2_out[i])
        x = x + g2 * mix

        x = x + 0.5 * g3 * _swiglu(_modulate(_rmsnorm(x, norm_ffn2[i]), sh3, sc3),
                                   w_ffn2_in[i], w_ffn2_out[i])
    return x
```

```python
import functools

import numpy as np
import jax
import jax.numpy as jnp
from jax import lax
from jax.experimental import pallas as pl
from jax.experimental.pallas import tpu as pltpu

D_MODEL = 1024
GRID_W = 64
HEAD_DIM = 128
N_Q_HEADS = D_MODEL // HEAD_DIM
N_KV_HEADS = N_Q_HEADS // 4
GQA_GROUP = N_Q_HEADS // N_KV_HEADS
Q_WIDTH = N_Q_HEADS * HEAD_DIM
KV_WIDTH = N_KV_HEADS * HEAD_DIM
FOURIER_GROUP = 128
N_FOURIER_GROUPS = 4
FOURIER_WIDTH = N_FOURIER_GROUPS * FOURIER_GROUP
D_FF = 2816
AXIS_ROPE_DIM = HEAD_DIM // 2
ROPE_THETA = 10000.0
EPS = 1e-6
N_MOD = 9
ATTN_SCALE = HEAD_DIM ** -0.5

K_OFF = Q_WIDTH
V_OFF = K_OFF + KV_WIDTH
F_OFF = V_OFF + KV_WIDTH
GA_OFF = F_OFF + FOURIER_WIDTH
GF_OFF = GA_OFF + D_MODEL

V7X_MXU_DIM = 256
V7X_VMEM_BYTES = 64 * 2**20
VMEM_LIMIT = V7X_VMEM_BYTES - 8 * 2**20

TOKEN_TILE = 512
FF_CHUNK = V7X_MXU_DIM
ATTN_Q_TILE = 256
ADALN_TILE = 1152

F32 = jnp.float32
BF16 = jnp.bfloat16


def _params(*sem):
    return pltpu.CompilerParams(dimension_semantics=sem, vmem_limit_bytes=VMEM_LIMIT)


def _resident(shape):
    return pl.BlockSpec(shape, lambda *_: (0,) * len(shape), pipeline_mode=pl.Buffered(1))


def _rms(x, gain):
    return x * lax.rsqrt(jnp.mean(x * x, axis=-1, keepdims=True) + EPS) * gain


def _dot(a, b):
    return jnp.dot(a, b, preferred_element_type=F32)


def _adaln_kernel(c_ref, w_ref, b_ref, o_ref):
    c = c_ref[...]
    a = (c * jax.nn.sigmoid(c)).astype(BF16)
    o_ref[...] = _dot(a, w_ref[...].astype(BF16)) + b_ref[...]


def _adaln(c_all, w, b):
    rows, n_out = c_all.shape[0], w.shape[1]
    return pl.pallas_call(
        _adaln_kernel,
        out_shape=jax.ShapeDtypeStruct((rows, n_out), F32),
        grid=(n_out // ADALN_TILE,),
        in_specs=[
            pl.BlockSpec((rows, D_MODEL), lambda j: (0, 0)),
            pl.BlockSpec((D_MODEL, ADALN_TILE), lambda j: (0, j)),
            pl.BlockSpec((1, ADALN_TILE), lambda j: (0, j)),
        ],
        out_specs=pl.BlockSpec((rows, ADALN_TILE), lambda j: (0, j)),
        compiler_params=_params("arbitrary"),
        name="adaln",
    )(c_all, w, b)


def _ffn_kernel(x_ref, mod_ref, gain_ref, win_ref, wout_ref, o_ref, act_ref, *, mod_base):
    x = x_ref[0]
    shift = mod_ref[0, mod_base:mod_base + 1, :]
    scale = mod_ref[0, mod_base + 1:mod_base + 2, :]
    gate = mod_ref[0, mod_base + 2:mod_base + 3, :]
    h = (_rms(x, gain_ref[...]) * (1.0 + scale) + shift).astype(BF16)
    for j in range(D_FF // FF_CHUNK):
        lo = j * FF_CHUNK
        g = _dot(h, win_ref[:, lo:lo + FF_CHUNK])
        u = _dot(h, win_ref[:, D_FF + lo:D_FF + lo + FF_CHUNK])
        act_ref[:, lo:lo + FF_CHUNK] = (g * jax.nn.sigmoid(g) * u).astype(BF16)
    y = _dot(act_ref[...], wout_ref[...])
    o_ref[0] = x + 0.5 * gate * y


def _ffn(x, mod, mod_row, mod_base, gain, w_in, w_out):
    nb, t, _ = x.shape
    tm = min(TOKEN_TILE, t)
    return pl.pallas_call(
        functools.partial(_ffn_kernel, mod_base=mod_base),
        out_shape=jax.ShapeDtypeStruct(x.shape, F32),
        grid=(nb, t // tm),
        in_specs=[
            pl.BlockSpec((1, tm, D_MODEL), lambda b, i: (b, i, 0)),
            pl.BlockSpec((1, N_MOD, D_MODEL), lambda b, i: (mod_row(b), 0, 0)),
            _resident((1, D_MODEL)),
            _resident((D_MODEL, 2 * D_FF)),
            _resident((D_FF, D_MODEL)),
        ],
        out_specs=pl.BlockSpec((1, tm, D_MODEL), lambda b, i: (b, i, 0)),
        scratch_shapes=[pltpu.VMEM((tm, D_FF), BF16)],
        compiler_params=_params("parallel", "parallel"),
        name="ffn",
    )(x, mod, gain, w_in, w_out)


def _head_norm(t, gain):
    return t * lax.rsqrt(jnp.mean(t * t, axis=-1, keepdims=True) + EPS) * gain


def _rope(t, cos2, sin2):
    return t * cos2 + pltpu.roll(t, AXIS_ROPE_DIM, 1) * sin2


def _inproj_kernel(x_ref, mod_ref, gain_ref, w_ref, qn_ref, kn_ref, cos_ref, sin_ref,
                   q_ref, k_ref, v_ref, f_ref, ga_ref, gf_ref):
    x = x_ref[0]
    shift = mod_ref[0, 3:4, :]
    scale = mod_ref[0, 4:5, :]
    h = (_rms(x, gain_ref[...]) * (1.0 + scale) + shift).astype(BF16)
    cos2 = cos_ref[...]
    sin2 = sin_ref[...]
    qn = qn_ref[...] * ATTN_SCALE
    kn = kn_ref[...]
    for hd in range(N_Q_HEADS):
        lo = hd * HEAD_DIM
        t = _dot(h, w_ref[:, lo:lo + HEAD_DIM])
        q_ref[0, :, lo:lo + HEAD_DIM] = _rope(_head_norm(t, qn), cos2, sin2).astype(BF16)
    for hd in range(N_KV_HEADS):
        lo = hd * HEAD_DIM
        t = _dot(h, w_ref[:, K_OFF + lo:K_OFF + lo + HEAD_DIM])
        k_ref[0, :, lo:lo + HEAD_DIM] = _rope(_head_norm(t, kn), cos2, sin2).astype(BF16)
    v_ref[0] = _dot(h, w_ref[:, V_OFF:V_OFF + KV_WIDTH]).astype(BF16)
    f_ref[0] = _dot(h, w_ref[:, F_OFF:F_OFF + FOURIER_WIDTH]).astype(BF16)
    for c in range(D_MODEL // V7X_MXU_DIM):
        lo = c * V7X_MXU_DIM
        ga = _dot(h, w_ref[:, GA_OFF + lo:GA_OFF + lo + V7X_MXU_DIM])
        ga_ref[0, :, lo:lo + V7X_MXU_DIM] = jax.nn.sigmoid(ga).astype(BF16)
        gf = _dot(h, w_ref[:, GF_OFF + lo:GF_OFF + lo + V7X_MXU_DIM])
        gf_ref[0, :, lo:lo + V7X_MXU_DIM] = jax.nn.sigmoid(gf).astype(BF16)


def _inproj(x, mod, gain, w_in, q_norm, k_norm, cos2, sin2):
    nb, t, _ = x.shape
    tm = TOKEN_TILE
    tok = lambda w: pl.BlockSpec((1, tm, w), lambda b, i: (b, i, 0))
    shp = lambda w: jax.ShapeDtypeStruct((nb, t, w), BF16)
    return pl.pallas_call(
        _inproj_kernel,
        out_shape=[shp(Q_WIDTH), shp(KV_WIDTH), shp(KV_WIDTH), shp(FOURIER_WIDTH),
                   shp(D_MODEL), shp(D_MODEL)],
        grid=(nb, t // tm),
        in_specs=[
            tok(D_MODEL),
            pl.BlockSpec((1, N_MOD, D_MODEL), lambda b, i: (b, 0, 0)),
            _resident((1, D_MODEL)),
            _resident(w_in.shape),
            _resident((1, HEAD_DIM)),
            _resident((1, HEAD_DIM)),
            pl.BlockSpec((tm, HEAD_DIM), lambda b, i: (i, 0)),
            pl.BlockSpec((tm, HEAD_DIM), lambda b, i: (i, 0)),
        ],
        out_specs=[tok(Q_WIDTH), tok(KV_WIDTH), tok(KV_WIDTH), tok(FOURIER_WIDTH),
                   tok(D_MODEL), tok(D_MODEL)],
        compiler_params=_params("parallel", "parallel"),
        name="inproj",
    )(x, mod, gain, w_in, q_norm, k_norm, cos2, sin2)


def _ctx_kv_kernel(x_ref, mod_ref, gain_ref, w_ref, kn_ref, k_ref, v_ref):
    x = x_ref[0]
    shift = mod_ref[0, 3:4, :]
    scale = mod_ref[0, 4:5, :]
    h = (_rms(x, gain_ref[...]) * (1.0 + scale) + shift).astype(BF16)
    kn = kn_ref[...]
    for hd in range(N_KV_HEADS):
        lo = hd * HEAD_DIM
        t = _dot(h, w_ref[:, K_OFF + lo:K_OFF + lo + HEAD_DIM])
        k_ref[0, :, lo:lo + HEAD_DIM] = _head_norm(t, kn).astype(BF16)
    v_ref[0] = _dot(h, w_ref[:, V_OFF:V_OFF + KV_WIDTH]).astype(BF16)


def _ctx_kv(ctx, mod, mod_row, gain, w_in, k_norm):
    nb, t, _ = ctx.shape
    tok = lambda w: pl.BlockSpec((1, t, w), lambda b: (b, 0, 0))
    shp = jax.ShapeDtypeStruct((nb, t, KV_WIDTH), BF16)
    return pl.pallas_call(
        _ctx_kv_kernel,
        out_shape=[shp, shp],
        grid=(nb,),
        in_specs=[
            tok(D_MODEL),
            pl.BlockSpec((1, N_MOD, D_MODEL), lambda b: (mod_row, 0, 0)),
            _resident((1, D_MODEL)),
            _resident(w_in.shape),
            _resident((1, HEAD_DIM)),
        ],
        out_specs=[tok(KV_WIDTH), tok(KV_WIDTH)],
        compiler_params=_params("parallel"),
        name="ctx_kv",
    )(ctx, mod, gain, w_in, k_norm)


_NT = (((1,), (1,)), ((), ()))


def _attn_kernel(q_ref, kc_ref, k_ref, vc_ref, v_ref, o_ref):
    kc, k, vc, v = kc_ref[0], k_ref[0], vc_ref[0], v_ref[0]
    for g in range(GQA_GROUP):
        lo = g * HEAD_DIM
        q = q_ref[0, :, lo:lo + HEAD_DIM]
        s_c = lax.dot_general(q, kc, _NT, preferred_element_type=F32)
        s_l = lax.dot_general(q, k, _NT, preferred_element_type=F32)
        m = jnp.maximum(jnp.max(s_c, axis=-1, keepdims=True), jnp.max(s_l, axis=-1, keepdims=True))
        p_c = jnp.exp(s_c - m)
        p_l = jnp.exp(s_l - m)
        denom = jnp.sum(p_c, axis=-1, keepdims=True) + jnp.sum(p_l, axis=-1, keepdims=True)
        o = _dot(p_c.astype(BF16), vc) + _dot(p_l.astype(BF16), v)
        o_ref[0, :, lo:lo + HEAD_DIM] = (o / denom).astype(BF16)


def _attention(q, k, v, k_c, v_c):
    nb, n, _ = q.shape
    n_ctx = k_c.shape[1]
    tq = ATTN_Q_TILE
    gw = GQA_GROUP * HEAD_DIM
    kv = lambda rows: pl.BlockSpec((1, rows, HEAD_DIM), lambda b, h, i: (b, 0, h))
    return pl.pallas_call(
        _attn_kernel,
        out_shape=jax.ShapeDtypeStruct((nb, n, Q_WIDTH), BF16),
        grid=(nb, N_KV_HEADS, n // tq),
        in_specs=[
            pl.BlockSpec((1, tq, gw), lambda b, h, i: (b, i, h)),
            kv(n_ctx), kv(n), kv(n_ctx), kv(n),
        ],
        out_specs=pl.BlockSpec((1, tq, gw), lambda b, h, i: (b, i, h)),
        compiler_params=_params("parallel", "parallel", "parallel"),
        name="attn",
    )(q, k_c, k, v_c, v)


def _dft_tables(n):
    c = FOURIER_GROUP
    scale = 1.0 / np.sqrt(float(n * c))
    mc = np.outer(np.arange(c), np.arange(c)) % c
    ang_c = 2.0 * np.pi * mc / c
    eye = np.eye(N_FOURIER_GROUPS)
    chan = np.concatenate([np.kron(eye, np.cos(ang_c) * scale),
                           np.kron(eye, np.sin(ang_c) * scale)], axis=1)
    kn = np.outer(np.arange(n), np.arange(n)) % n
    ang_n = 2.0 * np.pi * kn / n
    pos = np.concatenate([np.cos(ang_n), -np.sin(ang_n)], axis=1)
    return jnp.asarray(chan, F32).astype(BF16), jnp.asarray(pos, F32).astype(BF16)


def _fourier_kernel(f_ref, chan_ref, pos_ref, o_ref, z_ref, *, row_tile):
    n = f_ref.shape[1]
    for r in range(n // row_tile):
        lo = r * row_tile
        ab = _dot(f_ref[0, lo:lo + row_tile, :], chan_ref[...])
        z_ref[lo:lo + row_tile, :] = ab[:, :FOURIER_WIDTH].astype(BF16)
        z_ref[n + lo:n + lo + row_tile, :] = ab[:, FOURIER_WIDTH:].astype(BF16)
    for r in range(n // row_tile):
        lo = r * row_tile
        o_ref[0, lo:lo + row_tile, :] = _dot(pos_ref[lo:lo + row_tile, :], z_ref[...]).astype(BF16)


def _fourier(f):
    nb, n, _ = f.shape
    chan, pos = _dft_tables(n)
    blk = pl.BlockSpec((1, n, FOURIER_WIDTH), lambda b: (b, 0, 0))
    return pl.pallas_call(
        functools.partial(_fourier_kernel, row_tile=V7X_MXU_DIM),
        out_shape=jax.ShapeDtypeStruct(f.shape, BF16),
        grid=(nb,),
        in_specs=[blk, _resident(chan.shape), _resident(pos.shape)],
        out_specs=blk,
        scratch_shapes=[pltpu.VMEM((2 * n, FOURIER_WIDTH), BF16)],
        compiler_params=_params("parallel"),
        name="fourier",
    )(f, chan, pos)


def _merge_kernel(x_ref, mod_ref, ya_ref, yf_ref, ga_ref, gf_ref, wab_ref, wfb_ref, wo_ref, o_ref):
    gate = mod_ref[0, 5:6, :]
    merged = (ga_ref[0].astype(F32) * _dot(ya_ref[0], wab_ref[...])
              + gf_ref[0].astype(F32) * _dot(yf_ref[0], wfb_ref[...]))
    mix = _dot(merged.astype(BF16), wo_ref[...])
    o_ref[0] = x_ref[0] + gate * mix


def _merge(x, mod, y_attn, y_four, sig_a, sig_f, w_ab, w_fb, w_o):
    nb, t, _ = x.shape
    tm = TOKEN_TILE
    tok = lambda w: pl.BlockSpec((1, tm, w), lambda b, i: (b, i, 0))
    return pl.pallas_call(
        _merge_kernel,
        out_shape=jax.ShapeDtypeStruct(x.shape, F32),
        grid=(nb, t // tm),
        in_specs=[
            tok(D_MODEL),
            pl.BlockSpec((1, N_MOD, D_MODEL), lambda b, i: (b, 0, 0)),
            tok(Q_WIDTH), tok(FOURIER_WIDTH), tok(D_MODEL), tok(D_MODEL),
            _resident(w_ab.shape), _resident(w_fb.shape), _resident(w_o.shape),
        ],
        out_specs=tok(D_MODEL),
        compiler_params=_params("parallel", "parallel"),
        name="merge",
    )(x, mod, y_attn, y_four, sig_a, sig_f, w_ab, w_fb, w_o)


def _rope_tables(n_tokens):
    rows = n_tokens // GRID_W
    row_ids = jnp.repeat(jnp.arange(rows, dtype=F32), GRID_W)
    col_ids = jnp.tile(jnp.arange(GRID_W, dtype=F32), rows)
    inv_freq = ROPE_THETA ** (-jnp.arange(0, AXIS_ROPE_DIM, 2, dtype=F32) / AXIS_ROPE_DIM)
    ang = jnp.concatenate([row_ids[:, None] * inv_freq, col_ids[:, None] * inv_freq], axis=-1)
    cos, sin = jnp.cos(ang), jnp.sin(ang)
    return jnp.concatenate([cos, cos], axis=-1), jnp.concatenate([-sin, sin], axis=-1)


def kernel(x, c, ctx, c_ctx, w_ada, b_ada, norm_ffn1, w_ffn1_in, w_ffn1_out, norm_mix, w_in,
           q_norm, k_norm, w_attn_branch, w_fourier_branch, w_out, norm_ffn2, w_ffn2_in, w_ffn2_out):
    nb, n_lat, _ = x.shape
    n_ctx = ctx.shape[1]
    depth = w_ada.shape[0]
    assert depth == 1, "the context stream is only carried through the (single) last layer"
    cos2, sin2 = _rope_tables(n_lat)
    ctx_row = nb
    pad_rows = -(nb + 1) % 8
    c_all = jnp.concatenate([c, c_ctx[None, :], jnp.zeros((pad_rows, D_MODEL), F32)], axis=0)

    i = 0
    mod = _adaln(c_all, w_ada[i], b_ada[i][None, :]).reshape(c_all.shape[0], N_MOD, D_MODEL)
    w1_in, w1_out = w_ffn1_in[i].astype(BF16), w_ffn1_out[i].astype(BF16)
    w2_in, w2_out = w_ffn2_in[i].astype(BF16), w_ffn2_out[i].astype(BF16)
    w_in_b = w_in[i].astype(BF16)

    x = _ffn(x, mod, lambda b: b, 0, norm_ffn1[i][None, :], w1_in, w1_out)
    ctx_flat = _ffn(ctx.reshape(1, nb * n_ctx, D_MODEL), mod, lambda b: ctx_row, 0,
                    norm_ffn1[i][None, :], w1_in, w1_out)

    q, k, v, f, sig_a, sig_f = _inproj(x, mod, norm_mix[i][None, :], w_in_b,
                                       q_norm[i][None, :], k_norm[i][None, :], cos2, sin2)
    k_c, v_c = _ctx_kv(ctx_flat.reshape(nb, n_ctx, D_MODEL), mod, ctx_row, norm_mix[i][None, :],
                       w_in_b, k_norm[i][None, :])
    y_attn = _attention(q, k, v, k_c, v_c)
    y_four = _fourier(f)
    x = _merge(x, mod, y_attn, y_four, sig_a, sig_f, w_attn_branch[i].astype(BF16),
               w_fourier_branch[i].astype(BF16), w_out[i].astype(BF16))

    return _ffn(x, mod, lambda b: b, 6, norm_ffn2[i][None, :], w2_in, w2_out)
```

```python
import functools

import numpy as np
import jax
import jax.numpy as jnp
from jax import lax
from jax.experimental import pallas as pl
from jax.experimental.pallas import tpu as pltpu

D_MODEL = 1024
GRID_W = 64
HEAD_DIM = 128
N_Q_HEADS = D_MODEL // HEAD_DIM
N_KV_HEADS = N_Q_HEADS // 4
GQA_GROUP = N_Q_HEADS // N_KV_HEADS
Q_WIDTH = N_Q_HEADS * HEAD_DIM
KV_WIDTH = N_KV_HEADS * HEAD_DIM
FOURIER_GROUP = 128
N_FOURIER_GROUPS = 4
FOURIER_WIDTH = N_FOURIER_GROUPS * FOURIER_GROUP
D_FF = 2816
AXIS_ROPE_DIM = HEAD_DIM // 2
ROPE_THETA = 10000.0
EPS = 1e-6
N_MOD = 9
ATTN_SCALE = HEAD_DIM ** -0.5
LOG2_E = 1.4426950408889634

K_OFF = Q_WIDTH
V_OFF = K_OFF + KV_WIDTH
F_OFF = V_OFF + KV_WIDTH
GA_OFF = F_OFF + FOURIER_WIDTH
GF_OFF = GA_OFF + D_MODEL

V7X_MXU_DIM = 256
VX_WIDTH = N_KV_HEADS * V7X_MXU_DIM
V7X_VMEM_BYTES = 64 * 2**20
VMEM_LIMIT = V7X_VMEM_BYTES - 8 * 2**20

TOKEN_TILE = 512
FF_CHUNK = V7X_MXU_DIM
ATTN_Q_TILE = 512
ADALN_TILE = 1152

F32 = jnp.float32
BF16 = jnp.bfloat16


def _params(*sem):
    return pltpu.CompilerParams(dimension_semantics=sem, vmem_limit_bytes=VMEM_LIMIT)


def _resident(shape):
    return pl.BlockSpec(shape, lambda *_: (0,) * len(shape), pipeline_mode=pl.Buffered(1))


def _rms(x, gain):
    return x * lax.rsqrt(jnp.mean(x * x, axis=-1, keepdims=True) + EPS) * gain


def _dot(a, b):
    return jnp.dot(a, b, preferred_element_type=F32)


def _adaln_kernel(c_ref, w_ref, b_ref, o_ref):
    c = c_ref[...]
    a = (c * jax.nn.sigmoid(c)).astype(BF16)
    o_ref[...] = _dot(a, w_ref[...].astype(BF16)) + b_ref[...]


def _adaln(c_all, w, b):
    rows, n_out = c_all.shape[0], w.shape[1]
    return pl.pallas_call(
        _adaln_kernel,
        out_shape=jax.ShapeDtypeStruct((rows, n_out), F32),
        grid=(n_out // ADALN_TILE,),
        in_specs=[
            pl.BlockSpec((rows, D_MODEL), lambda j: (0, 0)),
            pl.BlockSpec((D_MODEL, ADALN_TILE), lambda j: (0, j)),
            pl.BlockSpec((1, ADALN_TILE), lambda j: (0, j)),
        ],
        out_specs=pl.BlockSpec((rows, ADALN_TILE), lambda j: (0, j)),
        compiler_params=_params("arbitrary"),
        name="adaln",
    )(c_all, w, b)


def _ffn_kernel(x_ref, mod_ref, gain_ref, win_ref, wout_ref, o_ref, act_ref, *, mod_base):
    x = x_ref[0]
    shift = mod_ref[0, mod_base:mod_base + 1, :]
    scale = mod_ref[0, mod_base + 1:mod_base + 2, :]
    gate = mod_ref[0, mod_base + 2:mod_base + 3, :]
    h = (_rms(x, gain_ref[...]) * (1.0 + scale) + shift).astype(BF16)
    for j in range(D_FF // FF_CHUNK):
        lo = j * FF_CHUNK
        g = _dot(h, win_ref[:, lo:lo + FF_CHUNK])
        u = _dot(h, win_ref[:, D_FF + lo:D_FF + lo + FF_CHUNK])
        act_ref[:, lo:lo + FF_CHUNK] = (g * jax.nn.sigmoid(g) * u).astype(BF16)
    y = _dot(act_ref[...], wout_ref[...])
    o_ref[0] = x + 0.5 * gate * y


def _ffn(x, mod, mod_row, mod_base, gain, w_in, w_out):
    nb, t, _ = x.shape
    tm = min(TOKEN_TILE, t)
    return pl.pallas_call(
        functools.partial(_ffn_kernel, mod_base=mod_base),
        out_shape=jax.ShapeDtypeStruct(x.shape, F32),
        grid=(nb, t // tm),
        in_specs=[
            pl.BlockSpec((1, tm, D_MODEL), lambda b, i: (b, i, 0)),
            pl.BlockSpec((1, N_MOD, D_MODEL), lambda b, i: (mod_row(b), 0, 0)),
            _resident((1, D_MODEL)),
            _resident((D_MODEL, 2 * D_FF)),
            _resident((D_FF, D_MODEL)),
        ],
        out_specs=pl.BlockSpec((1, tm, D_MODEL), lambda b, i: (b, i, 0)),
        scratch_shapes=[pltpu.VMEM((tm, D_FF), BF16)],
        compiler_params=_params("parallel", "parallel"),
        name="ffn",
    )(x, mod, gain, w_in, w_out)


def _head_norm(t, gain):
    return t * lax.rsqrt(jnp.mean(t * t, axis=-1, keepdims=True) + EPS) * gain


def _rope(t, cos2, sin2):
    return t * cos2 + pltpu.roll(t, AXIS_ROPE_DIM, 1) * sin2


def _store_widened_values(v_ref, v):
    ones = jnp.ones((v.shape[0], V7X_MXU_DIM - HEAD_DIM), BF16)
    for hd in range(N_KV_HEADS):
        lo = hd * V7X_MXU_DIM
        v_ref[0, :, lo:lo + HEAD_DIM] = v[:, hd * HEAD_DIM:(hd + 1) * HEAD_DIM].astype(BF16)
        v_ref[0, :, lo + HEAD_DIM:lo + V7X_MXU_DIM] = ones


def _inproj_kernel(x_ref, mod_ref, gain_ref, w_ref, qn_ref, kn_ref, cos_ref, sin_ref,
                   q_ref, k_ref, v_ref, f_ref, ga_ref, gf_ref):
    x = x_ref[0]
    shift = mod_ref[0, 3:4, :]
    scale = mod_ref[0, 4:5, :]
    h = (_rms(x, gain_ref[...]) * (1.0 + scale) + shift).astype(BF16)
    cos2 = cos_ref[...]
    sin2 = sin_ref[...]
    qn = qn_ref[...] * (ATTN_SCALE * LOG2_E)
    kn = kn_ref[...]
    heads_per_dot = V7X_MXU_DIM // HEAD_DIM

    def wide_dot(col):
        return _dot(h, w_ref[:, col:col + V7X_MXU_DIM])

    for c in range(Q_WIDTH // V7X_MXU_DIM):
        t = wide_dot(c * V7X_MXU_DIM)
        for j in range(heads_per_dot):
            tj = t[:, j * HEAD_DIM:(j + 1) * HEAD_DIM]
            q_ref[0, c * heads_per_dot + j] = _rope(_head_norm(tj, qn), cos2, sin2).astype(BF16)
    for c in range(KV_WIDTH // V7X_MXU_DIM):
        t = wide_dot(K_OFF + c * V7X_MXU_DIM)
        for j in range(heads_per_dot):
            lo = c * V7X_MXU_DIM + j * HEAD_DIM
            tj = t[:, j * HEAD_DIM:(j + 1) * HEAD_DIM]
            k_ref[0, :, lo:lo + HEAD_DIM] = _rope(_head_norm(tj, kn), cos2, sin2).astype(BF16)
    _store_widened_values(v_ref, _dot(h, w_ref[:, V_OFF:V_OFF + KV_WIDTH]))
    for c in range(FOURIER_WIDTH // V7X_MXU_DIM):
        lo = c * V7X_MXU_DIM
        f_ref[0, :, lo:lo + V7X_MXU_DIM] = wide_dot(F_OFF + lo).astype(BF16)
    for c in range(D_MODEL // V7X_MXU_DIM):
        lo = c * V7X_MXU_DIM
        ga_ref[0, :, lo:lo + V7X_MXU_DIM] = jax.nn.sigmoid(wide_dot(GA_OFF + lo)).astype(BF16)
        gf_ref[0, :, lo:lo + V7X_MXU_DIM] = jax.nn.sigmoid(wide_dot(GF_OFF + lo)).astype(BF16)


def _inproj(x, mod, gain, w_in, q_norm, k_norm, cos2, sin2):
    nb, t, _ = x.shape
    tm = TOKEN_TILE
    tok = lambda w: pl.BlockSpec((1, tm, w), lambda b, i: (b, i, 0))
    shp = lambda w: jax.ShapeDtypeStruct((nb, t, w), BF16)
    return pl.pallas_call(
        _inproj_kernel,
        out_shape=[jax.ShapeDtypeStruct((nb, N_Q_HEADS, t, HEAD_DIM), BF16), shp(KV_WIDTH),
                   shp(VX_WIDTH), shp(FOURIER_WIDTH), shp(D_MODEL), shp(D_MODEL)],
        grid=(nb, t // tm),
        in_specs=[
            tok(D_MODEL),
            pl.BlockSpec((1, N_MOD, D_MODEL), lambda b, i: (b, 0, 0)),
            _resident((1, D_MODEL)),
            _resident(w_in.shape),
            _resident((1, HEAD_DIM)),
            _resident((1, HEAD_DIM)),
            pl.BlockSpec((tm, HEAD_DIM), lambda b, i: (i, 0)),
            pl.BlockSpec((tm, HEAD_DIM), lambda b, i: (i, 0)),
        ],
        out_specs=[pl.BlockSpec((1, N_Q_HEADS, tm, HEAD_DIM), lambda b, i: (b, 0, i, 0)),
                   tok(KV_WIDTH), tok(VX_WIDTH), tok(FOURIER_WIDTH), tok(D_MODEL), tok(D_MODEL)],
        compiler_params=_params("parallel", "parallel"),
        name="inproj",
    )(x, mod, gain, w_in, q_norm, k_norm, cos2, sin2)


def _ctx_kv_kernel(x_ref, mod_ref, gain_ref, w_ref, kn_ref, k_ref, v_ref):
    x = x_ref[0]
    shift = mod_ref[0, 3:4, :]
    scale = mod_ref[0, 4:5, :]
    h = (_rms(x, gain_ref[...]) * (1.0 + scale) + shift).astype(BF16)
    kn = kn_ref[...]
    t = _dot(h, w_ref[:, K_OFF:K_OFF + KV_WIDTH])
    for hd in range(N_KV_HEADS):
        lo = hd * HEAD_DIM
        k_ref[0, :, lo:lo + HEAD_DIM] = _head_norm(t[:, lo:lo + HEAD_DIM], kn).astype(BF16)
    _store_widened_values(v_ref, _dot(h, w_ref[:, V_OFF:V_OFF + KV_WIDTH]))


def _ctx_kv(ctx, mod, mod_row, gain, w_in, k_norm):
    nb, t, _ = ctx.shape
    tok = lambda w: pl.BlockSpec((1, t, w), lambda b: (b, 0, 0))
    shp = lambda w: jax.ShapeDtypeStruct((nb, t, w), BF16)
    return pl.pallas_call(
        _ctx_kv_kernel,
        out_shape=[shp(KV_WIDTH), shp(VX_WIDTH)],
        grid=(nb,),
        in_specs=[
            tok(D_MODEL),
            pl.BlockSpec((1, N_MOD, D_MODEL), lambda b: (mod_row, 0, 0)),
            _resident((1, D_MODEL)),
            _resident(w_in.shape),
            _resident((1, HEAD_DIM)),
        ],
        out_specs=[tok(KV_WIDTH), tok(VX_WIDTH)],
        compiler_params=_params("parallel"),
        name="ctx_kv",
    )(ctx, mod, gain, w_in, k_norm)


_NT = (((1,), (1,)), ((), ()))


ATTN_SCRATCH_PER_HEAD = 5


def _attn_kernel(q_ref, kc_ref, k_ref, vcx_ref, vx_ref, o_ref, *scratch):
    heads = [scratch[ATTN_SCRATCH_PER_HEAD * g:ATTN_SCRATCH_PER_HEAD * (g + 1)] for g in range(GQA_GROUP)]

    @pl.when(pl.program_id(0) == 0)
    def _no_previous_item():
        for pc_ref, pl_ref, sc_ref, sl_ref, m_ref in heads:
            pc_ref[...] = jnp.ones(pc_ref.shape, BF16)
            pl_ref[...] = jnp.ones(pl_ref.shape, BF16)
            sc_ref[...] = jnp.zeros(sc_ref.shape, F32)
            sl_ref[...] = jnp.zeros(sl_ref.shape, F32)
            m_ref[...] = jnp.zeros(m_ref.shape, F32)

    def weights_from_scores(pc_ref, pl_ref, sc_ref, sl_ref, m_ref):
        m = m_ref[...]
        for s_ref, p_ref in ((sc_ref, pc_ref), (sl_ref, pl_ref)):
            for lo in range(0, s_ref.shape[1], V7X_MXU_DIM):
                p_ref[:, lo:lo + V7X_MXU_DIM] = jnp.exp2(s_ref[:, lo:lo + V7X_MXU_DIM] - m).astype(BF16)

    weights_from_scores(*heads[-1])

    for g, (pc_ref, pl_ref, sc_ref, sl_ref, m_ref) in enumerate(heads):
        oe = _dot(pc_ref[...], vcx_ref[0]) + _dot(pl_ref[...], vx_ref[0])
        o_ref[0, :, g * HEAD_DIM:(g + 1) * HEAD_DIM] = (oe[:, :HEAD_DIM] / oe[:, HEAD_DIM:]).astype(BF16)

        q = q_ref[0, g]
        s_c = lax.dot_general(q, kc_ref[0], _NT, preferred_element_type=F32)
        s_l = lax.dot_general(q, k_ref[0], _NT, preferred_element_type=F32)
        m_ref[...] = jnp.maximum(jnp.max(s_c, axis=-1, keepdims=True), jnp.max(s_l, axis=-1, keepdims=True))
        sc_ref[...] = s_c
        sl_ref[...] = s_l

    for head in heads[:-1]:
        weights_from_scores(*head)


def _attention(q, k, vx, k_c, vx_c):
    nb, _, n, _ = q.shape
    n_ctx = k_c.shape[1]
    tq = ATTN_Q_TILE
    tiles = n // tq
    items = nb * N_KV_HEADS * tiles

    def item(t):
        return t // (N_KV_HEADS * tiles), (t // tiles) % N_KV_HEADS, t % tiles

    def scored(t):
        return item(jnp.minimum(t, items - 1))

    def weighted(t):
        return item(jnp.maximum(t - 1, 0))

    def q_map(t):
        b, h, i = scored(t)
        return b, h, i, 0

    def k_map(t):
        b, h, _ = scored(t)
        return b, 0, h

    def v_map(t):
        b, h, _ = weighted(t)
        return b, 0, h

    def o_map(t):
        b, h, i = weighted(t)
        return b, i, h

    return pl.pallas_call(
        _attn_kernel,
        out_shape=jax.ShapeDtypeStruct((nb, n, Q_WIDTH), BF16),
        grid=(items + 1,),
        in_specs=[
            pl.BlockSpec((1, GQA_GROUP, tq, HEAD_DIM), q_map),
            pl.BlockSpec((1, n_ctx, HEAD_DIM), k_map),
            pl.BlockSpec((1, n, HEAD_DIM), k_map),
            pl.BlockSpec((1, n_ctx, V7X_MXU_DIM), v_map),
            pl.BlockSpec((1, n, V7X_MXU_DIM), v_map),
        ],
        out_specs=pl.BlockSpec((1, tq, GQA_GROUP * HEAD_DIM), o_map),
        scratch_shapes=GQA_GROUP * [
            pltpu.VMEM((tq, n_ctx), BF16), pltpu.VMEM((tq, n), BF16),
            pltpu.VMEM((tq, n_ctx), F32), pltpu.VMEM((tq, n), F32), pltpu.VMEM((tq, 1), F32)],
        compiler_params=_params("arbitrary"),
        name="attn",
    )(q, k_c, k, vx_c, vx)


def _dft_tables(n):
    c = FOURIER_GROUP
    scale = 1.0 / np.sqrt(float(n * c))
    mc = np.outer(np.arange(c), np.arange(c)) % c
    ang_c = 2.0 * np.pi * mc / c
    eye = np.eye(N_FOURIER_GROUPS)
    chan = np.concatenate([np.kron(eye, np.cos(ang_c) * scale),
                           np.kron(eye, np.sin(ang_c) * scale)], axis=1)
    kn = np.outer(np.arange(n), np.arange(n)) % n
    ang_n = 2.0 * np.pi * kn / n
    pos = np.concatenate([np.cos(ang_n), -np.sin(ang_n)], axis=1)
    return jnp.asarray(chan, F32).astype(BF16), jnp.asarray(pos, F32).astype(BF16)


def _fourier_kernel(f_ref, chan_ref, pos_ref, o_ref, z_ref, *, row_tile):
    n = f_ref.shape[1]
    for r in range(n // row_tile):
        lo = r * row_tile
        ab = _dot(f_ref[0, lo:lo + row_tile, :], chan_ref[...])
        z_ref[lo:lo + row_tile, :] = ab[:, :FOURIER_WIDTH].astype(BF16)
        z_ref[n + lo:n + lo + row_tile, :] = ab[:, FOURIER_WIDTH:].astype(BF16)
    for r in range(n // row_tile):
        lo = r * row_tile
        o_ref[0, lo:lo + row_tile, :] = _dot(pos_ref[lo:lo + row_tile, :], z_ref[...]).astype(BF16)


def _fourier(f):
    nb, n, _ = f.shape
    chan, pos = _dft_tables(n)
    blk = pl.BlockSpec((1, n, FOURIER_WIDTH), lambda b: (b, 0, 0))
    return pl.pallas_call(
        functools.partial(_fourier_kernel, row_tile=V7X_MXU_DIM),
        out_shape=jax.ShapeDtypeStruct(f.shape, BF16),
        grid=(nb,),
        in_specs=[blk, _resident(chan.shape), _resident(pos.shape)],
        out_specs=blk,
        scratch_shapes=[pltpu.VMEM((2 * n, FOURIER_WIDTH), BF16)],
        compiler_params=_params("parallel"),
        name="fourier",
    )(f, chan, pos)


def _merge_kernel(x_ref, mod_ref, ya_ref, yf_ref, ga_ref, gf_ref, wab_ref, wfb_ref, wo_ref, o_ref):
    gate = mod_ref[0, 5:6, :]
    merged = (ga_ref[0].astype(F32) * _dot(ya_ref[0], wab_ref[...])
              + gf_ref[0].astype(F32) * _dot(yf_ref[0], wfb_ref[...]))
    mix = _dot(merged.astype(BF16), wo_ref[...])
    o_ref[0] = x_ref[0] + gate * mix


def _merge(x, mod, y_attn, y_four, sig_a, sig_f, w_ab, w_fb, w_o):
    nb, t, _ = x.shape
    tm = TOKEN_TILE
    tok = lambda w: pl.BlockSpec((1, tm, w), lambda b, i: (b, i, 0))
    return pl.pallas_call(
        _merge_kernel,
        out_shape=jax.ShapeDtypeStruct(x.shape, F32),
        grid=(nb, t // tm),
        in_specs=[
            tok(D_MODEL),
            pl.BlockSpec((1, N_MOD, D_MODEL), lambda b, i: (b, 0, 0)),
            tok(Q_WIDTH), tok(FOURIER_WIDTH), tok(D_MODEL), tok(D_MODEL),
            _resident(w_ab.shape), _resident(w_fb.shape), _resident(w_o.shape),
        ],
        out_specs=tok(D_MODEL),
        compiler_params=_params("parallel", "parallel"),
        name="merge",
    )(x, mod, y_attn, y_four, sig_a, sig_f, w_ab, w_fb, w_o)


def _rope_tables(n_tokens):
    rows = n_tokens // GRID_W
    row_ids = jnp.repeat(jnp.arange(rows, dtype=F32), GRID_W)
    col_ids = jnp.tile(jnp.arange(GRID_W, dtype=F32), rows)
    inv_freq = ROPE_THETA ** (-jnp.arange(0, AXIS_ROPE_DIM, 2, dtype=F32) / AXIS_ROPE_DIM)
    ang = jnp.concatenate([row_ids[:, None] * inv_freq, col_ids[:, None] * inv_freq], axis=-1)
    cos, sin = jnp.cos(ang), jnp.sin(ang)
    return jnp.concatenate([cos, cos], axis=-1), jnp.concatenate([-sin, sin], axis=-1)


def kernel(x, c, ctx, c_ctx, w_ada, b_ada, norm_ffn1, w_ffn1_in, w_ffn1_out, norm_mix, w_in,
           q_norm, k_norm, w_attn_branch, w_fourier_branch, w_out, norm_ffn2, w_ffn2_in, w_ffn2_out):
    nb, n_lat, _ = x.shape
    n_ctx = ctx.shape[1]
    depth = w_ada.shape[0]
    assert depth == 1, "the context stream is only carried through the (single) last layer"
    cos2, sin2 = _rope_tables(n_lat)
    ctx_row = nb
    pad_rows = -(nb + 1) % 8
    c_all = jnp.concatenate([c, c_ctx[None, :], jnp.zeros((pad_rows, D_MODEL), F32)], axis=0)

    i = 0
    mod = _adaln(c_all, w_ada[i], b_ada[i][None, :]).reshape(c_all.shape[0], N_MOD, D_MODEL)
    w1_in, w1_out = w_ffn1_in[i].astype(BF16), w_ffn1_out[i].astype(BF16)
    w2_in, w2_out = w_ffn2_in[i].astype(BF16), w_ffn2_out[i].astype(BF16)
    w_in_b = w_in[i].astype(BF16)

    x = _ffn(x, mod, lambda b: b, 0, norm_ffn1[i][None, :], w1_in, w1_out)
    ctx_flat = _ffn(ctx.reshape(1, nb * n_ctx, D_MODEL), mod, lambda b: ctx_row, 0,
                    norm_ffn1[i][None, :], w1_in, w1_out)

    q, k, vx, f, sig_a, sig_f = _inproj(x, mod, norm_mix[i][None, :], w_in_b,
                                        q_norm[i][None, :], k_norm[i][None, :], cos2, sin2)
    k_c, vx_c = _ctx_kv(ctx_flat.reshape(nb, n_ctx, D_MODEL), mod, ctx_row, norm_mix[i][None, :],
                        w_in_b, k_norm[i][None, :])
    y_attn = _attention(q, k, vx, k_c, vx_c)
    y_four = _fourier(f)
    x = _merge(x, mod, y_attn, y_four, sig_a, sig_f, w_attn_branch[i].astype(BF16),
               w_fourier_branch[i].astype(BF16), w_out[i].astype(BF16))

    return _ffn(x, mod, lambda b: b, 6, norm_ffn2[i][None, :], w2_in, w2_out)
```

```python
import functools

import numpy as np
import jax
import jax.numpy as jnp
from jax import lax
from jax.experimental import pallas as pl
from jax.experimental.pallas import tpu as pltpu

D_MODEL = 1024
GRID_W = 64
HEAD_DIM = 128
N_Q_HEADS = D_MODEL // HEAD_DIM
N_KV_HEADS = N_Q_HEADS // 4
GQA_GROUP = N_Q_HEADS // N_KV_HEADS
Q_WIDTH = N_Q_HEADS * HEAD_DIM
KV_WIDTH = N_KV_HEADS * HEAD_DIM
FOURIER_GROUP = 128
N_FOURIER_GROUPS = 4
FOURIER_WIDTH = N_FOURIER_GROUPS * FOURIER_GROUP
D_FF = 2816
AXIS_ROPE_DIM = HEAD_DIM // 2
ROPE_THETA = 10000.0
EPS = 1e-6
N_MOD = 9
ATTN_SCALE = HEAD_DIM ** -0.5
LOG2_E = 1.4426950408889634

K_OFF = Q_WIDTH
V_OFF = K_OFF + KV_WIDTH
F_OFF = V_OFF + KV_WIDTH
GA_OFF = F_OFF + FOURIER_WIDTH
GF_OFF = GA_OFF + D_MODEL

V7X_MXU_DIM = 256
VX_WIDTH = N_KV_HEADS * V7X_MXU_DIM
V7X_VMEM_BYTES = 64 * 2**20
VMEM_LIMIT = V7X_VMEM_BYTES - 8 * 2**20

TOKEN_TILE = 512
FF_CHUNK = V7X_MXU_DIM
ATTN_Q_TILE = 512
ADALN_TILE = 1152

F32 = jnp.float32
BF16 = jnp.bfloat16


def _params(*sem):
    return pltpu.CompilerParams(dimension_semantics=sem, vmem_limit_bytes=VMEM_LIMIT)


def _resident(shape):
    return pl.BlockSpec(shape, lambda *_: (0,) * len(shape), pipeline_mode=pl.Buffered(1))


def _rms(x, gain):
    return x * lax.rsqrt(jnp.mean(x * x, axis=-1, keepdims=True) + EPS) * gain


def _dot(a, b):
    return jnp.dot(a, b, preferred_element_type=F32)


def _adaln_kernel(c_ref, w_ref, b_ref, o_ref):
    c = c_ref[...]
    a = (c * jax.nn.sigmoid(c)).astype(BF16)
    o_ref[...] = _dot(a, w_ref[...].astype(BF16)) + b_ref[...]


def _adaln(c_all, w, b):
    rows, n_out = c_all.shape[0], w.shape[1]
    return pl.pallas_call(
        _adaln_kernel,
        out_shape=jax.ShapeDtypeStruct((rows, n_out), F32),
        grid=(n_out // ADALN_TILE,),
        in_specs=[
            pl.BlockSpec((rows, D_MODEL), lambda j: (0, 0)),
            pl.BlockSpec((D_MODEL, ADALN_TILE), lambda j: (0, j)),
            pl.BlockSpec((1, ADALN_TILE), lambda j: (0, j)),
        ],
        out_specs=pl.BlockSpec((rows, ADALN_TILE), lambda j: (0, j)),
        compiler_params=_params("arbitrary"),
        name="adaln",
    )(c_all, w, b)


def _ffn_kernel(x_ref, mod_ref, gain_ref, win_ref, wout_ref, o_ref, act_ref, *, mod_base):
    x = x_ref[0]
    shift = mod_ref[0, mod_base:mod_base + 1, :]
    scale = mod_ref[0, mod_base + 1:mod_base + 2, :]
    gate = mod_ref[0, mod_base + 2:mod_base + 3, :]
    h = (_rms(x, gain_ref[...]) * (1.0 + scale) + shift).astype(BF16)
    for j in range(D_FF // FF_CHUNK):
        lo = j * FF_CHUNK
        g = _dot(h, win_ref[:, lo:lo + FF_CHUNK])
        u = _dot(h, win_ref[:, D_FF + lo:D_FF + lo + FF_CHUNK])
        act_ref[:, lo:lo + FF_CHUNK] = (g * jax.nn.sigmoid(g) * u).astype(BF16)
    y = _dot(act_ref[...], wout_ref[...])
    o_ref[0] = x + 0.5 * gate * y


def _ffn(x, mod, mod_row, mod_base, gain, w_in, w_out):
    nb, t, _ = x.shape
    tm = min(TOKEN_TILE, t)
    return pl.pallas_call(
        functools.partial(_ffn_kernel, mod_base=mod_base),
        out_shape=jax.ShapeDtypeStruct(x.shape, F32),
        grid=(nb, t // tm),
        in_specs=[
            pl.BlockSpec((1, tm, D_MODEL), lambda b, i: (b, i, 0)),
            pl.BlockSpec((1, N_MOD, D_MODEL), lambda b, i: (mod_row(b), 0, 0)),
            _resident((1, D_MODEL)),
            _resident((D_MODEL, 2 * D_FF)),
            _resident((D_FF, D_MODEL)),
        ],
        out_specs=pl.BlockSpec((1, tm, D_MODEL), lambda b, i: (b, i, 0)),
        scratch_shapes=[pltpu.VMEM((tm, D_FF), BF16)],
        compiler_params=_params("parallel", "parallel"),
        name="ffn",
    )(x, mod, gain, w_in, w_out)


def _head_norm(t, gain):
    return t * lax.rsqrt(jnp.mean(t * t, axis=-1, keepdims=True) + EPS) * gain


def _rope(t, cos2, sin2):
    return t * cos2 + pltpu.roll(t, AXIS_ROPE_DIM, 1) * sin2


def _store_widened_values(v_ref, v):
    ones = jnp.ones((v.shape[0], V7X_MXU_DIM - HEAD_DIM), BF16)
    for hd in range(N_KV_HEADS):
        lo = hd * V7X_MXU_DIM
        v_ref[0, :, lo:lo + HEAD_DIM] = v[:, hd * HEAD_DIM:(hd + 1) * HEAD_DIM].astype(BF16)
        v_ref[0, :, lo + HEAD_DIM:lo + V7X_MXU_DIM] = ones


def _inproj_kernel(x_ref, mod_ref, gain_ref, w_ref, qn_ref, kn_ref, cos_ref, sin_ref,
                   q_ref, k_ref, v_ref, f_ref, ga_ref, gf_ref):
    x = x_ref[0]
    shift = mod_ref[0, 3:4, :]
    scale = mod_ref[0, 4:5, :]
    h = (_rms(x, gain_ref[...]) * (1.0 + scale) + shift).astype(BF16)
    cos2 = cos_ref[...]
    sin2 = sin_ref[...]
    qn = qn_ref[...] * (ATTN_SCALE * LOG2_E)
    kn = kn_ref[...]
    heads_per_dot = V7X_MXU_DIM // HEAD_DIM

    def wide_dot(col):
        return _dot(h, w_ref[:, col:col + V7X_MXU_DIM])

    for c in range(Q_WIDTH // V7X_MXU_DIM):
        t = wide_dot(c * V7X_MXU_DIM)
        for j in range(heads_per_dot):
            tj = t[:, j * HEAD_DIM:(j + 1) * HEAD_DIM]
            q_ref[0, c * heads_per_dot + j] = _rope(_head_norm(tj, qn), cos2, sin2).astype(BF16)
    for c in range(KV_WIDTH // V7X_MXU_DIM):
        t = wide_dot(K_OFF + c * V7X_MXU_DIM)
        for j in range(heads_per_dot):
            lo = c * V7X_MXU_DIM + j * HEAD_DIM
            tj = t[:, j * HEAD_DIM:(j + 1) * HEAD_DIM]
            k_ref[0, :, lo:lo + HEAD_DIM] = _rope(_head_norm(tj, kn), cos2, sin2).astype(BF16)
    _store_widened_values(v_ref, _dot(h, w_ref[:, V_OFF:V_OFF + KV_WIDTH]))
    for c in range(FOURIER_WIDTH // V7X_MXU_DIM):
        lo = c * V7X_MXU_DIM
        f_ref[0, :, lo:lo + V7X_MXU_DIM] = wide_dot(F_OFF + lo).astype(BF16)
    for c in range(D_MODEL // V7X_MXU_DIM):
        lo = c * V7X_MXU_DIM
        ga_ref[0, :, lo:lo + V7X_MXU_DIM] = jax.nn.sigmoid(wide_dot(GA_OFF + lo)).astype(BF16)
        gf_ref[0, :, lo:lo + V7X_MXU_DIM] = jax.nn.sigmoid(wide_dot(GF_OFF + lo)).astype(BF16)


def _inproj(x, mod, gain, w_in, q_norm, k_norm, cos2, sin2):
    nb, t, _ = x.shape
    tm = TOKEN_TILE
    tok = lambda w: pl.BlockSpec((1, tm, w), lambda b, i: (b, i, 0))
    shp = lambda w: jax.ShapeDtypeStruct((nb, t, w), BF16)
    return pl.pallas_call(
        _inproj_kernel,
        out_shape=[jax.ShapeDtypeStruct((nb, N_Q_HEADS, t, HEAD_DIM), BF16), shp(KV_WIDTH),
                   shp(VX_WIDTH), shp(FOURIER_WIDTH), shp(D_MODEL), shp(D_MODEL)],
        grid=(nb, t // tm),
        in_specs=[
            tok(D_MODEL),
            pl.BlockSpec((1, N_MOD, D_MODEL), lambda b, i: (b, 0, 0)),
            _resident((1, D_MODEL)),
            _resident(w_in.shape),
            _resident((1, HEAD_DIM)),
            _resident((1, HEAD_DIM)),
            pl.BlockSpec((tm, HEAD_DIM), lambda b, i: (i, 0)),
            pl.BlockSpec((tm, HEAD_DIM), lambda b, i: (i, 0)),
        ],
        out_specs=[pl.BlockSpec((1, N_Q_HEADS, tm, HEAD_DIM), lambda b, i: (b, 0, i, 0)),
                   tok(KV_WIDTH), tok(VX_WIDTH), tok(FOURIER_WIDTH), tok(D_MODEL), tok(D_MODEL)],
        compiler_params=_params("parallel", "parallel"),
        name="inproj",
    )(x, mod, gain, w_in, q_norm, k_norm, cos2, sin2)


def _ctx_kv_kernel(x_ref, mod_ref, gain_ref, w_ref, kn_ref, k_ref, v_ref):
    x = x_ref[0]
    shift = mod_ref[0, 3:4, :]
    scale = mod_ref[0, 4:5, :]
    h = (_rms(x, gain_ref[...]) * (1.0 + scale) + shift).astype(BF16)
    kn = kn_ref[...]
    t = _dot(h, w_ref[:, K_OFF:K_OFF + KV_WIDTH])
    for hd in range(N_KV_HEADS):
        lo = hd * HEAD_DIM
        k_ref[0, :, lo:lo + HEAD_DIM] = _head_norm(t[:, lo:lo + HEAD_DIM], kn).astype(BF16)
    _store_widened_values(v_ref, _dot(h, w_ref[:, V_OFF:V_OFF + KV_WIDTH]))


def _ctx_kv(ctx, mod, mod_row, gain, w_in, k_norm):
    nb, t, _ = ctx.shape
    tok = lambda w: pl.BlockSpec((1, t, w), lambda b: (b, 0, 0))
    shp = lambda w: jax.ShapeDtypeStruct((nb, t, w), BF16)
    return pl.pallas_call(
        _ctx_kv_kernel,
        out_shape=[shp(KV_WIDTH), shp(VX_WIDTH)],
        grid=(nb,),
        in_specs=[
            tok(D_MODEL),
            pl.BlockSpec((1, N_MOD, D_MODEL), lambda b: (mod_row, 0, 0)),
            _resident((1, D_MODEL)),
            _resident(w_in.shape),
            _resident((1, HEAD_DIM)),
        ],
        out_specs=[tok(KV_WIDTH), tok(VX_WIDTH)],
        compiler_params=_params("parallel"),
        name="ctx_kv",
    )(ctx, mod, gain, w_in, k_norm)


_NT = (((1,), (1,)), ((), ()))


ATTN_SCRATCH_PER_HEAD = 5


def _attn_kernel(q_ref, kc_ref, k_ref, vcx_ref, vx_ref, o_ref, *scratch):
    heads = [scratch[ATTN_SCRATCH_PER_HEAD * g:ATTN_SCRATCH_PER_HEAD * (g + 1)] for g in range(GQA_GROUP)]

    @pl.when(pl.program_id(0) == 0)
    def _no_previous_item():
        for pc_ref, pl_ref, sc_ref, sl_ref, m_ref in heads:
            pc_ref[...] = jnp.ones(pc_ref.shape, BF16)
            pl_ref[...] = jnp.ones(pl_ref.shape, BF16)
            sc_ref[...] = jnp.zeros(sc_ref.shape, F32)
            sl_ref[...] = jnp.zeros(sl_ref.shape, F32)
            m_ref[...] = jnp.zeros(m_ref.shape, F32)

    def weights_from_scores(pc_ref, pl_ref, sc_ref, sl_ref, m_ref):
        m = m_ref[...]
        for s_ref, p_ref in ((sc_ref, pc_ref), (sl_ref, pl_ref)):
            for lo in range(0, s_ref.shape[1], V7X_MXU_DIM):
                p_ref[:, lo:lo + V7X_MXU_DIM] = jnp.exp2(s_ref[:, lo:lo + V7X_MXU_DIM] - m).astype(BF16)

    weights_from_scores(*heads[-1])

    for g, (pc_ref, pl_ref, sc_ref, sl_ref, m_ref) in enumerate(heads):
        oe = _dot(pc_ref[...], vcx_ref[0]) + _dot(pl_ref[...], vx_ref[0])
        o_ref[0, :, g * HEAD_DIM:(g + 1) * HEAD_DIM] = (oe[:, :HEAD_DIM] / oe[:, HEAD_DIM:]).astype(BF16)

        q = q_ref[0, g]
        s_c = lax.dot_general(q, kc_ref[0], _NT, preferred_element_type=F32)
        s_l = lax.dot_general(q, k_ref[0], _NT, preferred_element_type=F32)
        m_ref[...] = jnp.maximum(jnp.max(s_c, axis=-1, keepdims=True), jnp.max(s_l, axis=-1, keepdims=True))
        sc_ref[...] = s_c
        sl_ref[...] = s_l

    for head in heads[:-1]:
        weights_from_scores(*head)


def _attention(q, k, vx, k_c, vx_c):
    nb, _, n, _ = q.shape
    n_ctx = k_c.shape[1]
    tq = ATTN_Q_TILE
    tiles = n // tq
    items = nb * N_KV_HEADS * tiles

    def item(t):
        return t // (N_KV_HEADS * tiles), (t // tiles) % N_KV_HEADS, t % tiles

    def scored(t):
        return item(jnp.minimum(t, items - 1))

    def weighted(t):
        return item(jnp.maximum(t - 1, 0))

    def q_map(t):
        b, h, i = scored(t)
        return b, h, i, 0

    def k_map(t):
        b, h, _ = scored(t)
        return b, 0, h

    def v_map(t):
        b, h, _ = weighted(t)
        return b, 0, h

    def o_map(t):
        b, h, i = weighted(t)
        return b, i, h

    return pl.pallas_call(
        _attn_kernel,
        out_shape=jax.ShapeDtypeStruct((nb, n, Q_WIDTH), BF16),
        grid=(items + 1,),
        in_specs=[
            pl.BlockSpec((1, GQA_GROUP, tq, HEAD_DIM), q_map),
            pl.BlockSpec((1, n_ctx, HEAD_DIM), k_map),
            pl.BlockSpec((1, n, HEAD_DIM), k_map),
            pl.BlockSpec((1, n_ctx, V7X_MXU_DIM), v_map),
            pl.BlockSpec((1, n, V7X_MXU_DIM), v_map),
        ],
        out_specs=pl.BlockSpec((1, tq, GQA_GROUP * HEAD_DIM), o_map),
        scratch_shapes=GQA_GROUP * [
            pltpu.VMEM((tq, n_ctx), BF16), pltpu.VMEM((tq, n), BF16),
            pltpu.VMEM((tq, n_ctx), F32), pltpu.VMEM((tq, n), F32), pltpu.VMEM((tq, 1), F32)],
        compiler_params=_params("arbitrary"),
        name="attn",
    )(q, k_c, k, vx_c, vx)


DFT_RADIX = 8


def _dft_tables(n):
    r = n // DFT_RADIX
    k = np.arange(DFT_RADIX)[:, None, None] + DFT_RADIX * np.arange(r)[None, :, None]
    ang = 2.0 * np.pi * ((k * np.arange(r)[None, None, :]) % n) / n
    cos, sin = np.cos(ang), np.sin(ang)
    pos = np.concatenate([np.concatenate([cos, sin], axis=2),
                          np.concatenate([-sin, cos], axis=2)], axis=1)
    c = FOURIER_GROUP
    scale = 1.0 / np.sqrt(float(n * c))
    ang_c = 2.0 * np.pi * (np.outer(np.arange(c), np.arange(c)) % c) / c
    eye = np.eye(N_FOURIER_GROUPS)
    chan = np.concatenate([np.kron(eye, np.cos(ang_c) * scale),
                           np.kron(eye, np.sin(ang_c) * scale)], axis=0)
    return jnp.asarray(pos, F32).astype(BF16), jnp.asarray(chan, F32).astype(BF16)


def _fold_radix8(x):
    c = 0.5 ** 0.5
    e02p, e02m, e13p, e13m = x[0] + x[4], x[0] - x[4], x[2] + x[6], x[2] - x[6]
    o02p, o02m, o13p, o13m = x[1] + x[5], x[1] - x[5], x[3] + x[7], x[3] - x[7]
    e0, e2, o0, o2 = e02p + e13p, e02p - e13p, o02p + o13p, o02p - o13p
    p, q = c * (o02m - o13m), c * (o02m + o13m)
    re = [e0 + o0, e02m + p, e2, e02m - p, e0 - o0, e02m - p, e2, e02m + p]
    im1, im3 = -e13m - q, e13m - q
    im = [None, im1, -o2, im3, None, -im3, o2, -im1]
    return re, im


def _fourier_kernel(f_ref, pos_ref, chan_ref, o_ref, t_ref, y_ref):
    r = f_ref.shape[1] // DFT_RADIX
    w = FOURIER_WIDTH
    lane_chunk = 128
    for lo in range(0, w, lane_chunk):
        x = [f_ref[0, n1 * r:(n1 + 1) * r, lo:lo + lane_chunk].astype(F32) for n1 in range(DFT_RADIX)]
        re, im = _fold_radix8(x)
        for k1 in range(DFT_RADIX):
            t_ref[k1, 0:r, lo:lo + lane_chunk] = re[k1].astype(BF16)
            t_ref[k1, r:2 * r, lo:lo + lane_chunk] = (
                jnp.zeros((r, lane_chunk), BF16) if im[k1] is None else im[k1].astype(BF16))
    for k1 in range(DFT_RADIX):
        z = _dot(pos_ref[k1], t_ref[k1])
        y = (_dot(z[0:r].astype(BF16), chan_ref[0:w, :])
             + _dot(z[r:2 * r].astype(BF16), chan_ref[w:2 * w, :]))
        for j in range(w // lane_chunk):
            y_ref[j, pl.ds(k1, r, stride=DFT_RADIX), :] = y[:, j * lane_chunk:(j + 1) * lane_chunk]
    for j in range(w // lane_chunk):
        o_ref[0, :, j * lane_chunk:(j + 1) * lane_chunk] = y_ref[j].astype(BF16)


def _fourier(f):
    nb, n, w = f.shape
    pos, chan = _dft_tables(n)
    blk = pl.BlockSpec((1, n, w), lambda b: (b, 0, 0))
    return pl.pallas_call(
        _fourier_kernel,
        out_shape=jax.ShapeDtypeStruct(f.shape, BF16),
        grid=(nb,),
        in_specs=[blk, _resident(pos.shape), _resident(chan.shape)],
        out_specs=blk,
        scratch_shapes=[pltpu.VMEM((DFT_RADIX, 2 * n // DFT_RADIX, w), BF16),
                        pltpu.VMEM((w // 128, n, 128), F32)],
        compiler_params=_params("parallel"),
        name="fourier",
    )(f, pos, chan)


def _merge_kernel(x_ref, mod_ref, ya_ref, yf_ref, ga_ref, gf_ref, wab_ref, wfb_ref, wo_ref, o_ref):
    gate = mod_ref[0, 5:6, :]
    merged = (ga_ref[0].astype(F32) * _dot(ya_ref[0], wab_ref[...])
              + gf_ref[0].astype(F32) * _dot(yf_ref[0], wfb_ref[...]))
    mix = _dot(merged.astype(BF16), wo_ref[...])
    o_ref[0] = x_ref[0] + gate * mix


def _merge(x, mod, y_attn, y_four, sig_a, sig_f, w_ab, w_fb, w_o):
    nb, t, _ = x.shape
    tm = TOKEN_TILE
    tok = lambda w: pl.BlockSpec((1, tm, w), lambda b, i: (b, i, 0))
    return pl.pallas_call(
        _merge_kernel,
        out_shape=jax.ShapeDtypeStruct(x.shape, F32),
        grid=(nb, t // tm),
        in_specs=[
            tok(D_MODEL),
            pl.BlockSpec((1, N_MOD, D_MODEL), lambda b, i: (b, 0, 0)),
            tok(Q_WIDTH), tok(FOURIER_WIDTH), tok(D_MODEL), tok(D_MODEL),
            _resident(w_ab.shape), _resident(w_fb.shape), _resident(w_o.shape),
        ],
        out_specs=tok(D_MODEL),
        compiler_params=_params("parallel", "parallel"),
        name="merge",
    )(x, mod, y_attn, y_four, sig_a, sig_f, w_ab, w_fb, w_o)


def _rope_tables(n_tokens):
    rows = n_tokens // GRID_W
    row_ids = jnp.repeat(jnp.arange(rows, dtype=F32), GRID_W)
    col_ids = jnp.tile(jnp.arange(GRID_W, dtype=F32), rows)
    inv_freq = ROPE_THETA ** (-jnp.arange(0, AXIS_ROPE_DIM, 2, dtype=F32) / AXIS_ROPE_DIM)
    ang = jnp.concatenate([row_ids[:, None] * inv_freq, col_ids[:, None] * inv_freq], axis=-1)
    cos, sin = jnp.cos(ang), jnp.sin(ang)
    return jnp.concatenate([cos, cos], axis=-1), jnp.concatenate([-sin, sin], axis=-1)


def kernel(x, c, ctx, c_ctx, w_ada, b_ada, norm_ffn1, w_ffn1_in, w_ffn1_out, norm_mix, w_in,
           q_norm, k_norm, w_attn_branch, w_fourier_branch, w_out, norm_ffn2, w_ffn2_in, w_ffn2_out):
    nb, n_lat, _ = x.shape
    n_ctx = ctx.shape[1]
    depth = w_ada.shape[0]
    assert depth == 1, "the context stream is only carried through the (single) last layer"
    cos2, sin2 = _rope_tables(n_lat)
    ctx_row = nb
    pad_rows = -(nb + 1) % 8
    c_all = jnp.concatenate([c, c_ctx[None, :], jnp.zeros((pad_rows, D_MODEL), F32)], axis=0)

    i = 0
    mod = _adaln(c_all, w_ada[i], b_ada[i][None, :]).reshape(c_all.shape[0], N_MOD, D_MODEL)
    w1_in, w1_out = w_ffn1_in[i].astype(BF16), w_ffn1_out[i].astype(BF16)
    w2_in, w2_out = w_ffn2_in[i].astype(BF16), w_ffn2_out[i].astype(BF16)
    w_in_b = w_in[i].astype(BF16)

    x = _ffn(x, mod, lambda b: b, 0, norm_ffn1[i][None, :], w1_in, w1_out)
    ctx_flat = _ffn(ctx.reshape(1, nb * n_ctx, D_MODEL), mod, lambda b: ctx_row, 0,
                    norm_ffn1[i][None, :], w1_in, w1_out)

    q, k, vx, f, sig_a, sig_f = _inproj(x, mod, norm_mix[i][None, :], w_in_b,
                                        q_norm[i][None, :], k_norm[i][None, :], cos2, sin2)
    k_c, vx_c = _ctx_kv(ctx_flat.reshape(nb, n_ctx, D_MODEL), mod, ctx_row, norm_mix[i][None, :],
                        w_in_b, k_norm[i][None, :])
    y_attn = _attention(q, k, vx, k_c, vx_c)
    y_four = _fourier(f)
    x = _merge(x, mod, y_attn, y_four, sig_a, sig_f, w_attn_branch[i].astype(BF16),
               w_fourier_branch[i].astype(BF16), w_out[i].astype(BF16))

    return _ffn(x, mod, lambda b: b, 6, norm_ffn2[i][None, :], w2_in, w2_out)
```

```python
import functools

import numpy as np
import jax
import jax.numpy as jnp
from jax import lax
from jax.experimental import pallas as pl
from jax.experimental.pallas import tpu as pltpu

D_MODEL = 1024
GRID_W = 64
HEAD_DIM = 128
N_Q_HEADS = D_MODEL // HEAD_DIM
N_KV_HEADS = N_Q_HEADS // 4
GQA_GROUP = N_Q_HEADS // N_KV_HEADS
Q_WIDTH = N_Q_HEADS * HEAD_DIM
KV_WIDTH = N_KV_HEADS * HEAD_DIM
FOURIER_GROUP = 128
N_FOURIER_GROUPS = 4
FOURIER_WIDTH = N_FOURIER_GROUPS * FOURIER_GROUP
D_FF = 2816
AXIS_ROPE_DIM = HEAD_DIM // 2
ROPE_THETA = 10000.0
EPS = 1e-6
N_MOD = 9
ATTN_SCALE = HEAD_DIM ** -0.5
LOG2_E = 1.4426950408889634

K_OFF = Q_WIDTH
V_OFF = K_OFF + KV_WIDTH
F_OFF = V_OFF + KV_WIDTH
GA_OFF = F_OFF + FOURIER_WIDTH
GF_OFF = GA_OFF + D_MODEL

V7X_MXU_DIM = 256
VX_WIDTH = N_KV_HEADS * V7X_MXU_DIM
V7X_VMEM_BYTES = 64 * 2**20
VMEM_LIMIT = V7X_VMEM_BYTES - 8 * 2**20

TOKEN_TILE = 512
FF_CHUNK = V7X_MXU_DIM
ATTN_Q_TILE = 512
ADALN_TILE = 1152

F32 = jnp.float32
BF16 = jnp.bfloat16


def _params(*sem):
    return pltpu.CompilerParams(dimension_semantics=sem, vmem_limit_bytes=VMEM_LIMIT)


def _resident(shape):
    return pl.BlockSpec(shape, lambda *_: (0,) * len(shape), pipeline_mode=pl.Buffered(1))


def _rms(x, gain):
    return x * lax.rsqrt(jnp.mean(x * x, axis=-1, keepdims=True) + EPS) * gain


def _dot(a, b):
    return jnp.dot(a, b.astype(BF16), preferred_element_type=F32)


def _adaln_kernel(c_ref, w_ref, b_ref, o_ref):
    c = c_ref[...]
    a = (c * jax.nn.sigmoid(c)).astype(BF16)
    o_ref[...] = _dot(a, w_ref[...]) + b_ref[...]


def _adaln(c_all, w, b):
    rows, n_out = c_all.shape[0], w.shape[1]
    return pl.pallas_call(
        _adaln_kernel,
        out_shape=jax.ShapeDtypeStruct((rows, n_out), F32),
        grid=(n_out // ADALN_TILE,),
        in_specs=[
            pl.BlockSpec((rows, D_MODEL), lambda j: (0, 0)),
            pl.BlockSpec((D_MODEL, ADALN_TILE), lambda j: (0, j)),
            pl.BlockSpec((1, ADALN_TILE), lambda j: (0, j)),
        ],
        out_specs=pl.BlockSpec((rows, ADALN_TILE), lambda j: (0, j)),
        compiler_params=_params("arbitrary"),
        name="adaln",
    )(c_all, w, b)


def _ffn_kernel(x_ref, mod_ref, gain_ref, win_ref, wout_ref, o_ref, act_ref, *, mod_base):
    x = x_ref[0]
    shift = mod_ref[0, mod_base:mod_base + 1, :]
    scale = mod_ref[0, mod_base + 1:mod_base + 2, :]
    gate = mod_ref[0, mod_base + 2:mod_base + 3, :]
    h = (_rms(x, gain_ref[...]) * (1.0 + scale) + shift).astype(BF16)
    for j in range(D_FF // FF_CHUNK):
        lo = j * FF_CHUNK
        g = _dot(h, win_ref[:, lo:lo + FF_CHUNK])
        u = _dot(h, win_ref[:, D_FF + lo:D_FF + lo + FF_CHUNK])
        act_ref[:, lo:lo + FF_CHUNK] = (g * jax.nn.sigmoid(g) * u).astype(BF16)
    y = _dot(act_ref[...], wout_ref[...])
    o_ref[0] = x + 0.5 * gate * y


def _ffn(x, mod, mod_row, mod_base, gain, w_in, w_out):
    nb, t, _ = x.shape
    tm = min(TOKEN_TILE, t)
    return pl.pallas_call(
        functools.partial(_ffn_kernel, mod_base=mod_base),
        out_shape=jax.ShapeDtypeStruct(x.shape, F32),
        grid=(nb, t // tm),
        in_specs=[
            pl.BlockSpec((1, tm, D_MODEL), lambda b, i: (b, i, 0)),
            pl.BlockSpec((1, N_MOD, D_MODEL), lambda b, i: (mod_row(b), 0, 0)),
            _resident((1, D_MODEL)),
            _resident((D_MODEL, 2 * D_FF)),
            _resident((D_FF, D_MODEL)),
        ],
        out_specs=pl.BlockSpec((1, tm, D_MODEL), lambda b, i: (b, i, 0)),
        scratch_shapes=[pltpu.VMEM((tm, D_FF), BF16)],
        compiler_params=_params("parallel", "parallel"),
        name="ffn",
    )(x, mod, gain, w_in, w_out)


def _head_norm(t, gain):
    return t * lax.rsqrt(jnp.mean(t * t, axis=-1, keepdims=True) + EPS) * gain


def _rope(t, cos2, sin2):
    return t * cos2 + pltpu.roll(t, AXIS_ROPE_DIM, 1) * sin2


def _store_widened_values(v_ref, v):
    ones = jnp.ones((v.shape[0], V7X_MXU_DIM - HEAD_DIM), BF16)
    for hd in range(N_KV_HEADS):
        lo = hd * V7X_MXU_DIM
        v_ref[0, :, lo:lo + HEAD_DIM] = v[:, hd * HEAD_DIM:(hd + 1) * HEAD_DIM].astype(BF16)
        v_ref[0, :, lo + HEAD_DIM:lo + V7X_MXU_DIM] = ones


def _inproj_kernel(x_ref, mod_ref, gain_ref, w_ref, qn_ref, kn_ref, cos_ref, sin_ref,
                   q_ref, k_ref, v_ref, f_ref, ga_ref, gf_ref):
    x = x_ref[0]
    shift = mod_ref[0, 3:4, :]
    scale = mod_ref[0, 4:5, :]
    h = (_rms(x, gain_ref[...]) * (1.0 + scale) + shift).astype(BF16)
    cos2 = cos_ref[...]
    sin2 = sin_ref[...]
    qn = qn_ref[...] * (ATTN_SCALE * LOG2_E)
    kn = kn_ref[...]
    heads_per_dot = V7X_MXU_DIM // HEAD_DIM

    def wide_dot(col):
        return _dot(h, w_ref[:, col:col + V7X_MXU_DIM])

    for c in range(Q_WIDTH // V7X_MXU_DIM):
        t = wide_dot(c * V7X_MXU_DIM)
        for j in range(heads_per_dot):
            tj = t[:, j * HEAD_DIM:(j + 1) * HEAD_DIM]
            q_ref[0, c * heads_per_dot + j] = _rope(_head_norm(tj, qn), cos2, sin2).astype(BF16)
    for c in range(KV_WIDTH // V7X_MXU_DIM):
        t = wide_dot(K_OFF + c * V7X_MXU_DIM)
        for j in range(heads_per_dot):
            lo = c * V7X_MXU_DIM + j * HEAD_DIM
            tj = t[:, j * HEAD_DIM:(j + 1) * HEAD_DIM]
            k_ref[0, :, lo:lo + HEAD_DIM] = _rope(_head_norm(tj, kn), cos2, sin2).astype(BF16)
    _store_widened_values(v_ref, _dot(h, w_ref[:, V_OFF:V_OFF + KV_WIDTH]))
    for c in range(FOURIER_WIDTH // V7X_MXU_DIM):
        lo = c * V7X_MXU_DIM
        f_ref[0, :, lo:lo + V7X_MXU_DIM] = wide_dot(F_OFF + lo).astype(BF16)
    for c in range(D_MODEL // V7X_MXU_DIM):
        lo = c * V7X_MXU_DIM
        ga_ref[0, :, lo:lo + V7X_MXU_DIM] = jax.nn.sigmoid(wide_dot(GA_OFF + lo)).astype(BF16)
        gf_ref[0, :, lo:lo + V7X_MXU_DIM] = jax.nn.sigmoid(wide_dot(GF_OFF + lo)).astype(BF16)


def _inproj(x, mod, gain, w_in, q_norm, k_norm, cos2, sin2):
    nb, t, _ = x.shape
    tm = TOKEN_TILE
    tok = lambda w: pl.BlockSpec((1, tm, w), lambda b, i: (b, i, 0))
    shp = lambda w: jax.ShapeDtypeStruct((nb, t, w), BF16)
    return pl.pallas_call(
        _inproj_kernel,
        out_shape=[jax.ShapeDtypeStruct((nb, N_Q_HEADS, t, HEAD_DIM), BF16), shp(KV_WIDTH),
                   shp(VX_WIDTH), shp(FOURIER_WIDTH), shp(D_MODEL), shp(D_MODEL)],
        grid=(nb, t // tm),
        in_specs=[
            tok(D_MODEL),
            pl.BlockSpec((1, N_MOD, D_MODEL), lambda b, i: (b, 0, 0)),
            _resident((1, D_MODEL)),
            _resident(w_in.shape),
            _resident((1, HEAD_DIM)),
            _resident((1, HEAD_DIM)),
            pl.BlockSpec((tm, HEAD_DIM), lambda b, i: (i, 0)),
            pl.BlockSpec((tm, HEAD_DIM), lambda b, i: (i, 0)),
        ],
        out_specs=[pl.BlockSpec((1, N_Q_HEADS, tm, HEAD_DIM), lambda b, i: (b, 0, i, 0)),
                   tok(KV_WIDTH), tok(VX_WIDTH), tok(FOURIER_WIDTH), tok(D_MODEL), tok(D_MODEL)],
        compiler_params=_params("parallel", "parallel"),
        name="inproj",
    )(x, mod, gain, w_in, q_norm, k_norm, cos2, sin2)


def _ctx_kv_kernel(x_ref, mod_ref, gain_ref, w_ref, kn_ref, k_ref, v_ref):
    x = x_ref[0]
    shift = mod_ref[0, 3:4, :]
    scale = mod_ref[0, 4:5, :]
    h = (_rms(x, gain_ref[...]) * (1.0 + scale) + shift).astype(BF16)
    kn = kn_ref[...]
    t = _dot(h, w_ref[:, K_OFF:K_OFF + KV_WIDTH])
    for hd in range(N_KV_HEADS):
        lo = hd * HEAD_DIM
        k_ref[0, :, lo:lo + HEAD_DIM] = _head_norm(t[:, lo:lo + HEAD_DIM], kn).astype(BF16)
    _store_widened_values(v_ref, _dot(h, w_ref[:, V_OFF:V_OFF + KV_WIDTH]))


def _ctx_kv(ctx, mod, mod_row, gain, w_in, k_norm):
    nb, t, _ = ctx.shape
    tok = lambda w: pl.BlockSpec((1, t, w), lambda b: (b, 0, 0))
    shp = lambda w: jax.ShapeDtypeStruct((nb, t, w), BF16)
    return pl.pallas_call(
        _ctx_kv_kernel,
        out_shape=[shp(KV_WIDTH), shp(VX_WIDTH)],
        grid=(nb,),
        in_specs=[
            tok(D_MODEL),
            pl.BlockSpec((1, N_MOD, D_MODEL), lambda b: (mod_row, 0, 0)),
            _resident((1, D_MODEL)),
            _resident(w_in.shape),
            _resident((1, HEAD_DIM)),
        ],
        out_specs=[tok(KV_WIDTH), tok(VX_WIDTH)],
        compiler_params=_params("parallel"),
        name="ctx_kv",
    )(ctx, mod, gain, w_in, k_norm)


_NT = (((1,), (1,)), ((), ()))


ATTN_SCRATCH_PER_HEAD = 5


def _attn_kernel(q_ref, kc_ref, k_ref, vcx_ref, vx_ref, o_ref, *scratch):
    heads = [scratch[ATTN_SCRATCH_PER_HEAD * g:ATTN_SCRATCH_PER_HEAD * (g + 1)] for g in range(GQA_GROUP)]

    @pl.when(pl.program_id(0) == 0)
    def _no_previous_item():
        for pc_ref, pl_ref, sc_ref, sl_ref, m_ref in heads:
            pc_ref[...] = jnp.ones(pc_ref.shape, BF16)
            pl_ref[...] = jnp.ones(pl_ref.shape, BF16)
            sc_ref[...] = jnp.zeros(sc_ref.shape, F32)
            sl_ref[...] = jnp.zeros(sl_ref.shape, F32)
            m_ref[...] = jnp.zeros(m_ref.shape, F32)

    def weights_from_scores(pc_ref, pl_ref, sc_ref, sl_ref, m_ref):
        m = m_ref[...]
        for s_ref, p_ref in ((sc_ref, pc_ref), (sl_ref, pl_ref)):
            for lo in range(0, s_ref.shape[1], V7X_MXU_DIM):
                p_ref[:, lo:lo + V7X_MXU_DIM] = jnp.exp2(s_ref[:, lo:lo + V7X_MXU_DIM] - m).astype(BF16)

    weights_from_scores(*heads[-1])

    for g, (pc_ref, pl_ref, sc_ref, sl_ref, m_ref) in enumerate(heads):
        oe = _dot(pc_ref[...], vcx_ref[0]) + _dot(pl_ref[...], vx_ref[0])
        o_ref[0, :, g * HEAD_DIM:(g + 1) * HEAD_DIM] = (oe[:, :HEAD_DIM] / oe[:, HEAD_DIM:]).astype(BF16)

        q = q_ref[0, g]
        s_c = lax.dot_general(q, kc_ref[0], _NT, preferred_element_type=F32)
        s_l = lax.dot_general(q, k_ref[0], _NT, preferred_element_type=F32)
        m_ref[...] = jnp.maximum(jnp.max(s_c, axis=-1, keepdims=True), jnp.max(s_l, axis=-1, keepdims=True))
        sc_ref[...] = s_c
        sl_ref[...] = s_l

    for head in heads[:-1]:
        weights_from_scores(*head)


def _attention(q, k, vx, k_c, vx_c):
    nb, _, n, _ = q.shape
    n_ctx = k_c.shape[1]
    tq = ATTN_Q_TILE
    tiles = n // tq
    items = nb * N_KV_HEADS * tiles

    def item(t):
        return t // (N_KV_HEADS * tiles), (t // tiles) % N_KV_HEADS, t % tiles

    def scored(t):
        return item(jnp.minimum(t, items - 1))

    def weighted(t):
        return item(jnp.maximum(t - 1, 0))

    def q_map(t):
        b, h, i = scored(t)
        return b, h, i, 0

    def k_map(t):
        b, h, _ = scored(t)
        return b, 0, h

    def v_map(t):
        b, h, _ = weighted(t)
        return b, 0, h

    def o_map(t):
        b, h, i = weighted(t)
        return b, i, h

    return pl.pallas_call(
        _attn_kernel,
        out_shape=jax.ShapeDtypeStruct((nb, n, Q_WIDTH), BF16),
        grid=(items + 1,),
        in_specs=[
            pl.BlockSpec((1, GQA_GROUP, tq, HEAD_DIM), q_map),
            pl.BlockSpec((1, n_ctx, HEAD_DIM), k_map),
            pl.BlockSpec((1, n, HEAD_DIM), k_map),
            pl.BlockSpec((1, n_ctx, V7X_MXU_DIM), v_map),
            pl.BlockSpec((1, n, V7X_MXU_DIM), v_map),
        ],
        out_specs=pl.BlockSpec((1, tq, GQA_GROUP * HEAD_DIM), o_map),
        scratch_shapes=GQA_GROUP * [
            pltpu.VMEM((tq, n_ctx), BF16), pltpu.VMEM((tq, n), BF16),
            pltpu.VMEM((tq, n_ctx), F32), pltpu.VMEM((tq, n), F32), pltpu.VMEM((tq, 1), F32)],
        compiler_params=_params("arbitrary"),
        name="attn",
    )(q, k_c, k, vx_c, vx)


DFT_RADIX = 8


def _dft_tables(n):
    r = n // DFT_RADIX
    k = np.arange(DFT_RADIX)[:, None, None] + DFT_RADIX * np.arange(r)[None, :, None]
    ang = 2.0 * np.pi * ((k * np.arange(r)[None, None, :]) % n) / n
    cos, sin = np.cos(ang), np.sin(ang)
    pos = np.concatenate([np.concatenate([cos, sin], axis=2),
                          np.concatenate([-sin, cos], axis=2)], axis=1)
    c = FOURIER_GROUP
    scale = 1.0 / np.sqrt(float(n * c))
    ang_c = 2.0 * np.pi * (np.outer(np.arange(c), np.arange(c)) % c) / c
    eye = np.eye(N_FOURIER_GROUPS)
    chan = np.concatenate([np.kron(eye, np.cos(ang_c) * scale),
                           np.kron(eye, np.sin(ang_c) * scale)], axis=0)
    return jnp.asarray(pos, F32).astype(BF16), jnp.asarray(chan, F32).astype(BF16)


def _fold_radix8(x):
    c = 0.5 ** 0.5
    e02p, e02m, e13p, e13m = x[0] + x[4], x[0] - x[4], x[2] + x[6], x[2] - x[6]
    o02p, o02m, o13p, o13m = x[1] + x[5], x[1] - x[5], x[3] + x[7], x[3] - x[7]
    e0, e2, o0, o2 = e02p + e13p, e02p - e13p, o02p + o13p, o02p - o13p
    p, q = c * (o02m - o13m), c * (o02m + o13m)
    re = [e0 + o0, e02m + p, e2, e02m - p, e0 - o0, e02m - p, e2, e02m + p]
    im1, im3 = -e13m - q, e13m - q
    im = [None, im1, -o2, im3, None, -im3, o2, -im1]
    return re, im


def _fourier_kernel(f_ref, pos_ref, chan_ref, o_ref, t_ref, y_ref):
    r = f_ref.shape[1] // DFT_RADIX
    w = FOURIER_WIDTH
    lane_chunk = 128
    for lo in range(0, w, lane_chunk):
        x = [f_ref[0, n1 * r:(n1 + 1) * r, lo:lo + lane_chunk].astype(F32) for n1 in range(DFT_RADIX)]
        re, im = _fold_radix8(x)
        for k1 in range(DFT_RADIX):
            t_ref[k1, 0:r, lo:lo + lane_chunk] = re[k1].astype(BF16)
            t_ref[k1, r:2 * r, lo:lo + lane_chunk] = (
                jnp.zeros((r, lane_chunk), BF16) if im[k1] is None else im[k1].astype(BF16))
    for k1 in range(DFT_RADIX):
        z = _dot(pos_ref[k1], t_ref[k1])
        y = (_dot(z[0:r].astype(BF16), chan_ref[0:w, :])
             + _dot(z[r:2 * r].astype(BF16), chan_ref[w:2 * w, :]))
        for j in range(w // lane_chunk):
            y_ref[j, pl.ds(k1, r, stride=DFT_RADIX), :] = y[:, j * lane_chunk:(j + 1) * lane_chunk]
    for j in range(w // lane_chunk):
        o_ref[0, :, j * lane_chunk:(j + 1) * lane_chunk] = y_ref[j].astype(BF16)


def _fourier(f):
    nb, n, w = f.shape
    pos, chan = _dft_tables(n)
    blk = pl.BlockSpec((1, n, w), lambda b: (b, 0, 0))
    return pl.pallas_call(
        _fourier_kernel,
        out_shape=jax.ShapeDtypeStruct(f.shape, BF16),
        grid=(nb,),
        in_specs=[blk, _resident(pos.shape), _resident(chan.shape)],
        out_specs=blk,
        scratch_shapes=[pltpu.VMEM((DFT_RADIX, 2 * n // DFT_RADIX, w), BF16),
                        pltpu.VMEM((w // 128, n, 128), F32)],
        compiler_params=_params("parallel"),
        name="fourier",
    )(f, pos, chan)


def _merge_kernel(x_ref, mod_ref, ya_ref, yf_ref, ga_ref, gf_ref, wab_ref, wfb_ref, wo_ref, o_ref):
    gate = mod_ref[0, 5:6, :]
    merged = (ga_ref[0].astype(F32) * _dot(ya_ref[0], wab_ref[...])
              + gf_ref[0].astype(F32) * _dot(yf_ref[0], wfb_ref[...]))
    mix = _dot(merged.astype(BF16), wo_ref[...])
    o_ref[0] = x_ref[0] + gate * mix


def _merge(x, mod, y_attn, y_four, sig_a, sig_f, w_ab, w_fb, w_o):
    nb, t, _ = x.shape
    tm = TOKEN_TILE
    tok = lambda w: pl.BlockSpec((1, tm, w), lambda b, i: (b, i, 0))
    return pl.pallas_call(
        _merge_kernel,
        out_shape=jax.ShapeDtypeStruct(x.shape, F32),
        grid=(nb, t // tm),
        in_specs=[
            tok(D_MODEL),
            pl.BlockSpec((1, N_MOD, D_MODEL), lambda b, i: (b, 0, 0)),
            tok(Q_WIDTH), tok(FOURIER_WIDTH), tok(D_MODEL), tok(D_MODEL),
            _resident(w_ab.shape), _resident(w_fb.shape), _resident(w_o.shape),
        ],
        out_specs=tok(D_MODEL),
        compiler_params=_params("parallel", "parallel"),
        name="merge",
    )(x, mod, y_attn, y_four, sig_a, sig_f, w_ab, w_fb, w_o)


def _rope_tables(n_tokens):
    rows = n_tokens // GRID_W
    row_ids = jnp.repeat(jnp.arange(rows, dtype=F32), GRID_W)
    col_ids = jnp.tile(jnp.arange(GRID_W, dtype=F32), rows)
    inv_freq = ROPE_THETA ** (-jnp.arange(0, AXIS_ROPE_DIM, 2, dtype=F32) / AXIS_ROPE_DIM)
    ang = jnp.concatenate([row_ids[:, None] * inv_freq, col_ids[:, None] * inv_freq], axis=-1)
    cos, sin = jnp.cos(ang), jnp.sin(ang)
    return jnp.concatenate([cos, cos], axis=-1), jnp.concatenate([-sin, sin], axis=-1)


def kernel(x, c, ctx, c_ctx, w_ada, b_ada, norm_ffn1, w_ffn1_in, w_ffn1_out, norm_mix, w_in,
           q_norm, k_norm, w_attn_branch, w_fourier_branch, w_out, norm_ffn2, w_ffn2_in, w_ffn2_out):
    nb, n_lat, _ = x.shape
    n_ctx = ctx.shape[1]
    depth = w_ada.shape[0]
    assert depth == 1, "the context stream is only carried through the (single) last layer"
    cos2, sin2 = _rope_tables(n_lat)
    ctx_row = nb
    pad_rows = -(nb + 1) % 8
    c_all = jnp.concatenate([c, c_ctx[None, :], jnp.zeros((pad_rows, D_MODEL), F32)], axis=0)

    i = 0
    mod = _adaln(c_all, w_ada[i], b_ada[i][None, :]).reshape(c_all.shape[0], N_MOD, D_MODEL)
    x = _ffn(x, mod, lambda b: b, 0, norm_ffn1[i][None, :], w_ffn1_in[i], w_ffn1_out[i])
    ctx_flat = _ffn(ctx.reshape(1, nb * n_ctx, D_MODEL), mod, lambda b: ctx_row, 0,
                    norm_ffn1[i][None, :], w_ffn1_in[i], w_ffn1_out[i])

    q, k, vx, f, sig_a, sig_f = _inproj(x, mod, norm_mix[i][None, :], w_in[i],
                                        q_norm[i][None, :], k_norm[i][None, :], cos2, sin2)
    k_c, vx_c = _ctx_kv(ctx_flat.reshape(nb, n_ctx, D_MODEL), mod, ctx_row, norm_mix[i][None, :],
                        w_in[i], k_norm[i][None, :])
    y_attn = _attention(q, k, vx, k_c, vx_c)
    y_four = _fourier(f)
    x = _merge(x, mod, y_attn, y_four, sig_a, sig_f, w_attn_branch[i], w_fourier_branch[i], w_out[i])

    return _ffn(x, mod, lambda b: b, 6, norm_ffn2[i][None, :], w_ffn2_in[i], w_ffn2_out[i])
```

```python
import functools

import numpy as np
import jax
import jax.numpy as jnp
from jax import lax
from jax.experimental import pallas as pl
from jax.experimental.pallas import tpu as pltpu

D_MODEL = 1024
GRID_W = 64
HEAD_DIM = 128
N_Q_HEADS = D_MODEL // HEAD_DIM
N_KV_HEADS = N_Q_HEADS // 4
GQA_GROUP = N_Q_HEADS // N_KV_HEADS
Q_WIDTH = N_Q_HEADS * HEAD_DIM
KV_WIDTH = N_KV_HEADS * HEAD_DIM
FOURIER_GROUP = 128
N_FOURIER_GROUPS = 4
FOURIER_WIDTH = N_FOURIER_GROUPS * FOURIER_GROUP
D_FF = 2816
AXIS_ROPE_DIM = HEAD_DIM // 2
ROPE_THETA = 10000.0
EPS = 1e-6
N_MOD = 9
ATTN_SCALE = HEAD_DIM ** -0.5
LOG2_E = 1.4426950408889634

K_OFF = Q_WIDTH
V_OFF = K_OFF + KV_WIDTH
F_OFF = V_OFF + KV_WIDTH
GA_OFF = F_OFF + FOURIER_WIDTH
GF_OFF = GA_OFF + D_MODEL

V7X_MXU_DIM = 256
VX_WIDTH = N_KV_HEADS * V7X_MXU_DIM
V7X_VMEM_BYTES = 64 * 2**20
VMEM_LIMIT = V7X_VMEM_BYTES - 8 * 2**20

TOKEN_TILE = 512
WIDE_TOKEN_TILE = 1024
FF_CHUNK = V7X_MXU_DIM
ATTN_Q_TILE = 512
ADALN_TILE = 1152

F32 = jnp.float32
BF16 = jnp.bfloat16


def _params(*sem):
    return pltpu.CompilerParams(dimension_semantics=sem, vmem_limit_bytes=VMEM_LIMIT)


def _resident(shape):
    return pl.BlockSpec(shape, lambda *_: (0,) * len(shape), pipeline_mode=pl.Buffered(1))


def _rms(x, gain):
    return x * lax.rsqrt(jnp.mean(x * x, axis=-1, keepdims=True) + EPS) * gain


def _dot(a, b):
    return jnp.dot(a, b.astype(BF16), preferred_element_type=F32)


def _adaln_kernel(c_ref, w_ref, b_ref, o_ref):
    c = c_ref[...]
    a = (c * jax.nn.sigmoid(c)).astype(BF16)
    o_ref[...] = _dot(a, w_ref[...]) + b_ref[...]


def _adaln(c_all, w, b):
    rows, n_out = c_all.shape[0], w.shape[1]
    return pl.pallas_call(
        _adaln_kernel,
        out_shape=jax.ShapeDtypeStruct((rows, n_out), F32),
        grid=(n_out // ADALN_TILE,),
        in_specs=[
            pl.BlockSpec((rows, D_MODEL), lambda j: (0, 0)),
            pl.BlockSpec((D_MODEL, ADALN_TILE), lambda j: (0, j)),
            pl.BlockSpec((1, ADALN_TILE), lambda j: (0, j)),
        ],
        out_specs=pl.BlockSpec((rows, ADALN_TILE), lambda j: (0, j)),
        compiler_params=_params("arbitrary"),
        name="adaln",
    )(c_all, w, b)


class _TokenStreams:
    def __init__(self, arrays, tile):
        self.tile = tile
        self.per_batch = [a.shape[1] // tile for a in arrays]
        counts = [a.shape[0] * p for a, p in zip(arrays, self.per_batch)]
        self.starts = [sum(counts[:s]) for s in range(len(arrays) + 1)]
        self.steps = self.starts[-1]

    def batch_and_tile(self, s, t):
        local = jnp.clip(t - self.starts[s], 0, self.starts[s + 1] - self.starts[s] - 1)
        return local // self.per_batch[s], local % self.per_batch[s]

    def token_spec(self, s, width):
        def index(t):
            b, i = self.batch_and_tile(s, t)
            return b, i, 0
        return pl.BlockSpec((1, self.tile, width), index)

    def mod_spec(self, mod_rows):
        def index(t):
            row = mod_rows[0](self.batch_and_tile(0, t)[0])
            for s in range(1, len(mod_rows)):
                row = jnp.where(t >= self.starts[s], mod_rows[s](self.batch_and_tile(s, t)[0]), row)
            return row, 0, 0
        return pl.BlockSpec((1, N_MOD, D_MODEL), index)

    def active(self, s):
        t = pl.program_id(0)
        return jnp.logical_and(t >= self.starts[s], t < self.starts[s + 1])


def _ffn_kernel(*refs, streams, mod_base):
    n = len(streams.per_batch)
    x_refs, (mod_ref, gain_ref, win_ref, wout_ref), o_refs, act_ref = (
        refs[:n], refs[n:n + 4], refs[n + 4:2 * n + 4], refs[2 * n + 4])

    def half_step(x_ref, o_ref):
        x = x_ref[0]
        shift = mod_ref[0, mod_base:mod_base + 1, :]
        scale = mod_ref[0, mod_base + 1:mod_base + 2, :]
        gate = mod_ref[0, mod_base + 2:mod_base + 3, :]
        h = (_rms(x, gain_ref[...]) * (1.0 + scale) + shift).astype(BF16)
        for j in range(D_FF // FF_CHUNK):
            lo = j * FF_CHUNK
            g = _dot(h, win_ref[:, lo:lo + FF_CHUNK])
            u = _dot(h, win_ref[:, D_FF + lo:D_FF + lo + FF_CHUNK])
            act_ref[:, lo:lo + FF_CHUNK] = (g * jax.nn.sigmoid(g) * u).astype(BF16)
        y = _dot(act_ref[...], wout_ref[...])
        o_ref[0] = x + 0.5 * gate * y

    if n == 1:
        half_step(x_refs[0], o_refs[0])
    else:
        for s in range(n):
            pl.when(streams.active(s))(functools.partial(half_step, x_refs[s], o_refs[s]))


def _ffn(xs, mod, mod_rows, mod_base, gain, w_in, w_out):
    streams = _TokenStreams(xs, TOKEN_TILE)
    tok = [streams.token_spec(s, D_MODEL) for s in range(len(xs))]
    return pl.pallas_call(
        functools.partial(_ffn_kernel, streams=streams, mod_base=mod_base),
        out_shape=[jax.ShapeDtypeStruct(x.shape, F32) for x in xs],
        grid=(streams.steps,),
        in_specs=tok + [
            streams.mod_spec(mod_rows),
            _resident((1, D_MODEL)),
            _resident((D_MODEL, 2 * D_FF)),
            _resident((D_FF, D_MODEL)),
        ],
        out_specs=tok,
        scratch_shapes=[pltpu.VMEM((TOKEN_TILE, D_FF), BF16)],
        compiler_params=_params("arbitrary"),
        name="ffn",
    )(*xs, mod, gain, w_in, w_out)


def _head_norm(t, gain):
    return t * lax.rsqrt(jnp.mean(t * t, axis=-1, keepdims=True) + EPS) * gain


def _rope(t, cos2, sin2):
    return t * cos2 + pltpu.roll(t, AXIS_ROPE_DIM, 1) * sin2


def _store_widened_values(v_ref, v):
    ones = jnp.ones((v.shape[0], V7X_MXU_DIM - HEAD_DIM), BF16)
    for hd in range(N_KV_HEADS):
        lo = hd * V7X_MXU_DIM
        v_ref[0, :, lo:lo + HEAD_DIM] = v[:, hd * HEAD_DIM:(hd + 1) * HEAD_DIM].astype(BF16)
        v_ref[0, :, lo + HEAD_DIM:lo + V7X_MXU_DIM] = ones


def _inproj_kernel(x_ref, ctx_ref, mod_ref, gain_ref, w_ref, qn_ref, kn_ref, cos_ref, sin_ref,
                   q_ref, k_ref, v_ref, f_ref, ga_ref, gf_ref, kc_ref, vc_ref, *, streams):
    kn = kn_ref[...]
    heads_per_dot = V7X_MXU_DIM // HEAD_DIM

    def normed_input(tok_ref):
        shift = mod_ref[0, 3:4, :]
        scale = mod_ref[0, 4:5, :]
        return (_rms(tok_ref[0], gain_ref[...]) * (1.0 + scale) + shift).astype(BF16)

    @pl.when(streams.active(0))
    def _latent_tokens():
        h = normed_input(x_ref)

        def wide_dot(col):
            return _dot(h, w_ref[:, col:col + V7X_MXU_DIM])

        cos2 = cos_ref[...]
        sin2 = sin_ref[...]
        qn = qn_ref[...] * (ATTN_SCALE * LOG2_E)
        for c in range(Q_WIDTH // V7X_MXU_DIM):
            t = wide_dot(c * V7X_MXU_DIM)
            for j in range(heads_per_dot):
                tj = t[:, j * HEAD_DIM:(j + 1) * HEAD_DIM]
                q_ref[0, c * heads_per_dot + j] = _rope(_head_norm(tj, qn), cos2, sin2).astype(BF16)
        for c in range(KV_WIDTH // V7X_MXU_DIM):
            t = wide_dot(K_OFF + c * V7X_MXU_DIM)
            for j in range(heads_per_dot):
                lo = c * V7X_MXU_DIM + j * HEAD_DIM
                tj = t[:, j * HEAD_DIM:(j + 1) * HEAD_DIM]
                k_ref[0, :, lo:lo + HEAD_DIM] = _rope(_head_norm(tj, kn), cos2, sin2).astype(BF16)
        _store_widened_values(v_ref, _dot(h, w_ref[:, V_OFF:V_OFF + KV_WIDTH]))
        for c in range(FOURIER_WIDTH // V7X_MXU_DIM):
            lo = c * V7X_MXU_DIM
            f_ref[0, :, lo:lo + V7X_MXU_DIM] = wide_dot(F_OFF + lo).astype(BF16)
        for c in range(D_MODEL // V7X_MXU_DIM):
            lo = c * V7X_MXU_DIM
            ga_ref[0, :, lo:lo + V7X_MXU_DIM] = jax.nn.sigmoid(wide_dot(GA_OFF + lo)).astype(BF16)
            gf_ref[0, :, lo:lo + V7X_MXU_DIM] = jax.nn.sigmoid(wide_dot(GF_OFF + lo)).astype(BF16)

    @pl.when(streams.active(1))
    def _context_tokens():
        h = normed_input(ctx_ref)
        t = _dot(h, w_ref[:, K_OFF:K_OFF + KV_WIDTH])
        for hd in range(N_KV_HEADS):
            lo = hd * HEAD_DIM
            kc_ref[0, :, lo:lo + HEAD_DIM] = _head_norm(t[:, lo:lo + HEAD_DIM], kn).astype(BF16)
        _store_widened_values(vc_ref, _dot(h, w_ref[:, V_OFF:V_OFF + KV_WIDTH]))


def _inproj(x, ctx, mod, mod_rows, gain, w_in, q_norm, k_norm, cos2, sin2):
    nb, t, _ = x.shape
    streams = _TokenStreams([x, ctx], TOKEN_TILE)
    tm = streams.tile
    tok = lambda w: streams.token_spec(0, w)
    ctx_tok = lambda w: streams.token_spec(1, w)
    shp = lambda w: jax.ShapeDtypeStruct((nb, t, w), BF16)
    ctx_shp = lambda w: jax.ShapeDtypeStruct(ctx.shape[:2] + (w,), BF16)

    def q_index(step):
        b, i = streams.batch_and_tile(0, step)
        return b, 0, i, 0

    def rope_index(step):
        return streams.batch_and_tile(0, step)[1], 0

    return pl.pallas_call(
        functools.partial(_inproj_kernel, streams=streams),
        out_shape=[jax.ShapeDtypeStruct((nb, N_Q_HEADS, t, HEAD_DIM), BF16), shp(KV_WIDTH),
                   shp(VX_WIDTH), shp(FOURIER_WIDTH), shp(D_MODEL), shp(D_MODEL),
                   ctx_shp(KV_WIDTH), ctx_shp(VX_WIDTH)],
        grid=(streams.steps,),
        in_specs=[
            tok(D_MODEL),
            ctx_tok(D_MODEL),
            streams.mod_spec(mod_rows),
            _resident((1, D_MODEL)),
            _resident(w_in.shape),
            _resident((1, HEAD_DIM)),
            _resident((1, HEAD_DIM)),
            pl.BlockSpec((tm, HEAD_DIM), rope_index),
            pl.BlockSpec((tm, HEAD_DIM), rope_index),
        ],
        out_specs=[pl.BlockSpec((1, N_Q_HEADS, tm, HEAD_DIM), q_index),
                   tok(KV_WIDTH), tok(VX_WIDTH), tok(FOURIER_WIDTH), tok(D_MODEL), tok(D_MODEL),
                   ctx_tok(KV_WIDTH), ctx_tok(VX_WIDTH)],
        compiler_params=_params("arbitrary"),
        name="inproj",
    )(x, ctx, mod, gain, w_in, q_norm, k_norm, cos2, sin2)


_NT = (((1,), (1,)), ((), ()))


ATTN_SCRATCH_PER_HEAD = 5


def _attn_kernel(q_ref, kc_ref, k_ref, vcx_ref, vx_ref, o_ref, *scratch):
    heads = [scratch[ATTN_SCRATCH_PER_HEAD * g:ATTN_SCRATCH_PER_HEAD * (g + 1)] for g in range(GQA_GROUP)]

    @pl.when(pl.program_id(0) == 0)
    def _no_previous_item():
        for pc_ref, pl_ref, sc_ref, sl_ref, m_ref in heads:
            pc_ref[...] = jnp.ones(pc_ref.shape, BF16)
            pl_ref[...] = jnp.ones(pl_ref.shape, BF16)
            sc_ref[...] = jnp.zeros(sc_ref.shape, F32)
            sl_ref[...] = jnp.zeros(sl_ref.shape, F32)
            m_ref[...] = jnp.zeros(m_ref.shape, F32)

    def weights_from_scores(pc_ref, pl_ref, sc_ref, sl_ref, m_ref):
        m = m_ref[...]
        for s_ref, p_ref in ((sc_ref, pc_ref), (sl_ref, pl_ref)):
            for lo in range(0, s_ref.shape[1], V7X_MXU_DIM):
                p_ref[:, lo:lo + V7X_MXU_DIM] = jnp.exp2(s_ref[:, lo:lo + V7X_MXU_DIM] - m).astype(BF16)

    weights_from_scores(*heads[-1])

    for g, (pc_ref, pl_ref, sc_ref, sl_ref, m_ref) in enumerate(heads):
        oe = _dot(pc_ref[...], vcx_ref[0]) + _dot(pl_ref[...], vx_ref[0])
        o_ref[0, :, g * HEAD_DIM:(g + 1) * HEAD_DIM] = (oe[:, :HEAD_DIM] / oe[:, HEAD_DIM:]).astype(BF16)

        q = q_ref[0, g]
        s_c = lax.dot_general(q, kc_ref[0], _NT, preferred_element_type=F32)
        s_l = lax.dot_general(q, k_ref[0], _NT, preferred_element_type=F32)
        m_ref[...] = jnp.maximum(jnp.max(s_c, axis=-1, keepdims=True), jnp.max(s_l, axis=-1, keepdims=True))
        sc_ref[...] = s_c
        sl_ref[...] = s_l

    for head in heads[:-1]:
        weights_from_scores(*head)


def _attention(q, k, vx, k_c, vx_c):
    nb, _, n, _ = q.shape
    n_ctx = k_c.shape[1]
    tq = ATTN_Q_TILE
    tiles = n // tq
    items = nb * N_KV_HEADS * tiles

    def item(t):
        return t // (N_KV_HEADS * tiles), (t // tiles) % N_KV_HEADS, t % tiles

    def scored(t):
        return item(jnp.minimum(t, items - 1))

    def weighted(t):
        return item(jnp.maximum(t - 1, 0))

    def q_map(t):
        b, h, i = scored(t)
        return b, h, i, 0

    def k_map(t):
        b, h, _ = scored(t)
        return b, 0, h

    def v_map(t):
        b, h, _ = weighted(t)
        return b, 0, h

    def o_map(t):
        b, h, i = weighted(t)
        return b, i, h

    return pl.pallas_call(
        _attn_kernel,
        out_shape=jax.ShapeDtypeStruct((nb, n, Q_WIDTH), BF16),
        grid=(items + 1,),
        in_specs=[
            pl.BlockSpec((1, GQA_GROUP, tq, HEAD_DIM), q_map),
            pl.BlockSpec((1, n_ctx, HEAD_DIM), k_map),
            pl.BlockSpec((1, n, HEAD_DIM), k_map),
            pl.BlockSpec((1, n_ctx, V7X_MXU_DIM), v_map),
            pl.BlockSpec((1, n, V7X_MXU_DIM), v_map),
        ],
        out_specs=pl.BlockSpec((1, tq, GQA_GROUP * HEAD_DIM), o_map),
        scratch_shapes=GQA_GROUP * [
            pltpu.VMEM((tq, n_ctx), BF16), pltpu.VMEM((tq, n), BF16),
            pltpu.VMEM((tq, n_ctx), F32), pltpu.VMEM((tq, n), F32), pltpu.VMEM((tq, 1), F32)],
        compiler_params=_params("arbitrary"),
        name="attn",
    )(q, k_c, k, vx_c, vx)


DFT_RADIX = 8


def _dft_tables(n):
    r = n // DFT_RADIX
    k = np.arange(DFT_RADIX)[:, None, None] + DFT_RADIX * np.arange(r)[None, :, None]
    ang = 2.0 * np.pi * ((k * np.arange(r)[None, None, :]) % n) / n
    cos, sin = np.cos(ang), np.sin(ang)
    pos = np.concatenate([np.concatenate([cos, sin], axis=2),
                          np.concatenate([-sin, cos], axis=2)], axis=1)
    c = FOURIER_GROUP
    scale = 1.0 / np.sqrt(float(n * c))
    ang_c = 2.0 * np.pi * (np.outer(np.arange(c), np.arange(c)) % c) / c
    eye = np.eye(V7X_MXU_DIM // c)
    chan = np.concatenate([np.kron(eye, np.cos(ang_c) * scale),
                           np.kron(eye, np.sin(ang_c) * scale)], axis=0)
    return jnp.asarray(pos, F32).astype(BF16), jnp.asarray(chan, F32).astype(BF16)


def _fold_radix8(x):
    c = 0.5 ** 0.5
    e02p, e02m, e13p, e13m = x[0] + x[4], x[0] - x[4], x[2] + x[6], x[2] - x[6]
    o02p, o02m, o13p, o13m = x[1] + x[5], x[1] - x[5], x[3] + x[7], x[3] - x[7]
    e0, e2, o0, o2 = e02p + e13p, e02p - e13p, o02p + o13p, o02p - o13p
    p, q = c * (o02m - o13m), c * (o02m + o13m)
    re = [e0 + o0, e02m + p, e2, e02m - p, e0 - o0, e02m - p, e2, e02m + p]
    im1, im3 = -e13m - q, e13m - q
    im = [None, im1, -o2, im3, None, -im3, o2, -im1]
    return re, im


def _fourier_kernel(f_ref, pos_ref, chan_ref, o_ref, t_ref, zr_ref, zi_ref):
    r = f_ref.shape[1] // DFT_RADIX
    w = FOURIER_WIDTH
    pair = V7X_MXU_DIM
    for lo in range(0, w, pair):
        for sub in range(lo, lo + pair, FOURIER_GROUP):
            lanes = slice(sub, sub + FOURIER_GROUP)
            x = [f_ref[0, n1 * r:(n1 + 1) * r, lanes].astype(F32) for n1 in range(DFT_RADIX)]
            re, im = _fold_radix8(x)
            for k1 in range(DFT_RADIX):
                t_ref[k1, 0:r, lanes] = re[k1].astype(BF16)
                t_ref[k1, r:2 * r, lanes] = (
                    jnp.zeros((r, FOURIER_GROUP), BF16) if im[k1] is None else im[k1].astype(BF16))
        h = lo // pair
        for k1 in range(DFT_RADIX):
            z = _dot(pos_ref[k1], t_ref[k1, :, lo:lo + pair])
            zr_ref[h, k1 * r:(k1 + 1) * r, :] = z[0:r].astype(BF16)
            zi_ref[h, k1 * r:(k1 + 1) * r, :] = z[r:2 * r].astype(BF16)
        y = _dot(zr_ref[h], chan_ref[0:pair, :]) + _dot(zi_ref[h], chan_ref[pair:2 * pair, :])
        for k1 in range(DFT_RADIX):
            o_ref[0, :, k1 * w + lo:k1 * w + lo + pair] = y[k1 * r:(k1 + 1) * r].astype(BF16)


def _fourier(f):
    nb, n, w = f.shape
    pos, chan = _dft_tables(n)
    r = n // DFT_RADIX
    out = pl.pallas_call(
        _fourier_kernel,
        out_shape=jax.ShapeDtypeStruct((nb, r, DFT_RADIX * w), BF16),
        grid=(nb,),
        in_specs=[pl.BlockSpec((1, n, w), lambda b: (b, 0, 0)), _resident(pos.shape), _resident(chan.shape)],
        out_specs=pl.BlockSpec((1, r, DFT_RADIX * w), lambda b: (b, 0, 0)),
        scratch_shapes=[pltpu.VMEM((DFT_RADIX, 2 * r, w), BF16),
                        pltpu.VMEM((w // V7X_MXU_DIM, n, V7X_MXU_DIM), BF16),
                        pltpu.VMEM((w // V7X_MXU_DIM, n, V7X_MXU_DIM), BF16)],
        compiler_params=_params("parallel"),
        name="fourier",
    )(f, pos, chan)
    return out.reshape(nb, n, w)


def _merge_kernel(x_ref, mod_ref, ya_ref, yf_ref, ga_ref, gf_ref, wab_ref, wfb_ref, wo_ref, o_ref):
    gate = mod_ref[0, 5:6, :]
    merged = (ga_ref[0].astype(F32) * _dot(ya_ref[0], wab_ref[...])
              + gf_ref[0].astype(F32) * _dot(yf_ref[0], wfb_ref[...]))
    mix = _dot(merged.astype(BF16), wo_ref[...])
    o_ref[0] = x_ref[0] + gate * mix


def _merge(x, mod, y_attn, y_four, sig_a, sig_f, w_ab, w_fb, w_o):
    nb, t, _ = x.shape
    tm = WIDE_TOKEN_TILE
    tok = lambda w: pl.BlockSpec((1, tm, w), lambda b, i: (b, i, 0))
    return pl.pallas_call(
        _merge_kernel,
        out_shape=jax.ShapeDtypeStruct(x.shape, F32),
        grid=(nb, t // tm),
        in_specs=[
            tok(D_MODEL),
            pl.BlockSpec((1, N_MOD, D_MODEL), lambda b, i: (b, 0, 0)),
            tok(Q_WIDTH), tok(FOURIER_WIDTH), tok(D_MODEL), tok(D_MODEL),
            _resident(w_ab.shape), _resident(w_fb.shape), _resident(w_o.shape),
        ],
        out_specs=tok(D_MODEL),
        compiler_params=_params("parallel", "parallel"),
        name="merge",
    )(x, mod, y_attn, y_four, sig_a, sig_f, w_ab, w_fb, w_o)


def _rope_tables(n_tokens):
    rows = n_tokens // GRID_W
    row_ids = jnp.repeat(jnp.arange(rows, dtype=F32), GRID_W)
    col_ids = jnp.tile(jnp.arange(GRID_W, dtype=F32), rows)
    inv_freq = ROPE_THETA ** (-jnp.arange(0, AXIS_ROPE_DIM, 2, dtype=F32) / AXIS_ROPE_DIM)
    ang = jnp.concatenate([row_ids[:, None] * inv_freq, col_ids[:, None] * inv_freq], axis=-1)
    cos, sin = jnp.cos(ang), jnp.sin(ang)
    return jnp.concatenate([cos, cos], axis=-1), jnp.concatenate([-sin, sin], axis=-1)


def kernel(x, c, ctx, c_ctx, w_ada, b_ada, norm_ffn1, w_ffn1_in, w_ffn1_out, norm_mix, w_in,
           q_norm, k_norm, w_attn_branch, w_fourier_branch, w_out, norm_ffn2, w_ffn2_in, w_ffn2_out):
    nb, n_lat, _ = x.shape
    n_ctx = ctx.shape[1]
    depth = w_ada.shape[0]
    assert depth == 1, "the context stream is only carried through the (single) last layer"
    cos2, sin2 = _rope_tables(n_lat)
    ctx_row = nb
    pad_rows = -(nb + 1) % 8
    c_all = jnp.concatenate([c, c_ctx[None, :], jnp.zeros((pad_rows, D_MODEL), F32)], axis=0)

    i = 0
    mod = _adaln(c_all, w_ada[i], b_ada[i][None, :]).reshape(c_all.shape[0], N_MOD, D_MODEL)
    own_row, shared_row = (lambda b: b), (lambda b: ctx_row)
    x, ctx_flat = _ffn([x, ctx.reshape(1, nb * n_ctx, D_MODEL)], mod, [own_row, shared_row], 0,
                       norm_ffn1[i][None, :], w_ffn1_in[i], w_ffn1_out[i])

    q, k, vx, f, sig_a, sig_f, k_c, vx_c = _inproj(
        x, ctx_flat, mod, [own_row, shared_row], norm_mix[i][None, :], w_in[i],
        q_norm[i][None, :], k_norm[i][None, :], cos2, sin2)
    y_attn = _attention(q, k, vx, k_c.reshape(nb, n_ctx, KV_WIDTH), vx_c.reshape(nb, n_ctx, VX_WIDTH))
    y_four = _fourier(f)
    x = _merge(x, mod, y_attn, y_four, sig_a, sig_f, w_attn_branch[i], w_fourier_branch[i], w_out[i])

    return _ffn([x], mod, [own_row], 6, norm_ffn2[i][None, :], w_ffn2_in[i], w_ffn2_out[i])[0]
```

```python
import functools

import numpy as np
import jax
import jax.numpy as jnp
from jax import lax
from jax.experimental import pallas as pl
from jax.experimental.pallas import tpu as pltpu

D_MODEL = 1024
GRID_W = 64
HEAD_DIM = 128
N_Q_HEADS = D_MODEL // HEAD_DIM
N_KV_HEADS = N_Q_HEADS // 4
GQA_GROUP = N_Q_HEADS // N_KV_HEADS
Q_WIDTH = N_Q_HEADS * HEAD_DIM
KV_WIDTH = N_KV_HEADS * HEAD_DIM
FOURIER_GROUP = 128
N_FOURIER_GROUPS = 4
FOURIER_WIDTH = N_FOURIER_GROUPS * FOURIER_GROUP
D_FF = 2816
AXIS_ROPE_DIM = HEAD_DIM // 2
ROPE_THETA = 10000.0
EPS = 1e-6
N_MOD = 9
ATTN_SCALE = HEAD_DIM ** -0.5
LOG2_E = 1.4426950408889634

K_OFF = Q_WIDTH
V_OFF = K_OFF + KV_WIDTH
F_OFF = V_OFF + KV_WIDTH
GA_OFF = F_OFF + FOURIER_WIDTH
GF_OFF = GA_OFF + D_MODEL

V7X_MXU_DIM = 256
VX_WIDTH = N_KV_HEADS * V7X_MXU_DIM
V7X_VMEM_BYTES = 64 * 2**20
VMEM_LIMIT = V7X_VMEM_BYTES - 8 * 2**20

TOKEN_TILE = 512
WIDE_TOKEN_TILE = 1024
FF_CHUNK = V7X_MXU_DIM
ATTN_Q_TILE = 512
ADALN_TILE = 1152

F32 = jnp.float32
BF16 = jnp.bfloat16


def _params(*sem):
    return pltpu.CompilerParams(dimension_semantics=sem, vmem_limit_bytes=VMEM_LIMIT)


def _resident(shape):
    return pl.BlockSpec(shape, lambda *_: (0,) * len(shape), pipeline_mode=pl.Buffered(1))


def _rms(x, gain):
    return x * lax.rsqrt(jnp.mean(x * x, axis=-1, keepdims=True) + EPS) * gain


def _dot(a, b):
    return jnp.dot(a, b.astype(BF16), preferred_element_type=F32)


def _adaln_kernel(c_ref, w_ref, b_ref, o_ref):
    c = c_ref[...]
    a = (c * jax.nn.sigmoid(c)).astype(BF16)
    o_ref[...] = _dot(a, w_ref[...]) + b_ref[...]


def _adaln(c_all, w, b):
    rows, n_out = c_all.shape[0], w.shape[1]
    return pl.pallas_call(
        _adaln_kernel,
        out_shape=jax.ShapeDtypeStruct((rows, n_out), F32),
        grid=(n_out // ADALN_TILE,),
        in_specs=[
            pl.BlockSpec((rows, D_MODEL), lambda j: (0, 0)),
            pl.BlockSpec((D_MODEL, ADALN_TILE), lambda j: (0, j)),
            pl.BlockSpec((1, ADALN_TILE), lambda j: (0, j)),
        ],
        out_specs=pl.BlockSpec((rows, ADALN_TILE), lambda j: (0, j)),
        compiler_params=_params("arbitrary"),
        name="adaln",
    )(c_all, w, b)


class _TokenStreams:
    def __init__(self, arrays, tile):
        self.tile = tile
        self.per_batch = [a.shape[1] // tile for a in arrays]
        counts = [a.shape[0] * p for a, p in zip(arrays, self.per_batch)]
        self.starts = [sum(counts[:s]) for s in range(len(arrays) + 1)]
        self.steps = self.starts[-1]

    def batch_and_tile(self, s, t):
        local = jnp.clip(t - self.starts[s], 0, self.starts[s + 1] - self.starts[s] - 1)
        return local // self.per_batch[s], local % self.per_batch[s]

    def token_spec(self, s, width):
        def index(t):
            b, i = self.batch_and_tile(s, t)
            return b, i, 0
        return pl.BlockSpec((1, self.tile, width), index)

    def mod_spec(self, mod_rows):
        def index(t):
            row = mod_rows[0](self.batch_and_tile(0, t)[0])
            for s in range(1, len(mod_rows)):
                row = jnp.where(t >= self.starts[s], mod_rows[s](self.batch_and_tile(s, t)[0]), row)
            return row, 0, 0
        return pl.BlockSpec((1, N_MOD, D_MODEL), index)

    def active(self, s):
        t = pl.program_id(0)
        return jnp.logical_and(t >= self.starts[s], t < self.starts[s + 1])


def _ffn_kernel(*refs, streams, mod_base):
    n = len(streams.per_batch)
    x_refs, (mod_ref, gain_ref, win_ref, wout_ref), o_refs, act_ref = (
        refs[:n], refs[n:n + 4], refs[n + 4:2 * n + 4], refs[2 * n + 4])

    def half_step(x_ref, o_ref):
        x = x_ref[0]
        shift = mod_ref[0, mod_base:mod_base + 1, :]
        scale = mod_ref[0, mod_base + 1:mod_base + 2, :]
        gate = mod_ref[0, mod_base + 2:mod_base + 3, :]
        h = (_rms(x, gain_ref[...]) * (1.0 + scale) + shift).astype(BF16)
        for j in range(D_FF // FF_CHUNK):
            lo = j * FF_CHUNK
            g = _dot(h, win_ref[:, lo:lo + FF_CHUNK])
            u = _dot(h, win_ref[:, D_FF + lo:D_FF + lo + FF_CHUNK])
            act_ref[:, lo:lo + FF_CHUNK] = (g * jax.nn.sigmoid(g) * u).astype(BF16)
        y = _dot(act_ref[...], wout_ref[...])
        o_ref[0] = x + 0.5 * gate * y

    if n == 1:
        half_step(x_refs[0], o_refs[0])
    else:
        for s in range(n):
            pl.when(streams.active(s))(functools.partial(half_step, x_refs[s], o_refs[s]))


def _ffn(xs, mod, mod_rows, mod_base, gain, w_in, w_out):
    streams = _TokenStreams(xs, TOKEN_TILE)
    tok = [streams.token_spec(s, D_MODEL) for s in range(len(xs))]
    return pl.pallas_call(
        functools.partial(_ffn_kernel, streams=streams, mod_base=mod_base),
        out_shape=[jax.ShapeDtypeStruct(x.shape, F32) for x in xs],
        grid=(streams.steps,),
        in_specs=tok + [
            streams.mod_spec(mod_rows),
            _resident((1, D_MODEL)),
            _resident((D_MODEL, 2 * D_FF)),
            _resident((D_FF, D_MODEL)),
        ],
        out_specs=tok,
        scratch_shapes=[pltpu.VMEM((TOKEN_TILE, D_FF), BF16)],
        compiler_params=_params("arbitrary"),
        name="ffn",
    )(*xs, mod, gain, w_in, w_out)


def _head_norm(t, gain):
    return t * lax.rsqrt(jnp.mean(t * t, axis=-1, keepdims=True) + EPS) * gain


def _rope(t, cos2, sin2):
    return t * cos2 + pltpu.roll(t, AXIS_ROPE_DIM, 1) * sin2


def _store_widened_values(v_ref, v):
    ones = jnp.ones((v.shape[0], V7X_MXU_DIM - HEAD_DIM), BF16)
    for hd in range(N_KV_HEADS):
        lo = hd * V7X_MXU_DIM
        v_ref[0, :, lo:lo + HEAD_DIM] = v[:, hd * HEAD_DIM:(hd + 1) * HEAD_DIM].astype(BF16)
        v_ref[0, :, lo + HEAD_DIM:lo + V7X_MXU_DIM] = ones


def _inproj_kernel(x_ref, ctx_ref, mod_ref, gain_ref, w_ref, qn_ref, kn_ref, cos_ref, sin_ref,
                   q_ref, k_ref, v_ref, f_ref, ga_ref, gf_ref, kc_ref, vc_ref, *, streams):
    kn = kn_ref[...]
    heads_per_dot = V7X_MXU_DIM // HEAD_DIM

    def normed_input(tok_ref):
        shift = mod_ref[0, 3:4, :]
        scale = mod_ref[0, 4:5, :]
        return (_rms(tok_ref[0], gain_ref[...]) * (1.0 + scale) + shift).astype(BF16)

    @pl.when(streams.active(0))
    def _latent_tokens():
        h = normed_input(x_ref)

        def wide_dot(col):
            return _dot(h, w_ref[:, col:col + V7X_MXU_DIM])

        cos2 = cos_ref[...]
        sin2 = sin_ref[...]
        qn = qn_ref[...] * (ATTN_SCALE * LOG2_E)
        for c in range(Q_WIDTH // V7X_MXU_DIM):
            t = wide_dot(c * V7X_MXU_DIM)
            for j in range(heads_per_dot):
                tj = t[:, j * HEAD_DIM:(j + 1) * HEAD_DIM]
                q_ref[0, c * heads_per_dot + j] = _rope(_head_norm(tj, qn), cos2, sin2).astype(BF16)
        for c in range(KV_WIDTH // V7X_MXU_DIM):
            t = wide_dot(K_OFF + c * V7X_MXU_DIM)
            for j in range(heads_per_dot):
                lo = c * V7X_MXU_DIM + j * HEAD_DIM
                tj = t[:, j * HEAD_DIM:(j + 1) * HEAD_DIM]
                k_ref[0, :, lo:lo + HEAD_DIM] = _rope(_head_norm(tj, kn), cos2, sin2).astype(BF16)
        _store_widened_values(v_ref, _dot(h, w_ref[:, V_OFF:V_OFF + KV_WIDTH]))
        for c in range(FOURIER_WIDTH // V7X_MXU_DIM):
            lo = c * V7X_MXU_DIM
            f_ref[0, :, lo:lo + V7X_MXU_DIM] = wide_dot(F_OFF + lo).astype(BF16)
        for c in range(D_MODEL // V7X_MXU_DIM):
            lo = c * V7X_MXU_DIM
            ga_ref[0, :, lo:lo + V7X_MXU_DIM] = jax.nn.sigmoid(wide_dot(GA_OFF + lo)).astype(BF16)
            gf_ref[0, :, lo:lo + V7X_MXU_DIM] = jax.nn.sigmoid(wide_dot(GF_OFF + lo)).astype(BF16)

    @pl.when(streams.active(1))
    def _context_tokens():
        h = normed_input(ctx_ref)
        t = _dot(h, w_ref[:, K_OFF:K_OFF + KV_WIDTH])
        for hd in range(N_KV_HEADS):
            lo = hd * HEAD_DIM
            kc_ref[0, :, lo:lo + HEAD_DIM] = _head_norm(t[:, lo:lo + HEAD_DIM], kn).astype(BF16)
        _store_widened_values(vc_ref, _dot(h, w_ref[:, V_OFF:V_OFF + KV_WIDTH]))


def _inproj(x, ctx, mod, mod_rows, gain, w_in, q_norm, k_norm, cos2, sin2):
    nb, t, _ = x.shape
    streams = _TokenStreams([x, ctx], TOKEN_TILE)
    tm = streams.tile
    tok = lambda w: streams.token_spec(0, w)
    ctx_tok = lambda w: streams.token_spec(1, w)
    shp = lambda w: jax.ShapeDtypeStruct((nb, t, w), BF16)
    ctx_shp = lambda w: jax.ShapeDtypeStruct(ctx.shape[:2] + (w,), BF16)

    def q_index(step):
        b, i = streams.batch_and_tile(0, step)
        return b, 0, i, 0

    def rope_index(step):
        return streams.batch_and_tile(0, step)[1], 0

    return pl.pallas_call(
        functools.partial(_inproj_kernel, streams=streams),
        out_shape=[jax.ShapeDtypeStruct((nb, N_Q_HEADS, t, HEAD_DIM), BF16), shp(KV_WIDTH),
                   shp(VX_WIDTH), shp(FOURIER_WIDTH), shp(D_MODEL), shp(D_MODEL),
                   ctx_shp(KV_WIDTH), ctx_shp(VX_WIDTH)],
        grid=(streams.steps,),
        in_specs=[
            tok(D_MODEL),
            ctx_tok(D_MODEL),
            streams.mod_spec(mod_rows),
            _resident((1, D_MODEL)),
            _resident(w_in.shape),
            _resident((1, HEAD_DIM)),
            _resident((1, HEAD_DIM)),
            pl.BlockSpec((tm, HEAD_DIM), rope_index),
            pl.BlockSpec((tm, HEAD_DIM), rope_index),
        ],
        out_specs=[pl.BlockSpec((1, N_Q_HEADS, tm, HEAD_DIM), q_index),
                   tok(KV_WIDTH), tok(VX_WIDTH), tok(FOURIER_WIDTH), tok(D_MODEL), tok(D_MODEL),
                   ctx_tok(KV_WIDTH), ctx_tok(VX_WIDTH)],
        compiler_params=_params("arbitrary"),
        name="inproj",
    )(x, ctx, mod, gain, w_in, q_norm, k_norm, cos2, sin2)


_NT = (((1,), (1,)), ((), ()))


ATTN_SCRATCH_PER_HEAD = 5


def _attn_kernel(q_ref, kc_ref, k_ref, vcx_ref, vx_ref, o_ref, *scratch):
    heads = [scratch[ATTN_SCRATCH_PER_HEAD * g:ATTN_SCRATCH_PER_HEAD * (g + 1)] for g in range(GQA_GROUP)]

    @pl.when(pl.program_id(0) == 0)
    def _no_previous_item():
        for pc_ref, pl_ref, sc_ref, sl_ref, m_ref in heads:
            pc_ref[...] = jnp.ones(pc_ref.shape, BF16)
            pl_ref[...] = jnp.ones(pl_ref.shape, BF16)
            sc_ref[...] = jnp.zeros(sc_ref.shape, F32)
            sl_ref[...] = jnp.zeros(sl_ref.shape, F32)
            m_ref[...] = jnp.zeros(m_ref.shape, F32)

    def weights_from_scores(pc_ref, pl_ref, sc_ref, sl_ref, m_ref):
        m = m_ref[...]
        for s_ref, p_ref in ((sc_ref, pc_ref), (sl_ref, pl_ref)):
            for lo in range(0, s_ref.shape[1], V7X_MXU_DIM):
                p_ref[:, lo:lo + V7X_MXU_DIM] = jnp.exp2(s_ref[:, lo:lo + V7X_MXU_DIM] - m).astype(BF16)

    weights_from_scores(*heads[-1])

    for g, (pc_ref, pl_ref, sc_ref, sl_ref, m_ref) in enumerate(heads):
        oe = _dot(pc_ref[...], vcx_ref[0]) + _dot(pl_ref[...], vx_ref[0])
        o_ref[0, :, g * HEAD_DIM:(g + 1) * HEAD_DIM] = (oe[:, :HEAD_DIM] / oe[:, HEAD_DIM:]).astype(BF16)

        q = q_ref[0, g]
        s_c = lax.dot_general(q, kc_ref[0], _NT, preferred_element_type=F32)
        s_l = lax.dot_general(q, k_ref[0], _NT, preferred_element_type=F32)
        m_ref[...] = jnp.maximum(jnp.max(s_c, axis=-1, keepdims=True), jnp.max(s_l, axis=-1, keepdims=True))
        sc_ref[...] = s_c
        sl_ref[...] = s_l

    for head in heads[:-1]:
        weights_from_scores(*head)


def _attention(q, k, vx, k_c, vx_c):
    nb, _, n, _ = q.shape
    n_ctx = k_c.shape[1]
    tq = ATTN_Q_TILE
    tiles = n // tq
    items = nb * N_KV_HEADS * tiles

    def item(t):
        return t // (N_KV_HEADS * tiles), (t // tiles) % N_KV_HEADS, t % tiles

    def scored(t):
        return item(jnp.minimum(t, items - 1))

    def weighted(t):
        return item(jnp.maximum(t - 1, 0))

    def q_map(t):
        b, h, i = scored(t)
        return b, h, i, 0

    def k_map(t):
        b, h, _ = scored(t)
        return b, 0, h

    def v_map(t):
        b, h, _ = weighted(t)
        return b, 0, h

    def o_map(t):
        b, h, i = weighted(t)
        return b, i, h

    return pl.pallas_call(
        _attn_kernel,
        out_shape=jax.ShapeDtypeStruct((nb, n, Q_WIDTH), BF16),
        grid=(items + 1,),
        in_specs=[
            pl.BlockSpec((1, GQA_GROUP, tq, HEAD_DIM), q_map),
            pl.BlockSpec((1, n_ctx, HEAD_DIM), k_map),
            pl.BlockSpec((1, n, HEAD_DIM), k_map),
            pl.BlockSpec((1, n_ctx, V7X_MXU_DIM), v_map),
            pl.BlockSpec((1, n, V7X_MXU_DIM), v_map),
        ],
        out_specs=pl.BlockSpec((1, tq, GQA_GROUP * HEAD_DIM), o_map),
        scratch_shapes=GQA_GROUP * [
            pltpu.VMEM((tq, n_ctx), BF16), pltpu.VMEM((tq, n), BF16),
            pltpu.VMEM((tq, n_ctx), F32), pltpu.VMEM((tq, n), F32), pltpu.VMEM((tq, 1), F32)],
        compiler_params=_params("arbitrary"),
        name="attn",
    )(q, k_c, k, vx_c, vx)


DFT_RADIX = 8


def _dft_tables(n):
    r = n // DFT_RADIX
    k = np.arange(DFT_RADIX)[:, None, None] + DFT_RADIX * np.arange(r)[None, :, None]
    ang = 2.0 * np.pi * ((k * np.arange(r)[None, None, :]) % n) / n
    cos, sin = np.cos(ang), np.sin(ang)
    pos = np.concatenate([np.concatenate([cos, sin], axis=2),
                          np.concatenate([-sin, cos], axis=2)], axis=1)
    c = FOURIER_GROUP
    scale = 1.0 / np.sqrt(float(n * c))
    ang_c = 2.0 * np.pi * (np.outer(np.arange(c), np.arange(c)) % c) / c
    eye = np.eye(V7X_MXU_DIM // c)
    chan = np.concatenate([np.kron(eye, np.cos(ang_c) * scale),
                           np.kron(eye, np.sin(ang_c) * scale)], axis=0)
    return jnp.asarray(pos, F32).astype(BF16), jnp.asarray(chan, F32).astype(BF16)


def _fold_radix8(x):
    c = 0.5 ** 0.5
    e02p, e02m, e13p, e13m = x[0] + x[4], x[0] - x[4], x[2] + x[6], x[2] - x[6]
    o02p, o02m, o13p, o13m = x[1] + x[5], x[1] - x[5], x[3] + x[7], x[3] - x[7]
    e0, e2, o0, o2 = e02p + e13p, e02p - e13p, o02p + o13p, o02p - o13p
    p, q = c * (o02m - o13m), c * (o02m + o13m)
    re = [e0 + o0, e02m + p, e2, e02m - p, e0 - o0, e02m - p, e2, e02m + p]
    im1, im3 = -e13m - q, e13m - q
    im = [None, im1, -o2, im3, None, -im3, o2, -im1]
    return re, im


def _fourier_kernel(f_ref, pos_ref, chan_ref, o_ref, t_ref, zr_ref, zi_ref, y_ref):
    r = f_ref.shape[1] // DFT_RADIX
    w = FOURIER_WIDTH
    pair = V7X_MXU_DIM
    slab = y_ref.shape[2]
    for lo in range(0, w, pair):
        for sub in range(lo, lo + pair, FOURIER_GROUP):
            lanes = slice(sub, sub + FOURIER_GROUP)
            x = [f_ref[0, n1 * r:(n1 + 1) * r, lanes].astype(F32) for n1 in range(DFT_RADIX)]
            re, im = _fold_radix8(x)
            for k1 in range(DFT_RADIX):
                t_ref[k1, 0:r, lanes] = re[k1].astype(BF16)
                t_ref[k1, r:2 * r, lanes] = (
                    jnp.zeros((r, FOURIER_GROUP), BF16) if im[k1] is None else im[k1].astype(BF16))
        h = lo // pair
        for k1 in range(DFT_RADIX):
            z = _dot(pos_ref[k1], t_ref[k1, :, lo:lo + pair])
            zr_ref[h, k1 * r:(k1 + 1) * r, :] = z[0:r].astype(BF16)
            zi_ref[h, k1 * r:(k1 + 1) * r, :] = z[r:2 * r].astype(BF16)
        y = _dot(zr_ref[h], chan_ref[0:pair, :]) + _dot(zi_ref[h], chan_ref[pair:2 * pair, :])
        for k1 in range(DFT_RADIX):
            for j in range(pair // slab):
                y_ref[(lo + j * slab) // slab, pl.ds(k1, r, stride=DFT_RADIX), :] = (
                    y[k1 * r:(k1 + 1) * r, j * slab:(j + 1) * slab])
    for j in range(w // slab):
        o_ref[0, :, j * slab:(j + 1) * slab] = y_ref[j].astype(BF16)


def _fourier(f):
    nb, n, w = f.shape
    pos, chan = _dft_tables(n)
    r = n // DFT_RADIX
    blk = pl.BlockSpec((1, n, w), lambda b: (b, 0, 0))
    lanes = 128
    return pl.pallas_call(
        _fourier_kernel,
        out_shape=jax.ShapeDtypeStruct(f.shape, BF16),
        grid=(nb,),
        in_specs=[blk, _resident(pos.shape), _resident(chan.shape)],
        out_specs=blk,
        scratch_shapes=[pltpu.VMEM((DFT_RADIX, 2 * r, w), BF16),
                        pltpu.VMEM((w // V7X_MXU_DIM, n, V7X_MXU_DIM), BF16),
                        pltpu.VMEM((w // V7X_MXU_DIM, n, V7X_MXU_DIM), BF16),
                        pltpu.VMEM((w // lanes, n, lanes), F32)],
        compiler_params=_params("parallel"),
        name="fourier",
    )(f, pos, chan)


def _merge_kernel(x_ref, mod_ref, ya_ref, yf_ref, ga_ref, gf_ref, wab_ref, wfb_ref, wo_ref, o_ref):
    gate = mod_ref[0, 5:6, :]
    merged = (ga_ref[0].astype(F32) * _dot(ya_ref[0], wab_ref[...])
              + gf_ref[0].astype(F32) * _dot(yf_ref[0], wfb_ref[...]))
    mix = _dot(merged.astype(BF16), wo_ref[...])
    o_ref[0] = x_ref[0] + gate * mix


def _merge(x, mod, y_attn, y_four, sig_a, sig_f, w_ab, w_fb, w_o):
    nb, t, _ = x.shape
    tm = WIDE_TOKEN_TILE
    tok = lambda w: pl.BlockSpec((1, tm, w), lambda b, i: (b, i, 0))
    return pl.pallas_call(
        _merge_kernel,
        out_shape=jax.ShapeDtypeStruct(x.shape, F32),
        grid=(nb, t // tm),
        in_specs=[
            tok(D_MODEL),
            pl.BlockSpec((1, N_MOD, D_MODEL), lambda b, i: (b, 0, 0)),
            tok(Q_WIDTH), tok(FOURIER_WIDTH), tok(D_MODEL), tok(D_MODEL),
            _resident(w_ab.shape), _resident(w_fb.shape), _resident(w_o.shape),
        ],
        out_specs=tok(D_MODEL),
        compiler_params=_params("parallel", "parallel"),
        name="merge",
    )(x, mod, y_attn, y_four, sig_a, sig_f, w_ab, w_fb, w_o)


def _rope_tables(n_tokens):
    rows = n_tokens // GRID_W
    row_ids = jnp.repeat(jnp.arange(rows, dtype=F32), GRID_W)
    col_ids = jnp.tile(jnp.arange(GRID_W, dtype=F32), rows)
    inv_freq = ROPE_THETA ** (-jnp.arange(0, AXIS_ROPE_DIM, 2, dtype=F32) / AXIS_ROPE_DIM)
    ang = jnp.concatenate([row_ids[:, None] * inv_freq, col_ids[:, None] * inv_freq], axis=-1)
    cos, sin = jnp.cos(ang), jnp.sin(ang)
    return jnp.concatenate([cos, cos], axis=-1), jnp.concatenate([-sin, sin], axis=-1)


def kernel(x, c, ctx, c_ctx, w_ada, b_ada, norm_ffn1, w_ffn1_in, w_ffn1_out, norm_mix, w_in,
           q_norm, k_norm, w_attn_branch, w_fourier_branch, w_out, norm_ffn2, w_ffn2_in, w_ffn2_out):
    nb, n_lat, _ = x.shape
    n_ctx = ctx.shape[1]
    depth = w_ada.shape[0]
    assert depth == 1, "the context stream is only carried through the (single) last layer"
    cos2, sin2 = _rope_tables(n_lat)
    ctx_row = nb
    pad_rows = -(nb + 1) % 8
    c_all = jnp.concatenate([c, c_ctx[None, :], jnp.zeros((pad_rows, D_MODEL), F32)], axis=0)

    i = 0
    mod = _adaln(c_all, w_ada[i], b_ada[i][None, :]).reshape(c_all.shape[0], N_MOD, D_MODEL)
    own_row, shared_row = (lambda b: b), (lambda b: ctx_row)
    x, ctx_flat = _ffn([x, ctx.reshape(1, nb * n_ctx, D_MODEL)], mod, [own_row, shared_row], 0,
                       norm_ffn1[i][None, :], w_ffn1_in[i], w_ffn1_out[i])

    q, k, vx, f, sig_a, sig_f, k_c, vx_c = _inproj(
        x, ctx_flat, mod, [own_row, shared_row], norm_mix[i][None, :], w_in[i],
        q_norm[i][None, :], k_norm[i][None, :], cos2, sin2)
    y_attn = _attention(q, k, vx, k_c.reshape(nb, n_ctx, KV_WIDTH), vx_c.reshape(nb, n_ctx, VX_WIDTH))
    y_four = _fourier(f)
    x = _merge(x, mod, y_attn, y_four, sig_a, sig_f, w_attn_branch[i], w_fourier_branch[i], w_out[i])

    return _ffn([x], mod, [own_row], 6, norm_ffn2[i][None, :], w_ffn2_in[i], w_ffn2_out[i])[0]
```

```python
import functools

import numpy as np
import jax
import jax.numpy as jnp
from jax import lax
from jax.experimental import pallas as pl
from jax.experimental.pallas import tpu as pltpu

D_MODEL = 1024
GRID_W = 64
HEAD_DIM = 128
N_Q_HEADS = D_MODEL // HEAD_DIM
N_KV_HEADS = N_Q_HEADS // 4
GQA_GROUP = N_Q_HEADS // N_KV_HEADS
Q_WIDTH = N_Q_HEADS * HEAD_DIM
KV_WIDTH = N_KV_HEADS * HEAD_DIM
FOURIER_GROUP = 128
N_FOURIER_GROUPS = 4
FOURIER_WIDTH = N_FOURIER_GROUPS * FOURIER_GROUP
D_FF = 2816
AXIS_ROPE_DIM = HEAD_DIM // 2
ROPE_THETA = 10000.0
EPS = 1e-6
N_MOD = 9
ATTN_SCALE = HEAD_DIM ** -0.5
LOG2_E = 1.4426950408889634

K_OFF = Q_WIDTH
V_OFF = K_OFF + KV_WIDTH
F_OFF = V_OFF + KV_WIDTH
GA_OFF = F_OFF + FOURIER_WIDTH
GF_OFF = GA_OFF + D_MODEL

V7X_MXU_DIM = 256
VX_WIDTH = N_KV_HEADS * V7X_MXU_DIM
V7X_VMEM_BYTES = 64 * 2**20
VMEM_LIMIT = V7X_VMEM_BYTES - 8 * 2**20

TOKEN_TILE = 512
WIDE_TOKEN_TILE = 1024
FF_CHUNK = V7X_MXU_DIM
ATTN_Q_TILE = 512
ADALN_TILE = 1152

F32 = jnp.float32
BF16 = jnp.bfloat16


def _params(*sem):
    return pltpu.CompilerParams(dimension_semantics=sem, vmem_limit_bytes=VMEM_LIMIT)


def _resident(shape):
    return pl.BlockSpec(shape, lambda *_: (0,) * len(shape), pipeline_mode=pl.Buffered(1))


def _rms(x, gain):
    return x * lax.rsqrt(jnp.mean(x * x, axis=-1, keepdims=True) + EPS) * gain


def _dot(a, b):
    return jnp.dot(a, b.astype(BF16), preferred_element_type=F32)


def _sigmoid(x):
    return 0.5 * jnp.tanh(0.5 * x) + 0.5


def _adaln_kernel(c_ref, w_ref, b_ref, o_ref):
    c = c_ref[...]
    a = (c * _sigmoid(c)).astype(BF16)
    o_ref[...] = _dot(a, w_ref[...]) + b_ref[...]


def _adaln(c_all, w, b):
    rows, n_out = c_all.shape[0], w.shape[1]
    return pl.pallas_call(
        _adaln_kernel,
        out_shape=jax.ShapeDtypeStruct((rows, n_out), F32),
        grid=(n_out // ADALN_TILE,),
        in_specs=[
            pl.BlockSpec((rows, D_MODEL), lambda j: (0, 0)),
            pl.BlockSpec((D_MODEL, ADALN_TILE), lambda j: (0, j)),
            pl.BlockSpec((1, ADALN_TILE), lambda j: (0, j)),
        ],
        out_specs=pl.BlockSpec((rows, ADALN_TILE), lambda j: (0, j)),
        compiler_params=_params("arbitrary"),
        name="adaln",
    )(c_all, w, b)


class _TokenStreams:
    def __init__(self, arrays, tile):
        self.tile = tile
        self.per_batch = [a.shape[1] // tile for a in arrays]
        counts = [a.shape[0] * p for a, p in zip(arrays, self.per_batch)]
        self.starts = [sum(counts[:s]) for s in range(len(arrays) + 1)]
        self.steps = self.starts[-1]

    def batch_and_tile(self, s, t):
        local = jnp.clip(t - self.starts[s], 0, self.starts[s + 1] - self.starts[s] - 1)
        return local // self.per_batch[s], local % self.per_batch[s]

    def token_spec(self, s, width):
        def index(t):
            b, i = self.batch_and_tile(s, t)
            return b, i, 0
        return pl.BlockSpec((1, self.tile, width), index)

    def mod_spec(self, mod_rows):
        def index(t):
            row = mod_rows[0](self.batch_and_tile(0, t)[0])
            for s in range(1, len(mod_rows)):
                row = jnp.where(t >= self.starts[s], mod_rows[s](self.batch_and_tile(s, t)[0]), row)
            return row, 0, 0
        return pl.BlockSpec((1, N_MOD, D_MODEL), index)

    def active(self, s):
        t = pl.program_id(0)
        return jnp.logical_and(t >= self.starts[s], t < self.starts[s + 1])


def _ffn_kernel(*refs, streams, mod_base):
    n = len(streams.per_batch)
    x_refs, (mod_ref, gain_ref, win_ref, wout_ref), o_refs, act_ref = (
        refs[:n], refs[n:n + 4], refs[n + 4:2 * n + 4], refs[2 * n + 4])

    def half_step(x_ref, o_ref):
        x = x_ref[0]
        shift = mod_ref[0, mod_base:mod_base + 1, :]
        scale = mod_ref[0, mod_base + 1:mod_base + 2, :]
        gate = mod_ref[0, mod_base + 2:mod_base + 3, :]
        h = (_rms(x, gain_ref[...]) * (1.0 + scale) + shift).astype(BF16)
        for j in range(D_FF // FF_CHUNK):
            lo = j * FF_CHUNK
            g = _dot(h, win_ref[:, lo:lo + FF_CHUNK])
            u = _dot(h, win_ref[:, D_FF + lo:D_FF + lo + FF_CHUNK])
            act_ref[:, lo:lo + FF_CHUNK] = (g * _sigmoid(g) * u).astype(BF16)
        y = _dot(act_ref[...], wout_ref[...])
        o_ref[0] = x + 0.5 * gate * y

    if n == 1:
        half_step(x_refs[0], o_refs[0])
    else:
        for s in range(n):
            pl.when(streams.active(s))(functools.partial(half_step, x_refs[s], o_refs[s]))


def _ffn(xs, mod, mod_rows, mod_base, gain, w_in, w_out):
    streams = _TokenStreams(xs, TOKEN_TILE)
    tok = [streams.token_spec(s, D_MODEL) for s in range(len(xs))]
    return pl.pallas_call(
        functools.partial(_ffn_kernel, streams=streams, mod_base=mod_base),
        out_shape=[jax.ShapeDtypeStruct(x.shape, F32) for x in xs],
        grid=(streams.steps,),
        in_specs=tok + [
            streams.mod_spec(mod_rows),
            _resident((1, D_MODEL)),
            _resident((D_MODEL, 2 * D_FF)),
            _resident((D_FF, D_MODEL)),
        ],
        out_specs=tok,
        scratch_shapes=[pltpu.VMEM((TOKEN_TILE, D_FF), BF16)],
        compiler_params=_params("arbitrary"),
        name="ffn",
    )(*xs, mod, gain, w_in, w_out)


def _head_norm(t, gain):
    return t * lax.rsqrt(jnp.mean(t * t, axis=-1, keepdims=True) + EPS) * gain


def _rope(t, cos2, sin2):
    return t * cos2 + pltpu.roll(t, AXIS_ROPE_DIM, 1) * sin2


def _store_widened_values(v_ref, v):
    ones = jnp.ones((v.shape[0], V7X_MXU_DIM - HEAD_DIM), BF16)
    for hd in range(N_KV_HEADS):
        lo = hd * V7X_MXU_DIM
        v_ref[0, :, lo:lo + HEAD_DIM] = v[:, hd * HEAD_DIM:(hd + 1) * HEAD_DIM].astype(BF16)
        v_ref[0, :, lo + HEAD_DIM:lo + V7X_MXU_DIM] = ones


def _inproj_kernel(x_ref, ctx_ref, mod_ref, gain_ref, w_ref, qn_ref, kn_ref, cos_ref, sin_ref,
                   q_ref, k_ref, v_ref, f_ref, ga_ref, gf_ref, kc_ref, vc_ref, *, streams):
    kn = kn_ref[...]
    heads_per_dot = V7X_MXU_DIM // HEAD_DIM

    def normed_input(tok_ref):
        shift = mod_ref[0, 3:4, :]
        scale = mod_ref[0, 4:5, :]
        return (_rms(tok_ref[0], gain_ref[...]) * (1.0 + scale) + shift).astype(BF16)

    @pl.when(streams.active(0))
    def _latent_tokens():
        h = normed_input(x_ref)

        def wide_dot(col):
            return _dot(h, w_ref[:, col:col + V7X_MXU_DIM])

        cos2 = cos_ref[...]
        sin2 = sin_ref[...]
        qn = qn_ref[...] * (ATTN_SCALE * LOG2_E)
        for c in range(Q_WIDTH // V7X_MXU_DIM):
            t = wide_dot(c * V7X_MXU_DIM)
            for j in range(heads_per_dot):
                tj = t[:, j * HEAD_DIM:(j + 1) * HEAD_DIM]
                q_ref[0, c * heads_per_dot + j] = _rope(_head_norm(tj, qn), cos2, sin2).astype(BF16)
        for c in range(KV_WIDTH // V7X_MXU_DIM):
            t = wide_dot(K_OFF + c * V7X_MXU_DIM)
            for j in range(heads_per_dot):
                lo = c * V7X_MXU_DIM + j * HEAD_DIM
                tj = t[:, j * HEAD_DIM:(j + 1) * HEAD_DIM]
                k_ref[0, :, lo:lo + HEAD_DIM] = _rope(_head_norm(tj, kn), cos2, sin2).astype(BF16)
        _store_widened_values(v_ref, _dot(h, w_ref[:, V_OFF:V_OFF + KV_WIDTH]))
        for c in range(FOURIER_WIDTH // V7X_MXU_DIM):
            lo = c * V7X_MXU_DIM
            f_ref[0, :, lo:lo + V7X_MXU_DIM] = wide_dot(F_OFF + lo).astype(BF16)
        for c in range(D_MODEL // V7X_MXU_DIM):
            lo = c * V7X_MXU_DIM
            ga_ref[0, :, lo:lo + V7X_MXU_DIM] = _sigmoid(wide_dot(GA_OFF + lo)).astype(BF16)
            gf_ref[0, :, lo:lo + V7X_MXU_DIM] = _sigmoid(wide_dot(GF_OFF + lo)).astype(BF16)

    @pl.when(streams.active(1))
    def _context_tokens():
        h = normed_input(ctx_ref)
        t = _dot(h, w_ref[:, K_OFF:K_OFF + KV_WIDTH])
        for hd in range(N_KV_HEADS):
            lo = hd * HEAD_DIM
            kc_ref[0, :, lo:lo + HEAD_DIM] = _head_norm(t[:, lo:lo + HEAD_DIM], kn).astype(BF16)
        _store_widened_values(vc_ref, _dot(h, w_ref[:, V_OFF:V_OFF + KV_WIDTH]))


def _inproj(x, ctx, mod, mod_rows, gain, w_in, q_norm, k_norm, cos2, sin2):
    nb, t, _ = x.shape
    streams = _TokenStreams([x, ctx], TOKEN_TILE)
    tm = streams.tile
    tok = lambda w: streams.token_spec(0, w)
    ctx_tok = lambda w: streams.token_spec(1, w)
    shp = lambda w: jax.ShapeDtypeStruct((nb, t, w), BF16)
    ctx_shp = lambda w: jax.ShapeDtypeStruct(ctx.shape[:2] + (w,), BF16)

    def q_index(step):
        b, i = streams.batch_and_tile(0, step)
        return b, 0, i, 0

    def rope_index(step):
        return streams.batch_and_tile(0, step)[1], 0

    return pl.pallas_call(
        functools.partial(_inproj_kernel, streams=streams),
        out_shape=[jax.ShapeDtypeStruct((nb, N_Q_HEADS, t, HEAD_DIM), BF16), shp(KV_WIDTH),
                   shp(VX_WIDTH), shp(FOURIER_WIDTH), shp(D_MODEL), shp(D_MODEL),
                   ctx_shp(KV_WIDTH), ctx_shp(VX_WIDTH)],
        grid=(streams.steps,),
        in_specs=[
            tok(D_MODEL),
            ctx_tok(D_MODEL),
            streams.mod_spec(mod_rows),
            _resident((1, D_MODEL)),
            _resident(w_in.shape),
            _resident((1, HEAD_DIM)),
            _resident((1, HEAD_DIM)),
            pl.BlockSpec((tm, HEAD_DIM), rope_index),
            pl.BlockSpec((tm, HEAD_DIM), rope_index),
        ],
        out_specs=[pl.BlockSpec((1, N_Q_HEADS, tm, HEAD_DIM), q_index),
                   tok(KV_WIDTH), tok(VX_WIDTH), tok(FOURIER_WIDTH), tok(D_MODEL), tok(D_MODEL),
                   ctx_tok(KV_WIDTH), ctx_tok(VX_WIDTH)],
        compiler_params=_params("arbitrary"),
        name="inproj",
    )(x, ctx, mod, gain, w_in, q_norm, k_norm, cos2, sin2)


_NT = (((1,), (1,)), ((), ()))


ATTN_SCRATCH_PER_HEAD = 5


def _attn_kernel(q_ref, kc_ref, k_ref, vcx_ref, vx_ref, o_ref, *scratch):
    heads = [scratch[ATTN_SCRATCH_PER_HEAD * g:ATTN_SCRATCH_PER_HEAD * (g + 1)] for g in range(GQA_GROUP)]

    @pl.when(pl.program_id(0) == 0)
    def _no_previous_item():
        for pc_ref, pl_ref, sc_ref, sl_ref, m_ref in heads:
            pc_ref[...] = jnp.ones(pc_ref.shape, BF16)
            pl_ref[...] = jnp.ones(pl_ref.shape, BF16)
            sc_ref[...] = jnp.zeros(sc_ref.shape, F32)
            sl_ref[...] = jnp.zeros(sl_ref.shape, F32)
            m_ref[...] = jnp.zeros(m_ref.shape, F32)

    def weights_from_scores(pc_ref, pl_ref, sc_ref, sl_ref, m_ref):
        m = m_ref[...]
        for s_ref, p_ref in ((sc_ref, pc_ref), (sl_ref, pl_ref)):
            for lo in range(0, s_ref.shape[1], V7X_MXU_DIM):
                p_ref[:, lo:lo + V7X_MXU_DIM] = jnp.exp2(s_ref[:, lo:lo + V7X_MXU_DIM] - m).astype(BF16)

    weights_from_scores(*heads[-1])

    for g, (pc_ref, pl_ref, sc_ref, sl_ref, m_ref) in enumerate(heads):
        oe = _dot(pc_ref[...], vcx_ref[0]) + _dot(pl_ref[...], vx_ref[0])
        o_ref[0, :, g * HEAD_DIM:(g + 1) * HEAD_DIM] = (oe[:, :HEAD_DIM] / oe[:, HEAD_DIM:]).astype(BF16)

        q = q_ref[0, g]
        s_c = lax.dot_general(q, kc_ref[0], _NT, preferred_element_type=F32)
        s_l = lax.dot_general(q, k_ref[0], _NT, preferred_element_type=F32)
        m_ref[...] = jnp.maximum(jnp.max(s_c, axis=-1, keepdims=True), jnp.max(s_l, axis=-1, keepdims=True))
        sc_ref[...] = s_c
        sl_ref[...] = s_l

    for head in heads[:-1]:
        weights_from_scores(*head)


def _attention(q, k, vx, k_c, vx_c):
    nb, _, n, _ = q.shape
    n_ctx = k_c.shape[1]
    tq = ATTN_Q_TILE
    tiles = n // tq
    items = nb * N_KV_HEADS * tiles

    def item(t):
        return t // (N_KV_HEADS * tiles), (t // tiles) % N_KV_HEADS, t % tiles

    def scored(t):
        return item(jnp.minimum(t, items - 1))

    def weighted(t):
        return item(jnp.maximum(t - 1, 0))

    def q_map(t):
        b, h, i = scored(t)
        return b, h, i, 0

    def k_map(t):
        b, h, _ = scored(t)
        return b, 0, h

    def v_map(t):
        b, h, _ = weighted(t)
        return b, 0, h

    def o_map(t):
        b, h, i = weighted(t)
        return b, i, h

    return pl.pallas_call(
        _attn_kernel,
        out_shape=jax.ShapeDtypeStruct((nb, n, Q_WIDTH), BF16),
        grid=(items + 1,),
        in_specs=[
            pl.BlockSpec((1, GQA_GROUP, tq, HEAD_DIM), q_map),
            pl.BlockSpec((1, n_ctx, HEAD_DIM), k_map),
            pl.BlockSpec((1, n, HEAD_DIM), k_map),
            pl.BlockSpec((1, n_ctx, V7X_MXU_DIM), v_map),
            pl.BlockSpec((1, n, V7X_MXU_DIM), v_map),
        ],
        out_specs=pl.BlockSpec((1, tq, GQA_GROUP * HEAD_DIM), o_map),
        scratch_shapes=GQA_GROUP * [
            pltpu.VMEM((tq, n_ctx), BF16), pltpu.VMEM((tq, n), BF16),
            pltpu.VMEM((tq, n_ctx), F32), pltpu.VMEM((tq, n), F32), pltpu.VMEM((tq, 1), F32)],
        compiler_params=_params("arbitrary"),
        name="attn",
    )(q, k_c, k, vx_c, vx)


DFT_RADIX = 8


def _dft_tables(n):
    r = n // DFT_RADIX
    k = np.arange(DFT_RADIX)[:, None, None] + DFT_RADIX * np.arange(r)[None, :, None]
    ang = 2.0 * np.pi * ((k * np.arange(r)[None, None, :]) % n) / n
    cos, sin = np.cos(ang), np.sin(ang)
    pos = np.concatenate([np.concatenate([cos, sin], axis=2),
                          np.concatenate([-sin, cos], axis=2)], axis=1)
    c = FOURIER_GROUP
    scale = 1.0 / np.sqrt(float(n * c))
    ang_c = 2.0 * np.pi * (np.outer(np.arange(c), np.arange(c)) % c) / c
    eye = np.eye(V7X_MXU_DIM // c)
    chan = np.concatenate([np.kron(eye, np.cos(ang_c) * scale),
                           np.kron(eye, np.sin(ang_c) * scale)], axis=0)
    return jnp.asarray(pos, F32).astype(BF16), jnp.asarray(chan, F32).astype(BF16)


def _fold_radix8(x):
    c = 0.5 ** 0.5
    e02p, e02m, e13p, e13m = x[0] + x[4], x[0] - x[4], x[2] + x[6], x[2] - x[6]
    o02p, o02m, o13p, o13m = x[1] + x[5], x[1] - x[5], x[3] + x[7], x[3] - x[7]
    e0, e2, o0, o2 = e02p + e13p, e02p - e13p, o02p + o13p, o02p - o13p
    p, q = c * (o02m - o13m), c * (o02m + o13m)
    re = [e0 + o0, e02m + p, e2, e02m - p, e0 - o0, e02m - p, e2, e02m + p]
    im1, im3 = -e13m - q, e13m - q
    im = [None, im1, -o2, im3, None, -im3, o2, -im1]
    return re, im


def _fourier_kernel(f_ref, pos_ref, chan_ref, o_ref, t_ref, zr_ref, zi_ref, y_ref):
    r = f_ref.shape[1] // DFT_RADIX
    w = FOURIER_WIDTH
    pair = V7X_MXU_DIM
    slab = y_ref.shape[2]
    for lo in range(0, w, pair):
        for sub in range(lo, lo + pair, FOURIER_GROUP):
            lanes = slice(sub, sub + FOURIER_GROUP)
            x = [f_ref[0, n1 * r:(n1 + 1) * r, lanes].astype(F32) for n1 in range(DFT_RADIX)]
            re, im = _fold_radix8(x)
            for k1 in range(DFT_RADIX):
                t_ref[k1, 0:r, lanes] = re[k1].astype(BF16)
                t_ref[k1, r:2 * r, lanes] = (
                    jnp.zeros((r, FOURIER_GROUP), BF16) if im[k1] is None else im[k1].astype(BF16))
        h = lo // pair
        for k1 in range(DFT_RADIX):
            z = _dot(pos_ref[k1], t_ref[k1, :, lo:lo + pair])
            zr_ref[h, k1 * r:(k1 + 1) * r, :] = z[0:r].astype(BF16)
            zi_ref[h, k1 * r:(k1 + 1) * r, :] = z[r:2 * r].astype(BF16)
        y = _dot(zr_ref[h], chan_ref[0:pair, :]) + _dot(zi_ref[h], chan_ref[pair:2 * pair, :])
        for k1 in range(DFT_RADIX):
            for j in range(pair // slab):
                y_ref[(lo + j * slab) // slab, pl.ds(k1, r, stride=DFT_RADIX), :] = (
                    y[k1 * r:(k1 + 1) * r, j * slab:(j + 1) * slab])
    for j in range(w // slab):
        o_ref[0, :, j * slab:(j + 1) * slab] = y_ref[j].astype(BF16)


def _fourier(f):
    nb, n, w = f.shape
    pos, chan = _dft_tables(n)
    r = n // DFT_RADIX
    blk = pl.BlockSpec((1, n, w), lambda b: (b, 0, 0))
    lanes = 128
    return pl.pallas_call(
        _fourier_kernel,
        out_shape=jax.ShapeDtypeStruct(f.shape, BF16),
        grid=(nb,),
        in_specs=[blk, _resident(pos.shape), _resident(chan.shape)],
        out_specs=blk,
        scratch_shapes=[pltpu.VMEM((DFT_RADIX, 2 * r, w), BF16),
                        pltpu.VMEM((w // V7X_MXU_DIM, n, V7X_MXU_DIM), BF16),
                        pltpu.VMEM((w // V7X_MXU_DIM, n, V7X_MXU_DIM), BF16),
                        pltpu.VMEM((w // lanes, n, lanes), F32)],
        compiler_params=_params("parallel"),
        name="fourier",
    )(f, pos, chan)


def _merge_kernel(x_ref, mod_ref, ya_ref, yf_ref, ga_ref, gf_ref, wab_ref, wfb_ref, wo_ref, o_ref):
    gate = mod_ref[0, 5:6, :]
    merged = (ga_ref[0].astype(F32) * _dot(ya_ref[0], wab_ref[...])
              + gf_ref[0].astype(F32) * _dot(yf_ref[0], wfb_ref[...]))
    mix = _dot(merged.astype(BF16), wo_ref[...])
    o_ref[0] = x_ref[0] + gate * mix


def _merge(x, mod, y_attn, y_four, sig_a, sig_f, w_ab, w_fb, w_o):
    nb, t, _ = x.shape
    tm = WIDE_TOKEN_TILE
    tok = lambda w: pl.BlockSpec((1, tm, w), lambda b, i: (b, i, 0))
    return pl.pallas_call(
        _merge_kernel,
        out_shape=jax.ShapeDtypeStruct(x.shape, F32),
        grid=(nb, t // tm),
        in_specs=[
            tok(D_MODEL),
            pl.BlockSpec((1, N_MOD, D_MODEL), lambda b, i: (b, 0, 0)),
            tok(Q_WIDTH), tok(FOURIER_WIDTH), tok(D_MODEL), tok(D_MODEL),
            _resident(w_ab.shape), _resident(w_fb.shape), _resident(w_o.shape),
        ],
        out_specs=tok(D_MODEL),
        compiler_params=_params("parallel", "parallel"),
        name="merge",
    )(x, mod, y_attn, y_four, sig_a, sig_f, w_ab, w_fb, w_o)


def _rope_tables(n_tokens):
    rows = n_tokens // GRID_W
    row_ids = jnp.repeat(jnp.arange(rows, dtype=F32), GRID_W)
    col_ids = jnp.tile(jnp.arange(GRID_W, dtype=F32), rows)
    inv_freq = ROPE_THETA ** (-jnp.arange(0, AXIS_ROPE_DIM, 2, dtype=F32) / AXIS_ROPE_DIM)
    ang = jnp.concatenate([row_ids[:, None] * inv_freq, col_ids[:, None] * inv_freq], axis=-1)
    cos, sin = jnp.cos(ang), jnp.sin(ang)
    return jnp.concatenate([cos, cos], axis=-1), jnp.concatenate([-sin, sin], axis=-1)


def kernel(x, c, ctx, c_ctx, w_ada, b_ada, norm_ffn1, w_ffn1_in, w_ffn1_out, norm_mix, w_in,
           q_norm, k_norm, w_attn_branch, w_fourier_branch, w_out, norm_ffn2, w_ffn2_in, w_ffn2_out):
    nb, n_lat, _ = x.shape
    n_ctx = ctx.shape[1]
    depth = w_ada.shape[0]
    assert depth == 1, "the context stream is only carried through the (single) last layer"
    cos2, sin2 = _rope_tables(n_lat)
    ctx_row = nb
    pad_rows = -(nb + 1) % 8
    c_all = jnp.concatenate([c, c_ctx[None, :], jnp.zeros((pad_rows, D_MODEL), F32)], axis=0)

    i = 0
    mod = _adaln(c_all, w_ada[i], b_ada[i][None, :]).reshape(c_all.shape[0], N_MOD, D_MODEL)
    own_row, shared_row = (lambda b: b), (lambda b: ctx_row)
    x, ctx_flat = _ffn([x, ctx.reshape(1, nb * n_ctx, D_MODEL)], mod, [own_row, shared_row], 0,
                       norm_ffn1[i][None, :], w_ffn1_in[i], w_ffn1_out[i])

    q, k, vx, f, sig_a, sig_f, k_c, vx_c = _inproj(
        x, ctx_flat, mod, [own_row, shared_row], norm_mix[i][None, :], w_in[i],
        q_norm[i][None, :], k_norm[i][None, :], cos2, sin2)
    y_attn = _attention(q, k, vx, k_c.reshape(nb, n_ctx, KV_WIDTH), vx_c.reshape(nb, n_ctx, VX_WIDTH))
    y_four = _fourier(f)
    x = _merge(x, mod, y_attn, y_four, sig_a, sig_f, w_attn_branch[i], w_fourier_branch[i], w_out[i])

    return _ffn([x], mod, [own_row], 6, norm_ffn2[i][None, :], w_ffn2_in[i], w_ffn2_out[i])[0]
```

```python
import functools

import numpy as np
import jax
import jax.numpy as jnp
from jax import lax
from jax.experimental import pallas as pl
from jax.experimental.pallas import tpu as pltpu

D_MODEL = 1024
GRID_W = 64
HEAD_DIM = 128
N_Q_HEADS = D_MODEL // HEAD_DIM
N_KV_HEADS = N_Q_HEADS // 4
GQA_GROUP = N_Q_HEADS // N_KV_HEADS
Q_WIDTH = N_Q_HEADS * HEAD_DIM
KV_WIDTH = N_KV_HEADS * HEAD_DIM
FOURIER_GROUP = 128
N_FOURIER_GROUPS = 4
FOURIER_WIDTH = N_FOURIER_GROUPS * FOURIER_GROUP
D_FF = 2816
AXIS_ROPE_DIM = HEAD_DIM // 2
ROPE_THETA = 10000.0
EPS = 1e-6
N_MOD = 9
ATTN_SCALE = HEAD_DIM ** -0.5
LOG2_E = 1.4426950408889634

K_OFF = Q_WIDTH
V_OFF = K_OFF + KV_WIDTH
F_OFF = V_OFF + KV_WIDTH
GA_OFF = F_OFF + FOURIER_WIDTH
GF_OFF = GA_OFF + D_MODEL

V7X_MXU_DIM = 256
V7X_LANES = 128
V7X_SUBLANES = 8
VX_WIDTH = N_KV_HEADS * V7X_MXU_DIM
V7X_VMEM_BYTES = 64 * 2**20
VMEM_LIMIT = V7X_VMEM_BYTES - 8 * 2**20

TOKEN_TILE = 512
WIDE_TOKEN_TILE = 1024
FF_CHUNK = V7X_MXU_DIM
ATTN_Q_TILE = 512
ADALN_TILE = 2304

F32 = jnp.float32
BF16 = jnp.bfloat16


def _params(*sem):
    return pltpu.CompilerParams(dimension_semantics=sem, vmem_limit_bytes=VMEM_LIMIT)


def _resident(shape):
    return pl.BlockSpec(shape, lambda *_: (0,) * len(shape), pipeline_mode=pl.Buffered(1))


def _rms(x, gain):
    return x * lax.rsqrt(jnp.mean(x * x, axis=-1, keepdims=True) + EPS) * gain


def _dot(a, b):
    return jnp.dot(a, b.astype(BF16), preferred_element_type=F32)


def _sigmoid(x):
    return 0.5 * jnp.tanh(0.5 * x) + 0.5


def _adaln_kernel(c_ref, w_ref, b_ref, o_ref):
    c = c_ref[...]
    a = (c * _sigmoid(c)).astype(BF16)
    o_ref[...] = _dot(a, w_ref[...]) + b_ref[...]


def _adaln(c_all, w, b):
    rows, n_out = c_all.shape[0], w.shape[1]
    return pl.pallas_call(
        _adaln_kernel,
        out_shape=jax.ShapeDtypeStruct((rows, n_out), F32),
        grid=(n_out // ADALN_TILE,),
        in_specs=[
            pl.BlockSpec((rows, D_MODEL), lambda j: (0, 0)),
            pl.BlockSpec((D_MODEL, ADALN_TILE), lambda j: (0, j)),
            pl.BlockSpec((1, ADALN_TILE), lambda j: (0, j)),
        ],
        out_specs=pl.BlockSpec((rows, ADALN_TILE), lambda j: (0, j)),
        compiler_params=_params("arbitrary"),
        name="adaln",
    )(c_all, w, b)


class _TokenStreams:
    def __init__(self, arrays, tile):
        self.tile = tile
        self.per_batch = [a.shape[1] // tile for a in arrays]
        counts = [a.shape[0] * p for a, p in zip(arrays, self.per_batch)]
        self.starts = [sum(counts[:s]) for s in range(len(arrays) + 1)]
        self.steps = self.starts[-1]

    def batch_and_tile(self, s, t):
        local = jnp.clip(t - self.starts[s], 0, self.starts[s + 1] - self.starts[s] - 1)
        return local // self.per_batch[s], local % self.per_batch[s]

    def token_spec(self, s, width):
        def index(t):
            b, i = self.batch_and_tile(s, t)
            return b, i, 0
        return pl.BlockSpec((1, self.tile, width), index)

    def mod_spec(self, mod_rows):
        def index(t):
            row = mod_rows[0](self.batch_and_tile(0, t)[0])
            for s in range(1, len(mod_rows)):
                row = jnp.where(t >= self.starts[s], mod_rows[s](self.batch_and_tile(s, t)[0]), row)
            return row, 0, 0
        return pl.BlockSpec((1, N_MOD, D_MODEL), index)

    def active(self, s):
        t = pl.program_id(0)
        return jnp.logical_and(t >= self.starts[s], t < self.starts[s + 1])


def _ffn_kernel(*refs, streams, mod_base):
    n = len(streams.per_batch)
    x_refs, (mod_ref, gain_ref, win_ref, wout_ref), o_refs, act_ref = (
        refs[:n], refs[n:n + 4], refs[n + 4:2 * n + 4], refs[2 * n + 4])

    def half_step(x_ref, o_ref):
        x = x_ref[0]
        shift = mod_ref[0, mod_base:mod_base + 1, :]
        scale = mod_ref[0, mod_base + 1:mod_base + 2, :]
        gate = mod_ref[0, mod_base + 2:mod_base + 3, :]
        h = (_rms(x, gain_ref[...]) * (1.0 + scale) + shift).astype(BF16)
        for j in range(D_FF // FF_CHUNK):
            lo = j * FF_CHUNK
            g = _dot(h, win_ref[:, lo:lo + FF_CHUNK])
            u = _dot(h, win_ref[:, D_FF + lo:D_FF + lo + FF_CHUNK])
            act_ref[:, lo:lo + FF_CHUNK] = (g * _sigmoid(g) * u).astype(BF16)
        y = _dot(act_ref[...], wout_ref[...])
        o_ref[0] = x + 0.5 * gate * y

    if n == 1:
        half_step(x_refs[0], o_refs[0])
    else:
        for s in range(n):
            pl.when(streams.active(s))(functools.partial(half_step, x_refs[s], o_refs[s]))


def _ffn(xs, mod, mod_rows, mod_base, gain, w_in, w_out):
    tile = TOKEN_TILE
    streams = _TokenStreams(xs, tile)
    tok = [streams.token_spec(s, D_MODEL) for s in range(len(xs))]
    return pl.pallas_call(
        functools.partial(_ffn_kernel, streams=streams, mod_base=mod_base),
        out_shape=[jax.ShapeDtypeStruct(x.shape, F32) for x in xs],
        grid=(streams.steps,),
        in_specs=tok + [
            streams.mod_spec(mod_rows),
            _resident((1, D_MODEL)),
            _resident((D_MODEL, 2 * D_FF)),
            _resident((D_FF, D_MODEL)),
        ],
        out_specs=tok,
        scratch_shapes=[pltpu.VMEM((tile, D_FF), BF16)],
        compiler_params=_params("arbitrary"),
        name="ffn",
    )(*xs, mod, gain, w_in, w_out)


def _head_norm(t, gain):
    return t * lax.rsqrt(jnp.mean(t * t, axis=-1, keepdims=True) + EPS) * gain


def _rope(t, cos2, sin2):
    return t * cos2 + pltpu.roll(t, AXIS_ROPE_DIM, 1) * sin2


def _store_widened_values(v_ref, v):
    ones = jnp.ones((v.shape[0], V7X_MXU_DIM - HEAD_DIM), BF16)
    for hd in range(N_KV_HEADS):
        lo = hd * V7X_MXU_DIM
        v_ref[0, :, lo:lo + HEAD_DIM] = v[:, hd * HEAD_DIM:(hd + 1) * HEAD_DIM].astype(BF16)
        v_ref[0, :, lo + HEAD_DIM:lo + V7X_MXU_DIM] = ones


def _inproj_kernel(x_ref, ctx_ref, mod_ref, gain_ref, w_ref, qn_ref, kn_ref, cos_ref, sin_ref,
                   q_ref, k_ref, v_ref, f_ref, ga_ref, gf_ref, kc_ref, vc_ref, *, streams):
    kn = kn_ref[...]
    heads_per_dot = V7X_MXU_DIM // HEAD_DIM

    def normed_input(tok_ref):
        shift = mod_ref[0, 3:4, :]
        scale = mod_ref[0, 4:5, :]
        return (_rms(tok_ref[0], gain_ref[...]) * (1.0 + scale) + shift).astype(BF16)

    @pl.when(streams.active(0))
    def _latent_tokens():
        h = normed_input(x_ref)

        def wide_dot(col):
            return _dot(h, w_ref[:, col:col + V7X_MXU_DIM])

        cos2 = cos_ref[...]
        sin2 = sin_ref[...]
        qn = qn_ref[...] * (ATTN_SCALE * LOG2_E)
        for c in range(Q_WIDTH // V7X_MXU_DIM):
            t = wide_dot(c * V7X_MXU_DIM)
            for j in range(heads_per_dot):
                tj = t[:, j * HEAD_DIM:(j + 1) * HEAD_DIM]
                q_ref[0, c * heads_per_dot + j] = _rope(_head_norm(tj, qn), cos2, sin2).astype(BF16)
        for c in range(KV_WIDTH // V7X_MXU_DIM):
            t = wide_dot(K_OFF + c * V7X_MXU_DIM)
            for j in range(heads_per_dot):
                lo = c * V7X_MXU_DIM + j * HEAD_DIM
                tj = t[:, j * HEAD_DIM:(j + 1) * HEAD_DIM]
                k_ref[0, :, lo:lo + HEAD_DIM] = _rope(_head_norm(tj, kn), cos2, sin2).astype(BF16)
        _store_widened_values(v_ref, _dot(h, w_ref[:, V_OFF:V_OFF + KV_WIDTH]))
        for c in range(FOURIER_WIDTH // V7X_MXU_DIM):
            lo = c * V7X_MXU_DIM
            f_ref[0, :, lo:lo + V7X_MXU_DIM] = wide_dot(F_OFF + lo).astype(BF16)
        for c in range(D_MODEL // V7X_MXU_DIM):
            lo = c * V7X_MXU_DIM
            ga_ref[0, :, lo:lo + V7X_MXU_DIM] = _sigmoid(wide_dot(GA_OFF + lo)).astype(BF16)
            gf_ref[0, :, lo:lo + V7X_MXU_DIM] = _sigmoid(wide_dot(GF_OFF + lo)).astype(BF16)

    @pl.when(streams.active(1))
    def _context_tokens():
        h = normed_input(ctx_ref)
        t = _dot(h, w_ref[:, K_OFF:K_OFF + KV_WIDTH])
        for hd in range(N_KV_HEADS):
            lo = hd * HEAD_DIM
            kc_ref[0, :, lo:lo + HEAD_DIM] = _head_norm(t[:, lo:lo + HEAD_DIM], kn).astype(BF16)
        _store_widened_values(vc_ref, _dot(h, w_ref[:, V_OFF:V_OFF + KV_WIDTH]))


def _inproj(x, ctx, mod, mod_rows, gain, w_in, q_norm, k_norm, cos2, sin2):
    nb, t, _ = x.shape
    streams = _TokenStreams([x, ctx], TOKEN_TILE)
    tm = streams.tile
    tok = lambda w: streams.token_spec(0, w)
    ctx_tok = lambda w: streams.token_spec(1, w)
    shp = lambda w: jax.ShapeDtypeStruct((nb, t, w), BF16)
    ctx_shp = lambda w: jax.ShapeDtypeStruct(ctx.shape[:2] + (w,), BF16)

    def q_index(step):
        b, i = streams.batch_and_tile(0, step)
        return b, 0, i, 0

    def rope_index(step):
        return streams.batch_and_tile(0, step)[1], 0

    return pl.pallas_call(
        functools.partial(_inproj_kernel, streams=streams),
        out_shape=[jax.ShapeDtypeStruct((nb, N_Q_HEADS, t, HEAD_DIM), BF16), shp(KV_WIDTH),
                   shp(VX_WIDTH), shp(FOURIER_WIDTH), shp(D_MODEL), shp(D_MODEL),
                   ctx_shp(KV_WIDTH), ctx_shp(VX_WIDTH)],
        grid=(streams.steps,),
        in_specs=[
            tok(D_MODEL),
            ctx_tok(D_MODEL),
            streams.mod_spec(mod_rows),
            _resident((1, D_MODEL)),
            _resident(w_in.shape),
            _resident((1, HEAD_DIM)),
            _resident((1, HEAD_DIM)),
            pl.BlockSpec((tm, HEAD_DIM), rope_index),
            pl.BlockSpec((tm, HEAD_DIM), rope_index),
        ],
        out_specs=[pl.BlockSpec((1, N_Q_HEADS, tm, HEAD_DIM), q_index),
                   tok(KV_WIDTH), tok(VX_WIDTH), tok(FOURIER_WIDTH), tok(D_MODEL), tok(D_MODEL),
                   ctx_tok(KV_WIDTH), ctx_tok(VX_WIDTH)],
        compiler_params=_params("arbitrary"),
        name="inproj",
    )(x, ctx, mod, gain, w_in, q_norm, k_norm, cos2, sin2)


_NT = (((1,), (1,)), ((), ()))


ATTN_SCRATCH_PER_HEAD = 5


def _attn_kernel(q_ref, kc_ref, k_ref, vcx_ref, vx_ref, o_ref, *scratch):
    heads = [scratch[ATTN_SCRATCH_PER_HEAD * g:ATTN_SCRATCH_PER_HEAD * (g + 1)] for g in range(GQA_GROUP)]

    @pl.when(pl.program_id(0) == 0)
    def _no_previous_item():
        for pc_ref, pl_ref, sc_ref, sl_ref, m_ref in heads:
            pc_ref[...] = jnp.ones(pc_ref.shape, BF16)
            pl_ref[...] = jnp.ones(pl_ref.shape, BF16)
            sc_ref[...] = jnp.zeros(sc_ref.shape, F32)
            sl_ref[...] = jnp.zeros(sl_ref.shape, F32)
            m_ref[...] = jnp.zeros(m_ref.shape, F32)

    def weights_from_scores(pc_ref, pl_ref, sc_ref, sl_ref, m_ref):
        m = m_ref[...]
        for s_ref, p_ref in ((sc_ref, pc_ref), (sl_ref, pl_ref)):
            for lo in range(0, s_ref.shape[1], V7X_MXU_DIM):
                p_ref[:, lo:lo + V7X_MXU_DIM] = jnp.exp2(s_ref[:, lo:lo + V7X_MXU_DIM] - m).astype(BF16)

    weights_from_scores(*heads[-1])

    for g, (pc_ref, pl_ref, sc_ref, sl_ref, m_ref) in enumerate(heads):
        oe = _dot(pc_ref[...], vcx_ref[0]) + _dot(pl_ref[...], vx_ref[0])
        o_ref[0, :, g * HEAD_DIM:(g + 1) * HEAD_DIM] = (oe[:, :HEAD_DIM] / oe[:, HEAD_DIM:]).astype(BF16)

        q = q_ref[0, g]
        s_c = lax.dot_general(q, kc_ref[0], _NT, preferred_element_type=F32)
        s_l = lax.dot_general(q, k_ref[0], _NT, preferred_element_type=F32)
        m_ref[...] = jnp.maximum(jnp.max(s_c, axis=-1, keepdims=True), jnp.max(s_l, axis=-1, keepdims=True))
        sc_ref[...] = s_c
        sl_ref[...] = s_l

    for head in heads[:-1]:
        weights_from_scores(*head)


def _attention(q, k, vx, k_c, vx_c):
    nb, _, n, _ = q.shape
    n_ctx = k_c.shape[1]
    tq = ATTN_Q_TILE
    tiles = n // tq
    items = nb * N_KV_HEADS * tiles

    def item(t):
        return t // (N_KV_HEADS * tiles), (t // tiles) % N_KV_HEADS, t % tiles

    def scored(t):
        return item(jnp.minimum(t, items - 1))

    def weighted(t):
        return item(jnp.maximum(t - 1, 0))

    def q_map(t):
        b, h, i = scored(t)
        return b, h, i, 0

    def k_map(t):
        b, h, _ = scored(t)
        return b, 0, h

    def v_map(t):
        b, h, _ = weighted(t)
        return b, 0, h

    def o_map(t):
        b, h, i = weighted(t)
        return b, i, h

    return pl.pallas_call(
        _attn_kernel,
        out_shape=jax.ShapeDtypeStruct((nb, n, Q_WIDTH), BF16),
        grid=(items + 1,),
        in_specs=[
            pl.BlockSpec((1, GQA_GROUP, tq, HEAD_DIM), q_map),
            pl.BlockSpec((1, n_ctx, HEAD_DIM), k_map),
            pl.BlockSpec((1, n, HEAD_DIM), k_map),
            pl.BlockSpec((1, n_ctx, V7X_MXU_DIM), v_map),
            pl.BlockSpec((1, n, V7X_MXU_DIM), v_map),
        ],
        out_specs=pl.BlockSpec((1, tq, GQA_GROUP * HEAD_DIM), o_map),
        scratch_shapes=GQA_GROUP * [
            pltpu.VMEM((tq, n_ctx), BF16), pltpu.VMEM((tq, n), BF16),
            pltpu.VMEM((tq, n_ctx), F32), pltpu.VMEM((tq, n), F32), pltpu.VMEM((tq, 1), F32)],
        compiler_params=_params("arbitrary"),
        name="attn",
    )(q, k_c, k, vx_c, vx)


DFT_RADIX = 8


def _dft_tables(n):
    r = n // DFT_RADIX
    k = np.arange(DFT_RADIX)[:, None, None] + DFT_RADIX * np.arange(r)[None, :, None]
    ang = 2.0 * np.pi * ((k * np.arange(r)[None, None, :]) % n) / n
    cos, sin = np.cos(ang), np.sin(ang)
    pos = np.concatenate([np.concatenate([cos, sin], axis=2),
                          np.concatenate([-sin, cos], axis=2)], axis=1)
    c = FOURIER_GROUP
    scale = 1.0 / np.sqrt(float(n * c))
    ang_c = 2.0 * np.pi * (np.outer(np.arange(c), np.arange(c)) % c) / c
    eye = np.eye(V7X_MXU_DIM // c)
    chan = np.concatenate([np.kron(eye, np.cos(ang_c) * scale),
                           np.kron(eye, np.sin(ang_c) * scale)], axis=0)
    return jnp.asarray(pos, F32).astype(BF16), jnp.asarray(chan, F32).astype(BF16)


def _fold_radix8(x):
    c = 0.5 ** 0.5
    e02p, e02m, e13p, e13m = x[0] + x[4], x[0] - x[4], x[2] + x[6], x[2] - x[6]
    o02p, o02m, o13p, o13m = x[1] + x[5], x[1] - x[5], x[3] + x[7], x[3] - x[7]
    e0, e2, o0, o2 = e02p + e13p, e02p - e13p, o02p + o13p, o02p - o13p
    p, q = c * (o02m - o13m), c * (o02m + o13m)
    re = [e0 + o0, e02m + p, e2, e02m - p, e0 - o0, e02m - p, e2, e02m + p]
    im1, im3 = -e13m - q, e13m - q
    im = [None, im1, -o2, im3, None, -im3, o2, -im1]
    return re, im


def _fourier_kernel(f_ref, pos_ref, chan_ref, o_ref, t_ref, zr_ref, zi_ref, y_ref):
    r = f_ref.shape[1] // DFT_RADIX
    w = FOURIER_WIDTH
    pair = V7X_MXU_DIM
    slab = y_ref.shape[2]
    for lo in range(0, w, pair):
        for sub in range(lo, lo + pair, FOURIER_GROUP):
            lanes = slice(sub, sub + FOURIER_GROUP)
            x = [f_ref[0, n1 * r:(n1 + 1) * r, lanes].astype(F32) for n1 in range(DFT_RADIX)]
            re, im = _fold_radix8(x)
            for k1 in range(DFT_RADIX):
                t_ref[k1, 0:r, lanes] = re[k1].astype(BF16)
                t_ref[k1, r:2 * r, lanes] = (
                    jnp.zeros((r, FOURIER_GROUP), BF16) if im[k1] is None else im[k1].astype(BF16))
        h = lo // pair
        for k1 in range(DFT_RADIX):
            z = _dot(pos_ref[k1], t_ref[k1, :, lo:lo + pair])
            zr_ref[h, k1 * r:(k1 + 1) * r, :] = z[0:r].astype(BF16)
            zi_ref[h, k1 * r:(k1 + 1) * r, :] = z[r:2 * r].astype(BF16)
        y = _dot(zr_ref[h], chan_ref[0:pair, :]) + _dot(zi_ref[h], chan_ref[pair:2 * pair, :])
        for k1 in range(DFT_RADIX):
            for j in range(pair // slab):
                y_ref[(lo + j * slab) // slab, pl.ds(k1, r, stride=DFT_RADIX), :] = (
                    y[k1 * r:(k1 + 1) * r, j * slab:(j + 1) * slab])
    for j in range(w // slab):
        o_ref[0, :, j * slab:(j + 1) * slab] = y_ref[j].astype(BF16)


def _fourier(f):
    nb, n, w = f.shape
    pos, chan = _dft_tables(n)
    r = n // DFT_RADIX
    blk = pl.BlockSpec((1, n, w), lambda b: (b, 0, 0))
    lanes = V7X_LANES
    return pl.pallas_call(
        _fourier_kernel,
        out_shape=jax.ShapeDtypeStruct(f.shape, BF16),
        grid=(nb,),
        in_specs=[blk, _resident(pos.shape), _resident(chan.shape)],
        out_specs=blk,
        scratch_shapes=[pltpu.VMEM((DFT_RADIX, 2 * r, w), BF16),
                        pltpu.VMEM((w // V7X_MXU_DIM, n, V7X_MXU_DIM), BF16),
                        pltpu.VMEM((w // V7X_MXU_DIM, n, V7X_MXU_DIM), BF16),
                        pltpu.VMEM((w // lanes, n, lanes), F32)],
        compiler_params=_params("parallel"),
        name="fourier",
    )(f, pos, chan)


MERGE_INPUT_BUFFERS = 3


def _merge_kernel(x_hbm, mod_hbm, ya_hbm, yf_hbm, ga_hbm, gf_hbm, wab_ref, wfb_ref, wo_ref, o_hbm,
                  *, grid, tm):
    def step(x_ref, mod_ref, ya_ref, yf_ref, ga_ref, gf_ref, o_ref):
        gate = mod_ref[0, 5:6, :]
        merged = (ga_ref[0].astype(F32) * _dot(ya_ref[0], wab_ref[...])
                  + gf_ref[0].astype(F32) * _dot(yf_ref[0], wfb_ref[...]))
        mix = _dot(merged.astype(BF16), wo_ref[...])
        o_ref[0] = x_ref[0] + gate * mix

    def tok(w, buffers=None):
        mode = {} if buffers is None else {"pipeline_mode": pl.Buffered(buffers)}
        return pl.BlockSpec((1, tm, w), lambda b, i: (b, i, 0), **mode)

    deep = MERGE_INPUT_BUFFERS
    pltpu.emit_pipeline(
        step, grid=grid,
        in_specs=[tok(D_MODEL, deep), pl.BlockSpec((1, N_MOD, D_MODEL), lambda b, i: (b, 0, 0)),
                  tok(Q_WIDTH, deep), tok(FOURIER_WIDTH, deep), tok(D_MODEL, deep), tok(D_MODEL, deep)],
        out_specs=[tok(D_MODEL)],
    )(x_hbm, mod_hbm, ya_hbm, yf_hbm, ga_hbm, gf_hbm, o_hbm)


def _merge(x, mod, y_attn, y_four, sig_a, sig_f, w_ab, w_fb, w_o):
    nb, t, _ = x.shape
    tm = TOKEN_TILE
    hbm = pl.BlockSpec(memory_space=pl.ANY)
    vmem = pl.BlockSpec(memory_space=pltpu.VMEM)
    return pl.pallas_call(
        functools.partial(_merge_kernel, grid=(nb, t // tm), tm=tm),
        out_shape=jax.ShapeDtypeStruct(x.shape, F32),
        in_specs=[hbm] * 6 + [vmem] * 3,
        out_specs=hbm,
        compiler_params=pltpu.CompilerParams(vmem_limit_bytes=VMEM_LIMIT),
        name="merge",
    )(x, mod, y_attn, y_four, sig_a, sig_f, w_ab, w_fb, w_o)


def _rope_tables(n_tokens):
    f32 = np.float32
    rows = n_tokens // GRID_W
    row_ids = np.repeat(np.arange(rows, dtype=f32), GRID_W)
    col_ids = np.tile(np.arange(GRID_W, dtype=f32), rows)
    inv_freq = f32(ROPE_THETA) ** (-np.arange(0, AXIS_ROPE_DIM, 2, dtype=f32) / f32(AXIS_ROPE_DIM))
    ang = np.concatenate([row_ids[:, None] * inv_freq, col_ids[:, None] * inv_freq], axis=-1)
    cos, sin = np.cos(ang), np.sin(ang)
    assert ang.dtype == f32 and cos.dtype == f32
    return (jnp.asarray(np.concatenate([cos, cos], axis=-1)),
            jnp.asarray(np.concatenate([-sin, sin], axis=-1)))


def kernel(x, c, ctx, c_ctx, w_ada, b_ada, norm_ffn1, w_ffn1_in, w_ffn1_out, norm_mix, w_in,
           q_norm, k_norm, w_attn_branch, w_fourier_branch, w_out, norm_ffn2, w_ffn2_in, w_ffn2_out):
    nb, n_lat, _ = x.shape
    n_ctx = ctx.shape[1]
    depth = w_ada.shape[0]
    assert depth == 1, "the context stream is only carried through the (single) last layer"
    cos2, sin2 = _rope_tables(n_lat)
    ctx_row = nb
    pad_rows = -(nb + 1) % V7X_SUBLANES
    c_all = jnp.concatenate([c, c_ctx[None, :], jnp.zeros((pad_rows, D_MODEL), F32)], axis=0)

    i = 0
    mod = _adaln(c_all, w_ada[i], b_ada[i][None, :]).reshape(c_all.shape[0], N_MOD, D_MODEL)
    own_row, shared_row = (lambda b: b), (lambda b: ctx_row)
    x, ctx_flat = _ffn([x, ctx.reshape(1, nb * n_ctx, D_MODEL)], mod, [own_row, shared_row], 0,
                       norm_ffn1[i][None, :], w_ffn1_in[i], w_ffn1_out[i])

    q, k, vx, f, sig_a, sig_f, k_c, vx_c = _inproj(
        x, ctx_flat, mod, [own_row, shared_row], norm_mix[i][None, :], w_in[i],
        q_norm[i][None, :], k_norm[i][None, :], cos2, sin2)
    y_attn = _attention(q, k, vx, k_c.reshape(nb, n_ctx, KV_WIDTH), vx_c.reshape(nb, n_ctx, VX_WIDTH))
    y_four = _fourier(f)
    x = _merge(x, mod, y_attn, y_four, sig_a, sig_f, w_attn_branch[i], w_fourier_branch[i], w_out[i])

    return _ffn([x], mod, [own_row], 6, norm_ffn2[i][None, :], w_ffn2_in[i], w_ffn2_out[i])[0]
```

```python
import functools

import numpy as np
import jax
import jax.numpy as jnp
from jax import lax
from jax.experimental import pallas as pl
from jax.experimental.pallas import tpu as pltpu

D_MODEL = 1024
GRID_W = 64
HEAD_DIM = 128
N_Q_HEADS = D_MODEL // HEAD_DIM
N_KV_HEADS = N_Q_HEADS // 4
GQA_GROUP = N_Q_HEADS // N_KV_HEADS
Q_WIDTH = N_Q_HEADS * HEAD_DIM
KV_WIDTH = N_KV_HEADS * HEAD_DIM
FOURIER_GROUP = 128
N_FOURIER_GROUPS = 4
FOURIER_WIDTH = N_FOURIER_GROUPS * FOURIER_GROUP
D_FF = 2816
AXIS_ROPE_DIM = HEAD_DIM // 2
ROPE_THETA = 10000.0
EPS = 1e-6
N_MOD = 9
ATTN_SCALE = HEAD_DIM ** -0.5
LOG2_E = 1.4426950408889634

K_OFF = Q_WIDTH
V_OFF = K_OFF + KV_WIDTH
F_OFF = V_OFF + KV_WIDTH
GA_OFF = F_OFF + FOURIER_WIDTH
GF_OFF = GA_OFF + D_MODEL

V7X_MXU_DIM = 256
V7X_LANES = 128
V7X_SUBLANES = 8
VX_WIDTH = N_KV_HEADS * V7X_MXU_DIM
V7X_VMEM_BYTES = 64 * 2**20
VMEM_LIMIT = V7X_VMEM_BYTES - 8 * 2**20

TOKEN_TILE = 512
WIDE_TOKEN_TILE = 1024
FF_CHUNK = V7X_MXU_DIM
ATTN_Q_TILE = 512
ADALN_TILE = 2304

F32 = jnp.float32
BF16 = jnp.bfloat16


def _params(*sem):
    return pltpu.CompilerParams(dimension_semantics=sem, vmem_limit_bytes=VMEM_LIMIT)


def _resident(shape):
    return pl.BlockSpec(shape, lambda *_: (0,) * len(shape), pipeline_mode=pl.Buffered(1))


def _rms(x, gain):
    return x * lax.rsqrt(jnp.mean(x * x, axis=-1, keepdims=True) + EPS) * gain


def _dot(a, b):
    return jnp.dot(a, b.astype(BF16), preferred_element_type=F32)


def _sigmoid(x):
    return 0.5 * jnp.tanh(0.5 * x) + 0.5


def _adaln_kernel(c_ref, w_ref, b_ref, o_ref):
    c = c_ref[...]
    a = (c * _sigmoid(c)).astype(BF16)
    o_ref[...] = _dot(a, w_ref[...]) + b_ref[...]


def _adaln(c_all, w, b):
    rows, n_out = c_all.shape[0], w.shape[1]
    return pl.pallas_call(
        _adaln_kernel,
        out_shape=jax.ShapeDtypeStruct((rows, n_out), F32),
        grid=(n_out // ADALN_TILE,),
        in_specs=[
            pl.BlockSpec((rows, D_MODEL), lambda j: (0, 0)),
            pl.BlockSpec((D_MODEL, ADALN_TILE), lambda j: (0, j)),
            pl.BlockSpec((1, ADALN_TILE), lambda j: (0, j)),
        ],
        out_specs=pl.BlockSpec((rows, ADALN_TILE), lambda j: (0, j)),
        compiler_params=_params("arbitrary"),
        name="adaln",
    )(c_all, w, b)


class _TokenStreams:
    def __init__(self, arrays, tile):
        self.tile = tile
        self.per_batch = [a.shape[1] // tile for a in arrays]
        counts = [a.shape[0] * p for a, p in zip(arrays, self.per_batch)]
        self.starts = [sum(counts[:s]) for s in range(len(arrays) + 1)]
        self.steps = self.starts[-1]

    def batch_and_tile(self, s, t):
        local = jnp.clip(t - self.starts[s], 0, self.starts[s + 1] - self.starts[s] - 1)
        return local // self.per_batch[s], local % self.per_batch[s]

    def token_spec(self, s, width):
        def index(t):
            b, i = self.batch_and_tile(s, t)
            return b, i, 0
        return pl.BlockSpec((1, self.tile, width), index)

    def mod_spec(self, mod_rows):
        def index(t):
            row = mod_rows[0](self.batch_and_tile(0, t)[0])
            for s in range(1, len(mod_rows)):
                row = jnp.where(t >= self.starts[s], mod_rows[s](self.batch_and_tile(s, t)[0]), row)
            return row, 0, 0
        return pl.BlockSpec((1, N_MOD, D_MODEL), index)

    def active(self, s):
        t = pl.program_id(0)
        return jnp.logical_and(t >= self.starts[s], t < self.starts[s + 1])


def _ffn_kernel(*refs, streams, mod_base):
    n = len(streams.per_batch)
    x_refs, (mod_ref, gain_ref, win_ref, wout_ref), o_refs, act_ref = (
        refs[:n], refs[n:n + 4], refs[n + 4:2 * n + 4], refs[2 * n + 4])

    def half_step(x_ref, o_ref):
        x = x_ref[0]
        shift = mod_ref[0, mod_base:mod_base + 1, :]
        scale = mod_ref[0, mod_base + 1:mod_base + 2, :]
        gate = mod_ref[0, mod_base + 2:mod_base + 3, :]
        h = (_rms(x, gain_ref[...]) * (1.0 + scale) + shift).astype(BF16)
        for j in range(D_FF // FF_CHUNK):
            lo = j * FF_CHUNK
            g = _dot(h, win_ref[:, lo:lo + FF_CHUNK])
            u = _dot(h, win_ref[:, D_FF + lo:D_FF + lo + FF_CHUNK])
            act_ref[:, lo:lo + FF_CHUNK] = (g * _sigmoid(g) * u).astype(BF16)
        y = _dot(act_ref[...], wout_ref[...])
        o_ref[0] = x + 0.5 * gate * y

    if n == 1:
        half_step(x_refs[0], o_refs[0])
    else:
        for s in range(n):
            pl.when(streams.active(s))(functools.partial(half_step, x_refs[s], o_refs[s]))


def _ffn(xs, mod, mod_rows, mod_base, gain, w_in, w_out):
    tile = TOKEN_TILE
    streams = _TokenStreams(xs, tile)
    tok = [streams.token_spec(s, D_MODEL) for s in range(len(xs))]
    return pl.pallas_call(
        functools.partial(_ffn_kernel, streams=streams, mod_base=mod_base),
        out_shape=[jax.ShapeDtypeStruct(x.shape, F32) for x in xs],
        grid=(streams.steps,),
        in_specs=tok + [
            streams.mod_spec(mod_rows),
            _resident((1, D_MODEL)),
            _resident((D_MODEL, 2 * D_FF)),
            _resident((D_FF, D_MODEL)),
        ],
        out_specs=tok,
        scratch_shapes=[pltpu.VMEM((tile, D_FF), BF16)],
        compiler_params=_params("arbitrary"),
        name="ffn",
    )(*xs, mod, gain, w_in, w_out)


def _head_norm(t, gain):
    return t * lax.rsqrt(jnp.mean(t * t, axis=-1, keepdims=True) + EPS) * gain


def _rope(t, cos2, sin2):
    return t * cos2 + pltpu.roll(t, AXIS_ROPE_DIM, 1) * sin2


def _store_widened_values(v_ref, v):
    ones = jnp.ones((v.shape[0], V7X_MXU_DIM - HEAD_DIM), BF16)
    for hd in range(N_KV_HEADS):
        lo = hd * V7X_MXU_DIM
        v_ref[0, :, lo:lo + HEAD_DIM] = v[:, hd * HEAD_DIM:(hd + 1) * HEAD_DIM].astype(BF16)
        v_ref[0, :, lo + HEAD_DIM:lo + V7X_MXU_DIM] = ones


def _inproj_kernel(x_ref, ctx_ref, mod_ref, gain_ref, w_ref, qn_ref, kn_ref, cos_ref, sin_ref,
                   q_ref, k_ref, v_ref, f_ref, ga_ref, gf_ref, kc_ref, vc_ref, *, streams):
    kn = kn_ref[...]
    heads_per_dot = V7X_MXU_DIM // HEAD_DIM

    def normed_input(tok_ref):
        shift = mod_ref[0, 3:4, :]
        scale = mod_ref[0, 4:5, :]
        return (_rms(tok_ref[0], gain_ref[...]) * (1.0 + scale) + shift).astype(BF16)

    @pl.when(streams.active(0))
    def _latent_tokens():
        h = normed_input(x_ref)

        def wide_dot(col):
            return _dot(h, w_ref[:, col:col + V7X_MXU_DIM])

        cos2 = cos_ref[...]
        sin2 = sin_ref[...]
        qn = qn_ref[...] * (ATTN_SCALE * LOG2_E)
        for c in range(Q_WIDTH // V7X_MXU_DIM):
            t = wide_dot(c * V7X_MXU_DIM)
            for j in range(heads_per_dot):
                tj = t[:, j * HEAD_DIM:(j + 1) * HEAD_DIM]
                q_ref[0, c * heads_per_dot + j] = _rope(_head_norm(tj, qn), cos2, sin2).astype(BF16)
        for c in range(KV_WIDTH // V7X_MXU_DIM):
            t = wide_dot(K_OFF + c * V7X_MXU_DIM)
            for j in range(heads_per_dot):
                lo = c * V7X_MXU_DIM + j * HEAD_DIM
                tj = t[:, j * HEAD_DIM:(j + 1) * HEAD_DIM]
                k_ref[0, :, lo:lo + HEAD_DIM] = _rope(_head_norm(tj, kn), cos2, sin2).astype(BF16)
        _store_widened_values(v_ref, _dot(h, w_ref[:, V_OFF:V_OFF + KV_WIDTH]))
        for c in range(FOURIER_WIDTH // V7X_MXU_DIM):
            lo = c * V7X_MXU_DIM
            f_ref[0, :, lo:lo + V7X_MXU_DIM] = wide_dot(F_OFF + lo).astype(BF16)
        for c in range(D_MODEL // V7X_MXU_DIM):
            lo = c * V7X_MXU_DIM
            ga_ref[0, :, lo:lo + V7X_MXU_DIM] = _sigmoid(wide_dot(GA_OFF + lo)).astype(BF16)
            gf_ref[0, :, lo:lo + V7X_MXU_DIM] = _sigmoid(wide_dot(GF_OFF + lo)).astype(BF16)

    @pl.when(streams.active(1))
    def _context_tokens():
        h = normed_input(ctx_ref)
        t = _dot(h, w_ref[:, K_OFF:K_OFF + KV_WIDTH])
        for hd in range(N_KV_HEADS):
            lo = hd * HEAD_DIM
            kc_ref[0, :, lo:lo + HEAD_DIM] = _head_norm(t[:, lo:lo + HEAD_DIM], kn).astype(BF16)
        _store_widened_values(vc_ref, _dot(h, w_ref[:, V_OFF:V_OFF + KV_WIDTH]))


def _inproj(x, ctx, mod, mod_rows, gain, w_in, q_norm, k_norm, cos2, sin2):
    nb, t, _ = x.shape
    streams = _TokenStreams([x, ctx], TOKEN_TILE)
    tm = streams.tile
    tok = lambda w: streams.token_spec(0, w)
    ctx_tok = lambda w: streams.token_spec(1, w)
    shp = lambda w: jax.ShapeDtypeStruct((nb, t, w), BF16)
    ctx_shp = lambda w: jax.ShapeDtypeStruct(ctx.shape[:2] + (w,), BF16)

    def q_index(step):
        b, i = streams.batch_and_tile(0, step)
        return b, 0, i, 0

    def rope_index(step):
        return streams.batch_and_tile(0, step)[1], 0

    return pl.pallas_call(
        functools.partial(_inproj_kernel, streams=streams),
        out_shape=[jax.ShapeDtypeStruct((nb, N_Q_HEADS, t, HEAD_DIM), BF16), shp(KV_WIDTH),
                   shp(VX_WIDTH), shp(FOURIER_WIDTH), shp(D_MODEL), shp(D_MODEL),
                   ctx_shp(KV_WIDTH), ctx_shp(VX_WIDTH)],
        grid=(streams.steps,),
        in_specs=[
            tok(D_MODEL),
            ctx_tok(D_MODEL),
            streams.mod_spec(mod_rows),
            _resident((1, D_MODEL)),
            _resident(w_in.shape),
            _resident((1, HEAD_DIM)),
            _resident((1, HEAD_DIM)),
            pl.BlockSpec((tm, HEAD_DIM), rope_index),
            pl.BlockSpec((tm, HEAD_DIM), rope_index),
        ],
        out_specs=[pl.BlockSpec((1, N_Q_HEADS, tm, HEAD_DIM), q_index),
                   tok(KV_WIDTH), tok(VX_WIDTH), tok(FOURIER_WIDTH), tok(D_MODEL), tok(D_MODEL),
                   ctx_tok(KV_WIDTH), ctx_tok(VX_WIDTH)],
        compiler_params=_params("arbitrary"),
        name="inproj",
    )(x, ctx, mod, gain, w_in, q_norm, k_norm, cos2, sin2)


_NT = (((1,), (1,)), ((), ()))


ATTN_SCRATCH_PER_HEAD = 5


def _attn_kernel(q_ref, kc_ref, k_ref, vcx_ref, vx_ref, o_ref, *scratch):
    heads = [scratch[ATTN_SCRATCH_PER_HEAD * g:ATTN_SCRATCH_PER_HEAD * (g + 1)] for g in range(GQA_GROUP)]

    @pl.when(pl.program_id(0) == 0)
    def _no_previous_item():
        for pc_ref, pl_ref, sc_ref, sl_ref, m_ref in heads:
            pc_ref[...] = jnp.ones(pc_ref.shape, BF16)
            pl_ref[...] = jnp.ones(pl_ref.shape, BF16)
            sc_ref[...] = jnp.zeros(sc_ref.shape, F32)
            sl_ref[...] = jnp.zeros(sl_ref.shape, F32)
            m_ref[...] = jnp.zeros(m_ref.shape, F32)

    def weights_from_scores(pc_ref, pl_ref, sc_ref, sl_ref, m_ref):
        m = m_ref[...]
        for s_ref, p_ref in ((sc_ref, pc_ref), (sl_ref, pl_ref)):
            for lo in range(0, s_ref.shape[1], V7X_MXU_DIM):
                p_ref[:, lo:lo + V7X_MXU_DIM] = jnp.exp2(s_ref[:, lo:lo + V7X_MXU_DIM] - m).astype(BF16)

    weights_from_scores(*heads[-1])

    for g, (pc_ref, pl_ref, sc_ref, sl_ref, m_ref) in enumerate(heads):
        oe = _dot(pc_ref[...], vcx_ref[0]) + _dot(pl_ref[...], vx_ref[0])
        o_ref[0, :, g * HEAD_DIM:(g + 1) * HEAD_DIM] = (oe[:, :HEAD_DIM] / oe[:, HEAD_DIM:]).astype(BF16)

        q = q_ref[0, g]
        s_c = lax.dot_general(q, kc_ref[0], _NT, preferred_element_type=F32)
        s_l = lax.dot_general(q, k_ref[0], _NT, preferred_element_type=F32)
        m_ref[...] = jnp.maximum(jnp.max(s_c, axis=-1, keepdims=True), jnp.max(s_l, axis=-1, keepdims=True))
        sc_ref[...] = s_c
        sl_ref[...] = s_l

    for head in heads[:-1]:
        weights_from_scores(*head)


def _attention(q, k, vx, k_c, vx_c):
    nb, _, n, _ = q.shape
    n_ctx = k_c.shape[1]
    tq = ATTN_Q_TILE
    tiles = n // tq
    items = nb * N_KV_HEADS * tiles

    def item(t):
        return t // (N_KV_HEADS * tiles), (t // tiles) % N_KV_HEADS, t % tiles

    def scored(t):
        return item(jnp.minimum(t, items - 1))

    def weighted(t):
        return item(jnp.maximum(t - 1, 0))

    def q_map(t):
        b, h, i = scored(t)
        return b, h, i, 0

    def k_map(t):
        b, h, _ = scored(t)
        return b, 0, h

    def v_map(t):
        b, h, _ = weighted(t)
        return b, 0, h

    def o_map(t):
        b, h, i = weighted(t)
        return b, i, h

    return pl.pallas_call(
        _attn_kernel,
        out_shape=jax.ShapeDtypeStruct((nb, n, Q_WIDTH), BF16),
        grid=(items + 1,),
        in_specs=[
            pl.BlockSpec((1, GQA_GROUP, tq, HEAD_DIM), q_map),
            pl.BlockSpec((1, n_ctx, HEAD_DIM), k_map),
            pl.BlockSpec((1, n, HEAD_DIM), k_map),
            pl.BlockSpec((1, n_ctx, V7X_MXU_DIM), v_map),
            pl.BlockSpec((1, n, V7X_MXU_DIM), v_map),
        ],
        out_specs=pl.BlockSpec((1, tq, GQA_GROUP * HEAD_DIM), o_map),
        scratch_shapes=GQA_GROUP * [
            pltpu.VMEM((tq, n_ctx), BF16), pltpu.VMEM((tq, n), BF16),
            pltpu.VMEM((tq, n_ctx), F32), pltpu.VMEM((tq, n), F32), pltpu.VMEM((tq, 1), F32)],
        compiler_params=_params("arbitrary"),
        name="attn",
    )(q, k_c, k, vx_c, vx)


DFT_RADIX = 8


def _dft_tables(n):
    r = n // DFT_RADIX
    k = np.arange(DFT_RADIX)[:, None, None] + DFT_RADIX * np.arange(r)[None, :, None]
    ang = 2.0 * np.pi * ((k * np.arange(r)[None, None, :]) % n) / n
    cos, sin = np.cos(ang), np.sin(ang)
    pos = np.concatenate([np.concatenate([cos, sin], axis=2),
                          np.concatenate([-sin, cos], axis=2)], axis=1)
    c = FOURIER_GROUP
    scale = 1.0 / np.sqrt(float(n * c))
    ang_c = 2.0 * np.pi * (np.outer(np.arange(c), np.arange(c)) % c) / c
    eye = np.eye(V7X_MXU_DIM // c)
    chan = np.concatenate([np.kron(eye, np.cos(ang_c) * scale),
                           np.kron(eye, np.sin(ang_c) * scale)], axis=0)
    return jnp.asarray(pos, F32).astype(BF16), jnp.asarray(chan, F32).astype(BF16)


def _fold_radix8(x):
    c = 0.5 ** 0.5
    e02p, e02m, e13p, e13m = x[0] + x[4], x[0] - x[4], x[2] + x[6], x[2] - x[6]
    o02p, o02m, o13p, o13m = x[1] + x[5], x[1] - x[5], x[3] + x[7], x[3] - x[7]
    e0, e2, o0, o2 = e02p + e13p, e02p - e13p, o02p + o13p, o02p - o13p
    p, q = c * (o02m - o13m), c * (o02m + o13m)
    re = [e0 + o0, e02m + p, e2, e02m - p, e0 - o0, e02m - p, e2, e02m + p]
    im1, im3 = -e13m - q, e13m - q
    im = [None, im1, -o2, im3, None, -im3, o2, -im1]
    return re, im


def _fourier_kernel(f_ref, pos_ref, chan_ref, o_ref, t_ref, zr_ref, zi_ref, y_ref):
    r = f_ref.shape[1] // DFT_RADIX
    w = FOURIER_WIDTH
    pair = V7X_MXU_DIM
    slab = y_ref.shape[2]
    for lo in range(0, w, pair):
        for sub in range(lo, lo + pair, FOURIER_GROUP):
            lanes = slice(sub, sub + FOURIER_GROUP)
            x = [f_ref[0, n1 * r:(n1 + 1) * r, lanes].astype(F32) for n1 in range(DFT_RADIX)]
            re, im = _fold_radix8(x)
            for k1 in range(DFT_RADIX):
                t_ref[k1, 0:r, lanes] = re[k1].astype(BF16)
                t_ref[k1, r:2 * r, lanes] = (
                    jnp.zeros((r, FOURIER_GROUP), BF16) if im[k1] is None else im[k1].astype(BF16))
        h = lo // pair
        for k1 in range(DFT_RADIX):
            z = _dot(pos_ref[k1], t_ref[k1, :, lo:lo + pair])
            zr_ref[h, k1 * r:(k1 + 1) * r, :] = z[0:r].astype(BF16)
            zi_ref[h, k1 * r:(k1 + 1) * r, :] = z[r:2 * r].astype(BF16)
        y = _dot(zr_ref[h], chan_ref[0:pair, :]) + _dot(zi_ref[h], chan_ref[pair:2 * pair, :])
        for k1 in range(DFT_RADIX):
            for j in range(pair // slab):
                y_ref[(lo + j * slab) // slab, pl.ds(k1, r, stride=DFT_RADIX), :] = (
                    y[k1 * r:(k1 + 1) * r, j * slab:(j + 1) * slab])
    for j in range(w // slab):
        o_ref[0, :, j * slab:(j + 1) * slab] = y_ref[j].astype(BF16)


def _fourier(f):
    nb, n, w = f.shape
    pos, chan = _dft_tables(n)
    r = n // DFT_RADIX
    blk = pl.BlockSpec((1, n, w), lambda b: (b, 0, 0))
    lanes = V7X_LANES
    return pl.pallas_call(
        _fourier_kernel,
        out_shape=jax.ShapeDtypeStruct(f.shape, BF16),
        grid=(nb,),
        in_specs=[blk, _resident(pos.shape), _resident(chan.shape)],
        out_specs=blk,
        scratch_shapes=[pltpu.VMEM((DFT_RADIX, 2 * r, w), BF16),
                        pltpu.VMEM((w // V7X_MXU_DIM, n, V7X_MXU_DIM), BF16),
                        pltpu.VMEM((w // V7X_MXU_DIM, n, V7X_MXU_DIM), BF16),
                        pltpu.VMEM((w // lanes, n, lanes), F32)],
        compiler_params=_params("parallel"),
        name="fourier",
    )(f, pos, chan)


MERGE_INPUT_BUFFERS = 3


def _merge_kernel(x_hbm, mod_hbm, ya_hbm, yf_hbm, ga_hbm, gf_hbm, wab_ref, wfb_ref, wo_ref, o_hbm,
                  *, grid, tm):
    def step(x_ref, mod_ref, ya_ref, yf_ref, ga_ref, gf_ref, o_ref):
        gate = mod_ref[0, 5:6, :]
        merged = (ga_ref[0].astype(F32) * _dot(ya_ref[0], wab_ref[...])
                  + gf_ref[0].astype(F32) * _dot(yf_ref[0], wfb_ref[...]))
        mix = _dot(merged.astype(BF16), wo_ref[...])
        o_ref[0] = x_ref[0] + gate * mix

    def tok(w, buffers=None):
        mode = {} if buffers is None else {"pipeline_mode": pl.Buffered(buffers)}
        return pl.BlockSpec((1, tm, w), lambda b, i: (b, i, 0), **mode)

    deep = MERGE_INPUT_BUFFERS
    pltpu.emit_pipeline(
        step, grid=grid,
        in_specs=[tok(D_MODEL, deep), pl.BlockSpec((1, N_MOD, D_MODEL), lambda b, i: (b, 0, 0)),
                  tok(Q_WIDTH), tok(FOURIER_WIDTH), tok(D_MODEL), tok(D_MODEL)],
        out_specs=[tok(D_MODEL)],
    )(x_hbm, mod_hbm, ya_hbm, yf_hbm, ga_hbm, gf_hbm, o_hbm)


def _merge(x, mod, y_attn, y_four, sig_a, sig_f, w_ab, w_fb, w_o):
    nb, t, _ = x.shape
    tm = WIDE_TOKEN_TILE
    hbm = pl.BlockSpec(memory_space=pl.ANY)
    vmem = pl.BlockSpec(memory_space=pltpu.VMEM)
    return pl.pallas_call(
        functools.partial(_merge_kernel, grid=(nb, t // tm), tm=tm),
        out_shape=jax.ShapeDtypeStruct(x.shape, F32),
        in_specs=[hbm] * 6 + [vmem] * 3,
        out_specs=hbm,
        compiler_params=pltpu.CompilerParams(vmem_limit_bytes=VMEM_LIMIT),
        name="merge",
    )(x, mod, y_attn, y_four, sig_a, sig_f, w_ab, w_fb, w_o)


def _rope_tables(n_tokens):
    f32 = np.float32
    rows = n_tokens // GRID_W
    row_ids = np.repeat(np.arange(rows, dtype=f32), GRID_W)
    col_ids = np.tile(np.arange(GRID_W, dtype=f32), rows)
    inv_freq = f32(ROPE_THETA) ** (-np.arange(0, AXIS_ROPE_DIM, 2, dtype=f32) / f32(AXIS_ROPE_DIM))
    ang = np.concatenate([row_ids[:, None] * inv_freq, col_ids[:, None] * inv_freq], axis=-1)
    cos, sin = np.cos(ang), np.sin(ang)
    assert ang.dtype == f32 and cos.dtype == f32
    return (jnp.asarray(np.concatenate([cos, cos], axis=-1)),
            jnp.asarray(np.concatenate([-sin, sin], axis=-1)))


def kernel(x, c, ctx, c_ctx, w_ada, b_ada, norm_ffn1, w_ffn1_in, w_ffn1_out, norm_mix, w_in,
           q_norm, k_norm, w_attn_branch, w_fourier_branch, w_out, norm_ffn2, w_ffn2_in, w_ffn2_out):
    nb, n_lat, _ = x.shape
    n_ctx = ctx.shape[1]
    depth = w_ada.shape[0]
    assert depth == 1, "the context stream is only carried through the (single) last layer"
    cos2, sin2 = _rope_tables(n_lat)
    ctx_row = nb
    pad_rows = -(nb + 1) % V7X_SUBLANES
    c_all = jnp.concatenate([c, c_ctx[None, :], jnp.zeros((pad_rows, D_MODEL), F32)], axis=0)

    i = 0
    mod = _adaln(c_all, w_ada[i], b_ada[i][None, :]).reshape(c_all.shape[0], N_MOD, D_MODEL)
    own_row, shared_row = (lambda b: b), (lambda b: ctx_row)
    x, ctx_flat = _ffn([x, ctx.reshape(1, nb * n_ctx, D_MODEL)], mod, [own_row, shared_row], 0,
                       norm_ffn1[i][None, :], w_ffn1_in[i], w_ffn1_out[i])

    q, k, vx, f, sig_a, sig_f, k_c, vx_c = _inproj(
        x, ctx_flat, mod, [own_row, shared_row], norm_mix[i][None, :], w_in[i],
        q_norm[i][None, :], k_norm[i][None, :], cos2, sin2)
    y_attn = _attention(q, k, vx, k_c.reshape(nb, n_ctx, KV_WIDTH), vx_c.reshape(nb, n_ctx, VX_WIDTH))
    y_four = _fourier(f)
    x = _merge(x, mod, y_attn, y_four, sig_a, sig_f, w_attn_branch[i], w_fourier_branch[i], w_out[i])

    return _ffn([x], mod, [own_row], 6, norm_ffn2[i][None, :], w_ffn2_in[i], w_ffn2_out[i])[0]
```

```python
import functools

import numpy as np
import jax
import jax.numpy as jnp
from jax import lax
from jax.experimental import pallas as pl
from jax.experimental.pallas import tpu as pltpu

D_MODEL = 1024
GRID_W = 64
HEAD_DIM = 128
N_Q_HEADS = D_MODEL // HEAD_DIM
N_KV_HEADS = N_Q_HEADS // 4
GQA_GROUP = N_Q_HEADS // N_KV_HEADS
Q_WIDTH = N_Q_HEADS * HEAD_DIM
KV_WIDTH = N_KV_HEADS * HEAD_DIM
FOURIER_GROUP = 128
N_FOURIER_GROUPS = 4
FOURIER_WIDTH = N_FOURIER_GROUPS * FOURIER_GROUP
D_FF = 2816
AXIS_ROPE_DIM = HEAD_DIM // 2
ROPE_THETA = 10000.0
EPS = 1e-6
N_MOD = 9
ATTN_SCALE = HEAD_DIM ** -0.5
LOG2_E = 1.4426950408889634

K_OFF = Q_WIDTH
V_OFF = K_OFF + KV_WIDTH
F_OFF = V_OFF + KV_WIDTH
GA_OFF = F_OFF + FOURIER_WIDTH
GF_OFF = GA_OFF + D_MODEL

V7X_MXU_DIM = 256
V7X_LANES = 128
V7X_SUBLANES = 8
VX_WIDTH = N_KV_HEADS * V7X_MXU_DIM
V7X_VMEM_BYTES = 64 * 2**20
VMEM_LIMIT = V7X_VMEM_BYTES - 8 * 2**20

TOKEN_TILE = 512
WIDE_TOKEN_TILE = 1024
FF_CHUNK = V7X_MXU_DIM
ATTN_Q_TILE = 512
ADALN_TILE = 2304

F32 = jnp.float32
BF16 = jnp.bfloat16


def _params(*sem):
    return pltpu.CompilerParams(dimension_semantics=sem, vmem_limit_bytes=VMEM_LIMIT)


def _resident(shape):
    return pl.BlockSpec(shape, lambda *_: (0,) * len(shape), pipeline_mode=pl.Buffered(1))


def _rms(x, gain):
    return x * lax.rsqrt(jnp.mean(x * x, axis=-1, keepdims=True) + EPS) * gain


def _dot(a, b):
    return jnp.dot(a, b.astype(BF16), preferred_element_type=F32)


def _sigmoid(x):
    return 0.5 * jnp.tanh(0.5 * x) + 0.5


def _adaln_kernel(c_ref, w_ref, b_ref, o_ref):
    c = c_ref[...]
    a = (c * _sigmoid(c)).astype(BF16)
    o_ref[...] = _dot(a, w_ref[...]) + b_ref[...]


def _adaln(c_all, w, b):
    rows, n_out = c_all.shape[0], w.shape[1]
    assert rows % V7X_SUBLANES == 0 and n_out % ADALN_TILE == 0, (rows, n_out)
    return pl.pallas_call(
        _adaln_kernel,
        out_shape=jax.ShapeDtypeStruct((rows, n_out), F32),
        grid=(n_out // ADALN_TILE,),
        in_specs=[
            pl.BlockSpec((rows, D_MODEL), lambda j: (0, 0)),
            pl.BlockSpec((D_MODEL, ADALN_TILE), lambda j: (0, j)),
            pl.BlockSpec((1, ADALN_TILE), lambda j: (0, j)),
        ],
        out_specs=pl.BlockSpec((rows, ADALN_TILE), lambda j: (0, j)),
        compiler_params=_params("arbitrary"),
        name="adaln",
    )(c_all, w, b)


class _TokenStreams:
    def __init__(self, arrays, tile):
        assert all(a.ndim == 3 and a.shape[1] % tile == 0 for a in arrays), [a.shape for a in arrays]
        self.tile = tile
        self.per_batch = [a.shape[1] // tile for a in arrays]
        counts = [a.shape[0] * p for a, p in zip(arrays, self.per_batch)]
        self.starts = [sum(counts[:s]) for s in range(len(arrays) + 1)]
        self.steps = self.starts[-1]

    def batch_and_tile(self, s, t):
        local = jnp.clip(t - self.starts[s], 0, self.starts[s + 1] - self.starts[s] - 1)
        return local // self.per_batch[s], local % self.per_batch[s]

    def token_spec(self, s, width):
        def index(t):
            b, i = self.batch_and_tile(s, t)
            return b, i, 0
        return pl.BlockSpec((1, self.tile, width), index)

    def mod_spec(self, mod_rows):
        def index(t):
            row = mod_rows[0](self.batch_and_tile(0, t)[0])
            for s in range(1, len(mod_rows)):
                row = jnp.where(t >= self.starts[s], mod_rows[s](self.batch_and_tile(s, t)[0]), row)
            return row, 0, 0
        return pl.BlockSpec((1, N_MOD, D_MODEL), index)

    def active(self, s):
        t = pl.program_id(0)
        return jnp.logical_and(t >= self.starts[s], t < self.starts[s + 1])


def _ffn_kernel(*refs, streams, mod_base):
    n = len(streams.per_batch)
    x_refs, (mod_ref, gain_ref, win_ref, wout_ref), o_refs, act_ref = (
        refs[:n], refs[n:n + 4], refs[n + 4:2 * n + 4], refs[2 * n + 4])

    def half_step(x_ref, o_ref):
        x = x_ref[0]
        shift = mod_ref[0, mod_base:mod_base + 1, :]
        scale = mod_ref[0, mod_base + 1:mod_base + 2, :]
        gate = mod_ref[0, mod_base + 2:mod_base + 3, :]
        h = (_rms(x, gain_ref[...]) * (1.0 + scale) + shift).astype(BF16)
        for j in range(D_FF // FF_CHUNK):
            lo = j * FF_CHUNK
            g = _dot(h, win_ref[:, lo:lo + FF_CHUNK])
            u = _dot(h, win_ref[:, D_FF + lo:D_FF + lo + FF_CHUNK])
            act_ref[:, lo:lo + FF_CHUNK] = (g * _sigmoid(g) * u).astype(BF16)
        y = _dot(act_ref[...], wout_ref[...])
        o_ref[0] = x + 0.5 * gate * y

    if n == 1:
        half_step(x_refs[0], o_refs[0])
    else:
        for s in range(n):
            pl.when(streams.active(s))(functools.partial(half_step, x_refs[s], o_refs[s]))


def _ffn(xs, mod, mod_rows, mod_base, gain, w_in, w_out):
    tile = TOKEN_TILE
    streams = _TokenStreams(xs, tile)
    tok = [streams.token_spec(s, D_MODEL) for s in range(len(xs))]
    return pl.pallas_call(
        functools.partial(_ffn_kernel, streams=streams, mod_base=mod_base),
        out_shape=[jax.ShapeDtypeStruct(x.shape, F32) for x in xs],
        grid=(streams.steps,),
        in_specs=tok + [
            streams.mod_spec(mod_rows),
            _resident((1, D_MODEL)),
            _resident((D_MODEL, 2 * D_FF)),
            _resident((D_FF, D_MODEL)),
        ],
        out_specs=tok,
        scratch_shapes=[pltpu.VMEM((tile, D_FF), BF16)],
        compiler_params=_params("arbitrary"),
        name="ffn",
    )(*xs, mod, gain, w_in, w_out)


def _head_norm(t, gain):
    return t * lax.rsqrt(jnp.mean(t * t, axis=-1, keepdims=True) + EPS) * gain


def _rope(t, cos2, sin2):
    return t * cos2 + pltpu.roll(t, AXIS_ROPE_DIM, 1) * sin2


def _store_widened_values(v_ref, v):
    ones = jnp.ones((v.shape[0], V7X_MXU_DIM - HEAD_DIM), BF16)
    for hd in range(N_KV_HEADS):
        lo = hd * V7X_MXU_DIM
        v_ref[0, :, lo:lo + HEAD_DIM] = v[:, hd * HEAD_DIM:(hd + 1) * HEAD_DIM].astype(BF16)
        v_ref[0, :, lo + HEAD_DIM:lo + V7X_MXU_DIM] = ones


def _inproj_kernel(x_ref, ctx_ref, mod_ref, gain_ref, w_ref, qn_ref, kn_ref, cos_ref, sin_ref,
                   q_ref, k_ref, v_ref, f_ref, ga_ref, gf_ref, kc_ref, vc_ref, *, streams):
    kn = kn_ref[...]
    heads_per_dot = V7X_MXU_DIM // HEAD_DIM

    def normed_input(tok_ref):
        shift = mod_ref[0, 3:4, :]
        scale = mod_ref[0, 4:5, :]
        return (_rms(tok_ref[0], gain_ref[...]) * (1.0 + scale) + shift).astype(BF16)

    @pl.when(streams.active(0))
    def _latent_tokens():
        h = normed_input(x_ref)

        def wide_dot(col):
            return _dot(h, w_ref[:, col:col + V7X_MXU_DIM])

        cos2 = cos_ref[...]
        sin2 = sin_ref[...]
        qn = qn_ref[...] * (ATTN_SCALE * LOG2_E)
        for c in range(Q_WIDTH // V7X_MXU_DIM):
            t = wide_dot(c * V7X_MXU_DIM)
            for j in range(heads_per_dot):
                tj = t[:, j * HEAD_DIM:(j + 1) * HEAD_DIM]
                q_ref[0, c * heads_per_dot + j] = _rope(_head_norm(tj, qn), cos2, sin2).astype(BF16)
        for c in range(KV_WIDTH // V7X_MXU_DIM):
            t = wide_dot(K_OFF + c * V7X_MXU_DIM)
            for j in range(heads_per_dot):
                lo = c * V7X_MXU_DIM + j * HEAD_DIM
                tj = t[:, j * HEAD_DIM:(j + 1) * HEAD_DIM]
                k_ref[0, :, lo:lo + HEAD_DIM] = _rope(_head_norm(tj, kn), cos2, sin2).astype(BF16)
        _store_widened_values(v_ref, _dot(h, w_ref[:, V_OFF:V_OFF + KV_WIDTH]))
        for c in range(FOURIER_WIDTH // V7X_MXU_DIM):
            lo = c * V7X_MXU_DIM
            f_ref[0, :, lo:lo + V7X_MXU_DIM] = wide_dot(F_OFF + lo).astype(BF16)
        for c in range(D_MODEL // V7X_MXU_DIM):
            lo = c * V7X_MXU_DIM
            ga_ref[0, :, lo:lo + V7X_MXU_DIM] = _sigmoid(wide_dot(GA_OFF + lo)).astype(BF16)
            gf_ref[0, :, lo:lo + V7X_MXU_DIM] = _sigmoid(wide_dot(GF_OFF + lo)).astype(BF16)

    @pl.when(streams.active(1))
    def _context_tokens():
        h = normed_input(ctx_ref)
        t = _dot(h, w_ref[:, K_OFF:K_OFF + KV_WIDTH])
        for hd in range(N_KV_HEADS):
            lo = hd * HEAD_DIM
            kc_ref[0, :, lo:lo + HEAD_DIM] = _head_norm(t[:, lo:lo + HEAD_DIM], kn).astype(BF16)
        _store_widened_values(vc_ref, _dot(h, w_ref[:, V_OFF:V_OFF + KV_WIDTH]))


def _inproj(x, ctx, mod, mod_rows, gain, w_in, q_norm, k_norm, cos2, sin2):
    nb, t, _ = x.shape
    streams = _TokenStreams([x, ctx], TOKEN_TILE)
    tm = streams.tile
    tok = lambda w: streams.token_spec(0, w)
    ctx_tok = lambda w: streams.token_spec(1, w)
    shp = lambda w: jax.ShapeDtypeStruct((nb, t, w), BF16)
    ctx_shp = lambda w: jax.ShapeDtypeStruct(ctx.shape[:2] + (w,), BF16)

    def q_index(step):
        b, i = streams.batch_and_tile(0, step)
        return b, 0, i, 0

    def rope_index(step):
        return streams.batch_and_tile(0, step)[1], 0

    return pl.pallas_call(
        functools.partial(_inproj_kernel, streams=streams),
        out_shape=[jax.ShapeDtypeStruct((nb, N_Q_HEADS, t, HEAD_DIM), BF16), shp(KV_WIDTH),
                   shp(VX_WIDTH), shp(FOURIER_WIDTH), shp(D_MODEL), shp(D_MODEL),
                   ctx_shp(KV_WIDTH), ctx_shp(VX_WIDTH)],
        grid=(streams.steps,),
        in_specs=[
            tok(D_MODEL),
            ctx_tok(D_MODEL),
            streams.mod_spec(mod_rows),
            _resident((1, D_MODEL)),
            _resident(w_in.shape),
            _resident((1, HEAD_DIM)),
            _resident((1, HEAD_DIM)),
            pl.BlockSpec((tm, HEAD_DIM), rope_index),
            pl.BlockSpec((tm, HEAD_DIM), rope_index),
        ],
        out_specs=[pl.BlockSpec((1, N_Q_HEADS, tm, HEAD_DIM), q_index),
                   tok(KV_WIDTH), tok(VX_WIDTH), tok(FOURIER_WIDTH), tok(D_MODEL), tok(D_MODEL),
                   ctx_tok(KV_WIDTH), ctx_tok(VX_WIDTH)],
        compiler_params=_params("arbitrary"),
        name="inproj",
    )(x, ctx, mod, gain, w_in, q_norm, k_norm, cos2, sin2)


_NT = (((1,), (1,)), ((), ()))


ATTN_SCRATCH_PER_HEAD = 5


def _attn_kernel(q_ref, kc_ref, k_ref, vcx_ref, vx_ref, o_ref, *scratch):
    heads = [scratch[ATTN_SCRATCH_PER_HEAD * g:ATTN_SCRATCH_PER_HEAD * (g + 1)] for g in range(GQA_GROUP)]

    @pl.when(pl.program_id(0) == 0)
    def _no_previous_item():
        for pc_ref, pl_ref, sc_ref, sl_ref, m_ref in heads:
            pc_ref[...] = jnp.ones(pc_ref.shape, BF16)
            pl_ref[...] = jnp.ones(pl_ref.shape, BF16)
            sc_ref[...] = jnp.zeros(sc_ref.shape, F32)
            sl_ref[...] = jnp.zeros(sl_ref.shape, F32)
            m_ref[...] = jnp.zeros(m_ref.shape, F32)

    def weights_from_scores(pc_ref, pl_ref, sc_ref, sl_ref, m_ref):
        m = m_ref[...]
        for s_ref, p_ref in ((sc_ref, pc_ref), (sl_ref, pl_ref)):
            for lo in range(0, s_ref.shape[1], V7X_MXU_DIM):
                p_ref[:, lo:lo + V7X_MXU_DIM] = jnp.exp2(s_ref[:, lo:lo + V7X_MXU_DIM] - m).astype(BF16)

    weights_from_scores(*heads[-1])

    for g, (pc_ref, pl_ref, sc_ref, sl_ref, m_ref) in enumerate(heads):
        oe = _dot(pc_ref[...], vcx_ref[0]) + _dot(pl_ref[...], vx_ref[0])
        o_ref[0, :, g * HEAD_DIM:(g + 1) * HEAD_DIM] = (oe[:, :HEAD_DIM] / oe[:, HEAD_DIM:]).astype(BF16)

        q = q_ref[0, g]
        s_c = lax.dot_general(q, kc_ref[0], _NT, preferred_element_type=F32)
        s_l = lax.dot_general(q, k_ref[0], _NT, preferred_element_type=F32)
        m_ref[...] = jnp.maximum(jnp.max(s_c, axis=-1, keepdims=True), jnp.max(s_l, axis=-1, keepdims=True))
        sc_ref[...] = s_c
        sl_ref[...] = s_l

    for head in heads[:-1]:
        weights_from_scores(*head)


def _attention(q, k, vx, k_c, vx_c):
    nb, _, n, _ = q.shape
    n_ctx = k_c.shape[1]
    tq = ATTN_Q_TILE
    assert n % tq == 0 and n % V7X_MXU_DIM == 0 and n_ctx % V7X_MXU_DIM == 0, (n, n_ctx)
    tiles = n // tq
    items = nb * N_KV_HEADS * tiles

    def item(t):
        return t // (N_KV_HEADS * tiles), (t // tiles) % N_KV_HEADS, t % tiles

    def scored(t):
        return item(jnp.minimum(t, items - 1))

    def weighted(t):
        return item(jnp.maximum(t - 1, 0))

    def q_map(t):
        b, h, i = scored(t)
        return b, h, i, 0

    def k_map(t):
        b, h, _ = scored(t)
        return b, 0, h

    def v_map(t):
        b, h, _ = weighted(t)
        return b, 0, h

    def o_map(t):
        b, h, i = weighted(t)
        return b, i, h

    return pl.pallas_call(
        _attn_kernel,
        out_shape=jax.ShapeDtypeStruct((nb, n, Q_WIDTH), BF16),
        grid=(items + 1,),
        in_specs=[
            pl.BlockSpec((1, GQA_GROUP, tq, HEAD_DIM), q_map),
            pl.BlockSpec((1, n_ctx, HEAD_DIM), k_map),
            pl.BlockSpec((1, n, HEAD_DIM), k_map),
            pl.BlockSpec((1, n_ctx, V7X_MXU_DIM), v_map),
            pl.BlockSpec((1, n, V7X_MXU_DIM), v_map),
        ],
        out_specs=pl.BlockSpec((1, tq, GQA_GROUP * HEAD_DIM), o_map),
        scratch_shapes=GQA_GROUP * [
            pltpu.VMEM((tq, n_ctx), BF16), pltpu.VMEM((tq, n), BF16),
            pltpu.VMEM((tq, n_ctx), F32), pltpu.VMEM((tq, n), F32), pltpu.VMEM((tq, 1), F32)],
        compiler_params=_params("arbitrary"),
        name="attn",
    )(q, k_c, k, vx_c, vx)


DFT_RADIX = 8


def _dft_tables(n):
    r = n // DFT_RADIX
    k = np.arange(DFT_RADIX)[:, None, None] + DFT_RADIX * np.arange(r)[None, :, None]
    ang = 2.0 * np.pi * ((k * np.arange(r)[None, None, :]) % n) / n
    cos, sin = np.cos(ang), np.sin(ang)
    pos = np.concatenate([np.concatenate([cos, sin], axis=2),
                          np.concatenate([-sin, cos], axis=2)], axis=1)
    c = FOURIER_GROUP
    scale = 1.0 / np.sqrt(float(n * c))
    ang_c = 2.0 * np.pi * (np.outer(np.arange(c), np.arange(c)) % c) / c
    eye = np.eye(V7X_MXU_DIM // c)
    chan = np.concatenate([np.kron(eye, np.cos(ang_c) * scale),
                           np.kron(eye, np.sin(ang_c) * scale)], axis=0)
    return jnp.asarray(pos, F32).astype(BF16), jnp.asarray(chan, F32).astype(BF16)


def _fold_radix8(x):
    c = 0.5 ** 0.5
    e02p, e02m, e13p, e13m = x[0] + x[4], x[0] - x[4], x[2] + x[6], x[2] - x[6]
    o02p, o02m, o13p, o13m = x[1] + x[5], x[1] - x[5], x[3] + x[7], x[3] - x[7]
    e0, e2, o0, o2 = e02p + e13p, e02p - e13p, o02p + o13p, o02p - o13p
    p, q = c * (o02m - o13m), c * (o02m + o13m)
    re = [e0 + o0, e02m + p, e2, e02m - p, e0 - o0, e02m - p, e2, e02m + p]
    im1, im3 = -e13m - q, e13m - q
    im = [None, im1, -o2, im3, None, -im3, o2, -im1]
    return re, im


def _fourier_kernel(f_ref, pos_ref, chan_ref, o_ref, t_ref, zr_ref, zi_ref, y_ref):
    r = f_ref.shape[1] // DFT_RADIX
    w = FOURIER_WIDTH
    pair = V7X_MXU_DIM
    slab = y_ref.shape[2]
    for lo in range(0, w, pair):
        for sub in range(lo, lo + pair, FOURIER_GROUP):
            lanes = slice(sub, sub + FOURIER_GROUP)
            x = [f_ref[0, n1 * r:(n1 + 1) * r, lanes].astype(F32) for n1 in range(DFT_RADIX)]
            re, im = _fold_radix8(x)
            for k1 in range(DFT_RADIX):
                t_ref[k1, 0:r, lanes] = re[k1].astype(BF16)
                t_ref[k1, r:2 * r, lanes] = (
                    jnp.zeros((r, FOURIER_GROUP), BF16) if im[k1] is None else im[k1].astype(BF16))
        h = lo // pair
        for k1 in range(DFT_RADIX):
            z = _dot(pos_ref[k1], t_ref[k1, :, lo:lo + pair])
            zr_ref[h, k1 * r:(k1 + 1) * r, :] = z[0:r].astype(BF16)
            zi_ref[h, k1 * r:(k1 + 1) * r, :] = z[r:2 * r].astype(BF16)
        y = _dot(zr_ref[h], chan_ref[0:pair, :]) + _dot(zi_ref[h], chan_ref[pair:2 * pair, :])
        for k1 in range(DFT_RADIX):
            for j in range(pair // slab):
                y_ref[(lo + j * slab) // slab, pl.ds(k1, r, stride=DFT_RADIX), :] = (
                    y[k1 * r:(k1 + 1) * r, j * slab:(j + 1) * slab])
    for j in range(w // slab):
        o_ref[0, :, j * slab:(j + 1) * slab] = y_ref[j].astype(BF16)


def _fourier(f):
    nb, n, w = f.shape
    assert w == FOURIER_WIDTH and n % (DFT_RADIX * V7X_MXU_DIM) == 0, f.shape
    pos, chan = _dft_tables(n)
    r = n // DFT_RADIX
    blk = pl.BlockSpec((1, n, w), lambda b: (b, 0, 0))
    lanes = V7X_LANES
    return pl.pallas_call(
        _fourier_kernel,
        out_shape=jax.ShapeDtypeStruct(f.shape, BF16),
        grid=(nb,),
        in_specs=[blk, _resident(pos.shape), _resident(chan.shape)],
        out_specs=blk,
        scratch_shapes=[pltpu.VMEM((DFT_RADIX, 2 * r, w), BF16),
                        pltpu.VMEM((w // V7X_MXU_DIM, n, V7X_MXU_DIM), BF16),
                        pltpu.VMEM((w // V7X_MXU_DIM, n, V7X_MXU_DIM), BF16),
                        pltpu.VMEM((w // lanes, n, lanes), F32)],
        compiler_params=_params("parallel"),
        name="fourier",
    )(f, pos, chan)


def _merge_kernel(x_ref, mod_ref, ya_ref, yf_ref, ga_ref, gf_ref, wab_ref, wfb_ref, wo_ref, o_ref):
    gate = mod_ref[0, 5:6, :]
    merged = (ga_ref[0].astype(F32) * _dot(ya_ref[0], wab_ref[...])
              + gf_ref[0].astype(F32) * _dot(yf_ref[0], wfb_ref[...]))
    mix = _dot(merged.astype(BF16), wo_ref[...])
    o_ref[0] = x_ref[0] + gate * mix


def _merge(x, mod, y_attn, y_four, sig_a, sig_f, w_ab, w_fb, w_o):
    nb, t, _ = x.shape
    tm = WIDE_TOKEN_TILE
    assert t % tm == 0, x.shape
    tok = lambda w: pl.BlockSpec((1, tm, w), lambda b, i: (b, i, 0))
    return pl.pallas_call(
        _merge_kernel,
        out_shape=jax.ShapeDtypeStruct(x.shape, F32),
        grid=(nb, t // tm),
        in_specs=[
            tok(D_MODEL),
            pl.BlockSpec((1, N_MOD, D_MODEL), lambda b, i: (b, 0, 0)),
            tok(Q_WIDTH), tok(FOURIER_WIDTH), tok(D_MODEL), tok(D_MODEL),
            _resident(w_ab.shape), _resident(w_fb.shape), _resident(w_o.shape),
        ],
        out_specs=tok(D_MODEL),
        compiler_params=_params("parallel", "parallel"),
        name="merge",
    )(x, mod, y_attn, y_four, sig_a, sig_f, w_ab, w_fb, w_o)


def _rope_tables(n_tokens):
    f32 = np.float32
    rows = n_tokens // GRID_W
    row_ids = np.repeat(np.arange(rows, dtype=f32), GRID_W)
    col_ids = np.tile(np.arange(GRID_W, dtype=f32), rows)
    inv_freq = f32(ROPE_THETA) ** (-np.arange(0, AXIS_ROPE_DIM, 2, dtype=f32) / f32(AXIS_ROPE_DIM))
    ang = np.concatenate([row_ids[:, None] * inv_freq, col_ids[:, None] * inv_freq], axis=-1)
    cos, sin = np.cos(ang), np.sin(ang)
    assert ang.dtype == f32 and cos.dtype == f32
    return (jnp.asarray(np.concatenate([cos, cos], axis=-1)),
            jnp.asarray(np.concatenate([-sin, sin], axis=-1)))


def kernel(x, c, ctx, c_ctx, w_ada, b_ada, norm_ffn1, w_ffn1_in, w_ffn1_out, norm_mix, w_in,
           q_norm, k_norm, w_attn_branch, w_fourier_branch, w_out, norm_ffn2, w_ffn2_in, w_ffn2_out):
    nb, n_lat, _ = x.shape
    n_ctx = ctx.shape[1]
    depth = w_ada.shape[0]
    assert depth == 1, "the context stream is only carried through the (single) last layer"
    assert x.shape[2] == D_MODEL and ctx.shape == (nb, n_ctx, D_MODEL) and n_lat % GRID_W == 0
    assert w_in.shape[1:] == (D_MODEL, GF_OFF + D_MODEL) and w_ffn1_in.shape[1:] == (D_MODEL, 2 * D_FF)
    assert (nb * n_ctx) % TOKEN_TILE == 0, "context tokens are processed as one flat stream of tiles"
    cos2, sin2 = _rope_tables(n_lat)
    ctx_row = nb
    pad_rows = -(nb + 1) % V7X_SUBLANES
    c_all = jnp.concatenate([c, c_ctx[None, :], jnp.zeros((pad_rows, D_MODEL), F32)], axis=0)

    i = 0
    mod = _adaln(c_all, w_ada[i], b_ada[i][None, :]).reshape(c_all.shape[0], N_MOD, D_MODEL)
    own_row, shared_row = (lambda b: b), (lambda b: ctx_row)
    x, ctx_flat = _ffn([x, ctx.reshape(1, nb * n_ctx, D_MODEL)], mod, [own_row, shared_row], 0,
                       norm_ffn1[i][None, :], w_ffn1_in[i], w_ffn1_out[i])

    q, k, vx, f, sig_a, sig_f, k_c, vx_c = _inproj(
        x, ctx_flat, mod, [own_row, shared_row], norm_mix[i][None, :], w_in[i],
        q_norm[i][None, :], k_norm[i][None, :], cos2, sin2)
    y_attn = _attention(q, k, vx, k_c.reshape(nb, n_ctx, KV_WIDTH), vx_c.reshape(nb, n_ctx, VX_WIDTH))
    y_four = _fourier(f)
    x = _merge(x, mod, y_attn, y_four, sig_a, sig_f, w_attn_branch[i], w_fourier_branch[i], w_out[i])

    return _ffn([x], mod, [own_row], 6, norm_ffn2[i][None, :], w_ffn2_in[i], w_ffn2_out[i])[0]
```

```python
import functools

import numpy as np
import jax
import jax.numpy as jnp
from jax import lax
from jax.experimental import pallas as pl
from jax.experimental.pallas import tpu as pltpu

D_MODEL = 1024
GRID_W = 64
HEAD_DIM = 128
N_Q_HEADS = D_MODEL // HEAD_DIM
N_KV_HEADS = N_Q_HEADS // 4
GQA_GROUP = N_Q_HEADS // N_KV_HEADS
Q_WIDTH = N_Q_HEADS * HEAD_DIM
KV_WIDTH = N_KV_HEADS * HEAD_DIM
FOURIER_GROUP = 128
N_FOURIER_GROUPS = 4
FOURIER_WIDTH = N_FOURIER_GROUPS * FOURIER_GROUP
D_FF = 2816
AXIS_ROPE_DIM = HEAD_DIM // 2
ROPE_THETA = 10000.0
EPS = 1e-6
N_MOD = 9
ATTN_SCALE = HEAD_DIM ** -0.5
LOG2_E = 1.4426950408889634

K_OFF = Q_WIDTH
V_OFF = K_OFF + KV_WIDTH
F_OFF = V_OFF + KV_WIDTH
GA_OFF = F_OFF + FOURIER_WIDTH
GF_OFF = GA_OFF + D_MODEL

V7X_MXU_DIM = 256
V7X_LANES = 128
V7X_SUBLANES = 8
VX_WIDTH = N_KV_HEADS * V7X_MXU_DIM
V7X_VMEM_BYTES = 64 * 2**20
VMEM_LIMIT = V7X_VMEM_BYTES - 8 * 2**20

TOKEN_TILE = 512
WIDE_TOKEN_TILE = 1024
FF_CHUNK = V7X_MXU_DIM
ATTN_Q_TILE = 512
ADALN_TILE = 2304

F32 = jnp.float32
BF16 = jnp.bfloat16


def _params(*sem):
    return pltpu.CompilerParams(dimension_semantics=sem, vmem_limit_bytes=VMEM_LIMIT)


def _resident(shape):
    return pl.BlockSpec(shape, lambda *_: (0,) * len(shape), pipeline_mode=pl.Buffered(1))


def _rms(x, gain):
    return x * lax.rsqrt(jnp.mean(x * x, axis=-1, keepdims=True) + EPS) * gain


def _dot(a, b):
    return jnp.dot(a, b.astype(BF16), preferred_element_type=F32)


def _sigmoid(x):
    return 0.5 * jnp.tanh(0.5 * x) + 0.5


def _adaln_kernel(c_ref, w_ref, b_ref, o_ref):
    c = c_ref[...]
    a = (c * _sigmoid(c)).astype(BF16)
    o_ref[...] = _dot(a, w_ref[...]) + b_ref[...]


def _adaln(c_all, w, b):
    rows, n_out = c_all.shape[0], w.shape[1]
    assert rows % V7X_SUBLANES == 0 and n_out % ADALN_TILE == 0, (rows, n_out)
    return pl.pallas_call(
        _adaln_kernel,
        out_shape=jax.ShapeDtypeStruct((rows, n_out), F32),
        grid=(n_out // ADALN_TILE,),
        in_specs=[
            pl.BlockSpec((rows, D_MODEL), lambda j: (0, 0)),
            pl.BlockSpec((D_MODEL, ADALN_TILE), lambda j: (0, j)),
            pl.BlockSpec((1, ADALN_TILE), lambda j: (0, j)),
        ],
        out_specs=pl.BlockSpec((rows, ADALN_TILE), lambda j: (0, j)),
        compiler_params=_params("arbitrary"),
        name="adaln",
    )(c_all, w, b)


class _TokenStreams:
    def __init__(self, arrays, tile):
        assert all(a.ndim == 3 and a.shape[1] % tile == 0 for a in arrays), [a.shape for a in arrays]
        self.tile = tile
        self.per_batch = [a.shape[1] // tile for a in arrays]
        counts = [a.shape[0] * p for a, p in zip(arrays, self.per_batch)]
        self.starts = [sum(counts[:s]) for s in range(len(arrays) + 1)]
        self.steps = self.starts[-1]

    def batch_and_tile(self, s, t):
        local = jnp.clip(t - self.starts[s], 0, self.starts[s + 1] - self.starts[s] - 1)
        return local // self.per_batch[s], local % self.per_batch[s]

    def token_spec(self, s, width):
        def index(t):
            b, i = self.batch_and_tile(s, t)
            return b, i, 0
        return pl.BlockSpec((1, self.tile, width), index)

    def mod_spec(self, mod_rows):
        def index(t):
            row = mod_rows[0](self.batch_and_tile(0, t)[0])
            for s in range(1, len(mod_rows)):
                row = jnp.where(t >= self.starts[s], mod_rows[s](self.batch_and_tile(s, t)[0]), row)
            return row, 0, 0
        return pl.BlockSpec((1, N_MOD, D_MODEL), index)

    def active(self, s):
        t = pl.program_id(0)
        return jnp.logical_and(t >= self.starts[s], t < self.starts[s + 1])


def _ffn_kernel(*refs, streams, mod_base):
    n = len(streams.per_batch)
    x_refs, (mod_ref, gain_ref, win_ref, wout_ref), o_refs, act_ref = (
        refs[:n], refs[n:n + 4], refs[n + 4:2 * n + 4], refs[2 * n + 4])

    def half_step(x_ref, o_ref):
        x = x_ref[0]
        shift = mod_ref[0, mod_base:mod_base + 1, :]
        scale = mod_ref[0, mod_base + 1:mod_base + 2, :]
        gate = mod_ref[0, mod_base + 2:mod_base + 3, :]
        h = (_rms(x, gain_ref[...]) * (1.0 + scale) + shift).astype(BF16)
        for j in range(D_FF // FF_CHUNK):
            lo = j * FF_CHUNK
            g = _dot(h, win_ref[:, lo:lo + FF_CHUNK])
            u = _dot(h, win_ref[:, D_FF + lo:D_FF + lo + FF_CHUNK])
            act_ref[:, lo:lo + FF_CHUNK] = (g * _sigmoid(g) * u).astype(BF16)
        y = _dot(act_ref[...], wout_ref[...])
        o_ref[0] = x + 0.5 * gate * y

    if n == 1:
        half_step(x_refs[0], o_refs[0])
    else:
        for s in range(n):
            pl.when(streams.active(s))(functools.partial(half_step, x_refs[s], o_refs[s]))


def _ffn(xs, mod, mod_rows, mod_base, gain, w_in, w_out):
    tile = TOKEN_TILE
    streams = _TokenStreams(xs, tile)
    tok = [streams.token_spec(s, D_MODEL) for s in range(len(xs))]
    return pl.pallas_call(
        functools.partial(_ffn_kernel, streams=streams, mod_base=mod_base),
        out_shape=[jax.ShapeDtypeStruct(x.shape, F32) for x in xs],
        grid=(streams.steps,),
        in_specs=tok + [
            streams.mod_spec(mod_rows),
            _resident((1, D_MODEL)),
            _resident((D_MODEL, 2 * D_FF)),
            _resident((D_FF, D_MODEL)),
        ],
        out_specs=tok,
        scratch_shapes=[pltpu.VMEM((tile, D_FF), BF16)],
        compiler_params=_params("arbitrary"),
        name="ffn",
    )(*xs, mod, gain, w_in, w_out)


def _head_norm(t, gain):
    return t * lax.rsqrt(jnp.mean(t * t, axis=-1, keepdims=True) + EPS) * gain


def _rope(t, cos2, sin2):
    return t * cos2 + pltpu.roll(t, AXIS_ROPE_DIM, 1) * sin2


def _store_widened_values(v_ref, v):
    ones = jnp.ones((v.shape[0], V7X_MXU_DIM - HEAD_DIM), BF16)
    for hd in range(N_KV_HEADS):
        lo = hd * V7X_MXU_DIM
        v_ref[0, :, lo:lo + HEAD_DIM] = v[:, hd * HEAD_DIM:(hd + 1) * HEAD_DIM].astype(BF16)
        v_ref[0, :, lo + HEAD_DIM:lo + V7X_MXU_DIM] = ones


def _inproj_kernel(x_ref, ctx_ref, mod_ref, gain_ref, w_ref, qn_ref, kn_ref, cos_ref, sin_ref,
                   q_ref, k_ref, v_ref, f_ref, ga_ref, gf_ref, kc_ref, vc_ref, *, streams):
    kn = kn_ref[...]
    heads_per_dot = V7X_MXU_DIM // HEAD_DIM

    def normed_input(tok_ref):
        shift = mod_ref[0, 3:4, :]
        scale = mod_ref[0, 4:5, :]
        return (_rms(tok_ref[0], gain_ref[...]) * (1.0 + scale) + shift).astype(BF16)

    @pl.when(streams.active(0))
    def _latent_tokens():
        h = normed_input(x_ref)

        def wide_dot(col):
            return _dot(h, w_ref[:, col:col + V7X_MXU_DIM])

        cos2 = cos_ref[...]
        sin2 = sin_ref[...]
        qn = qn_ref[...] * (ATTN_SCALE * LOG2_E)
        for c in range(Q_WIDTH // V7X_MXU_DIM):
            t = wide_dot(c * V7X_MXU_DIM)
            for j in range(heads_per_dot):
                tj = t[:, j * HEAD_DIM:(j + 1) * HEAD_DIM]
                q_ref[0, c * heads_per_dot + j] = _rope(_head_norm(tj, qn), cos2, sin2).astype(BF16)
        for c in range(KV_WIDTH // V7X_MXU_DIM):
            t = wide_dot(K_OFF + c * V7X_MXU_DIM)
            for j in range(heads_per_dot):
                lo = c * V7X_MXU_DIM + j * HEAD_DIM
                tj = t[:, j * HEAD_DIM:(j + 1) * HEAD_DIM]
                k_ref[0, :, lo:lo + HEAD_DIM] = _rope(_head_norm(tj, kn), cos2, sin2).astype(BF16)
        _store_widened_values(v_ref, _dot(h, w_ref[:, V_OFF:V_OFF + KV_WIDTH]))
        for c in range(FOURIER_WIDTH // V7X_MXU_DIM):
            lo = c * V7X_MXU_DIM
            f_ref[0, :, lo:lo + V7X_MXU_DIM] = wide_dot(F_OFF + lo).astype(BF16)
        for c in range(D_MODEL // V7X_MXU_DIM):
            lo = c * V7X_MXU_DIM
            ga_ref[0, :, lo:lo + V7X_MXU_DIM] = _sigmoid(wide_dot(GA_OFF + lo)).astype(BF16)
            gf_ref[0, :, lo:lo + V7X_MXU_DIM] = _sigmoid(wide_dot(GF_OFF + lo)).astype(BF16)

    @pl.when(streams.active(1))
    def _context_tokens():
        h = normed_input(ctx_ref)
        t = _dot(h, w_ref[:, K_OFF:K_OFF + KV_WIDTH])
        for hd in range(N_KV_HEADS):
            lo = hd * HEAD_DIM
            kc_ref[0, :, lo:lo + HEAD_DIM] = _head_norm(t[:, lo:lo + HEAD_DIM], kn).astype(BF16)
        _store_widened_values(vc_ref, _dot(h, w_ref[:, V_OFF:V_OFF + KV_WIDTH]))


def _inproj(x, ctx, mod, mod_rows, gain, w_in, q_norm, k_norm, cos2, sin2):
    nb, t, _ = x.shape
    streams = _TokenStreams([x, ctx], TOKEN_TILE)
    tm = streams.tile
    tok = lambda w: streams.token_spec(0, w)
    ctx_tok = lambda w: streams.token_spec(1, w)
    shp = lambda w: jax.ShapeDtypeStruct((nb, t, w), BF16)
    ctx_shp = lambda w: jax.ShapeDtypeStruct(ctx.shape[:2] + (w,), BF16)

    def q_index(step):
        b, i = streams.batch_and_tile(0, step)
        return b, 0, i, 0

    def rope_index(step):
        return streams.batch_and_tile(0, step)[1], 0

    return pl.pallas_call(
        functools.partial(_inproj_kernel, streams=streams),
        out_shape=[jax.ShapeDtypeStruct((nb, N_Q_HEADS, t, HEAD_DIM), BF16), shp(KV_WIDTH),
                   shp(VX_WIDTH), shp(FOURIER_WIDTH), shp(D_MODEL), shp(D_MODEL),
                   ctx_shp(KV_WIDTH), ctx_shp(VX_WIDTH)],
        grid=(streams.steps,),
        in_specs=[
            tok(D_MODEL),
            ctx_tok(D_MODEL),
            streams.mod_spec(mod_rows),
            _resident((1, D_MODEL)),
            _resident(w_in.shape),
            _resident((1, HEAD_DIM)),
            _resident((1, HEAD_DIM)),
            pl.BlockSpec((tm, HEAD_DIM), rope_index),
            pl.BlockSpec((tm, HEAD_DIM), rope_index),
        ],
        out_specs=[pl.BlockSpec((1, N_Q_HEADS, tm, HEAD_DIM), q_index),
                   tok(KV_WIDTH), tok(VX_WIDTH), tok(FOURIER_WIDTH), tok(D_MODEL), tok(D_MODEL),
                   ctx_tok(KV_WIDTH), ctx_tok(VX_WIDTH)],
        compiler_params=_params("arbitrary"),
        name="inproj",
    )(x, ctx, mod, gain, w_in, q_norm, k_norm, cos2, sin2)


_NT = (((1,), (1,)), ((), ()))


ATTN_SCRATCH_PER_HEAD = 5


def _attn_kernel(q_ref, kc_ref, k_ref, vcx_ref, vx_ref, o_ref, *scratch):
    heads = [scratch[ATTN_SCRATCH_PER_HEAD * g:ATTN_SCRATCH_PER_HEAD * (g + 1)] for g in range(GQA_GROUP)]

    @pl.when(pl.program_id(0) == 0)
    def _no_previous_item():
        for pc_ref, pl_ref, sc_ref, sl_ref, m_ref in heads:
            pc_ref[...] = jnp.ones(pc_ref.shape, BF16)
            pl_ref[...] = jnp.ones(pl_ref.shape, BF16)
            sc_ref[...] = jnp.zeros(sc_ref.shape, F32)
            sl_ref[...] = jnp.zeros(sl_ref.shape, F32)
            m_ref[...] = jnp.zeros(m_ref.shape, F32)

    def weights_from_scores(pc_ref, pl_ref, sc_ref, sl_ref, m_ref):
        m = m_ref[...]
        for s_ref, p_ref in ((sc_ref, pc_ref), (sl_ref, pl_ref)):
            for lo in range(0, s_ref.shape[1], V7X_MXU_DIM):
                p_ref[:, lo:lo + V7X_MXU_DIM] = jnp.exp2(s_ref[:, lo:lo + V7X_MXU_DIM] - m).astype(BF16)

    weights_from_scores(*heads[-1])

    for g, (pc_ref, pl_ref, sc_ref, sl_ref, m_ref) in enumerate(heads):
        oe = _dot(pc_ref[...], vcx_ref[0]) + _dot(pl_ref[...], vx_ref[0])
        o_ref[0, :, g * HEAD_DIM:(g + 1) * HEAD_DIM] = (oe[:, :HEAD_DIM] / oe[:, HEAD_DIM:]).astype(BF16)

        q = q_ref[0, g]
        s_c = lax.dot_general(q, kc_ref[0], _NT, preferred_element_type=F32)
        s_l = lax.dot_general(q, k_ref[0], _NT, preferred_element_type=F32)
        m_ref[...] = jnp.maximum(jnp.max(s_c, axis=-1, keepdims=True), jnp.max(s_l, axis=-1, keepdims=True))
        sc_ref[...] = s_c
        sl_ref[...] = s_l

    for head in heads[:-1]:
        weights_from_scores(*head)


def _attention(q, k, vx, k_c, vx_c):
    nb, _, n, _ = q.shape
    n_ctx = k_c.shape[1]
    tq = ATTN_Q_TILE
    assert n % tq == 0 and n % V7X_MXU_DIM == 0 and n_ctx % V7X_MXU_DIM == 0, (n, n_ctx)
    tiles = n // tq
    items = nb * N_KV_HEADS * tiles

    def item(t):
        return t // (N_KV_HEADS * tiles), (t // tiles) % N_KV_HEADS, t % tiles

    def scored(t):
        return item(jnp.minimum(t, items - 1))

    def weighted(t):
        return item(jnp.maximum(t - 1, 0))

    def q_map(t):
        b, h, i = scored(t)
        return b, h, i, 0

    def k_map(t):
        b, h, _ = scored(t)
        return b, 0, h

    def v_map(t):
        b, h, _ = weighted(t)
        return b, 0, h

    def o_map(t):
        b, h, i = weighted(t)
        return b, i, h

    return pl.pallas_call(
        _attn_kernel,
        out_shape=jax.ShapeDtypeStruct((nb, n, Q_WIDTH), BF16),
        grid=(items + 1,),
        in_specs=[
            pl.BlockSpec((1, GQA_GROUP, tq, HEAD_DIM), q_map),
            pl.BlockSpec((1, n_ctx, HEAD_DIM), k_map),
            pl.BlockSpec((1, n, HEAD_DIM), k_map),
            pl.BlockSpec((1, n_ctx, V7X_MXU_DIM), v_map),
            pl.BlockSpec((1, n, V7X_MXU_DIM), v_map),
        ],
        out_specs=pl.BlockSpec((1, tq, GQA_GROUP * HEAD_DIM), o_map),
        scratch_shapes=GQA_GROUP * [
            pltpu.VMEM((tq, n_ctx), BF16), pltpu.VMEM((tq, n), BF16),
            pltpu.VMEM((tq, n_ctx), F32), pltpu.VMEM((tq, n), F32), pltpu.VMEM((tq, 1), F32)],
        compiler_params=_params("arbitrary"),
        name="attn",
    )(q, k_c, k, vx_c, vx)


DFT_RADIX = 8


def _dft_tables(n):
    r = n // DFT_RADIX
    k = np.arange(DFT_RADIX)[:, None, None] + DFT_RADIX * np.arange(r)[None, :, None]
    ang = 2.0 * np.pi * ((k * np.arange(r)[None, None, :]) % n) / n
    cos, sin = np.cos(ang), np.sin(ang)
    pos = np.concatenate([np.concatenate([cos, sin], axis=2),
                          np.concatenate([-sin, cos], axis=2)], axis=1)
    c = FOURIER_GROUP
    scale = 1.0 / np.sqrt(float(n * c))
    ang_c = 2.0 * np.pi * (np.outer(np.arange(c), np.arange(c)) % c) / c
    eye = np.eye(V7X_MXU_DIM // c)
    chan = np.concatenate([np.kron(eye, np.cos(ang_c) * scale),
                           np.kron(eye, np.sin(ang_c) * scale)], axis=0)
    return jnp.asarray(pos, F32).astype(BF16), jnp.asarray(chan, F32).astype(BF16)


def _fold_radix8(x):
    c = 0.5 ** 0.5
    e02p, e02m, e13p, e13m = x[0] + x[4], x[0] - x[4], x[2] + x[6], x[2] - x[6]
    o02p, o02m, o13p, o13m = x[1] + x[5], x[1] - x[5], x[3] + x[7], x[3] - x[7]
    e0, e2, o0, o2 = e02p + e13p, e02p - e13p, o02p + o13p, o02p - o13p
    p, q = c * (o02m - o13m), c * (o02m + o13m)
    re = [e0 + o0, e02m + p, e2, e02m - p, e0 - o0, e02m - p, e2, e02m + p]
    im1, im3 = -e13m - q, e13m - q
    im = [None, im1, -o2, im3, None, -im3, o2, -im1]
    return re, im


def _fourier_kernel(f_ref, pos_ref, chan_ref, o_ref, t_ref, zr_ref, zi_ref, y_ref):
    r = f_ref.shape[1] // DFT_RADIX
    w = FOURIER_WIDTH
    pair = V7X_MXU_DIM
    slab = y_ref.shape[2]
    for lo in range(0, w, pair):
        for sub in range(lo, lo + pair, FOURIER_GROUP):
            lanes = slice(sub, sub + FOURIER_GROUP)
            x = [f_ref[0, n1 * r:(n1 + 1) * r, lanes].astype(F32) for n1 in range(DFT_RADIX)]
            re, im = _fold_radix8(x)
            for k1 in range(DFT_RADIX):
                t_ref[k1, 0:r, lanes] = re[k1].astype(BF16)
                t_ref[k1, r:2 * r, lanes] = (
                    jnp.zeros((r, FOURIER_GROUP), BF16) if im[k1] is None else im[k1].astype(BF16))
        h = lo // pair
        for k1 in range(DFT_RADIX):
            z = _dot(pos_ref[k1], t_ref[k1, :, lo:lo + pair])
            zr_ref[h, k1 * r:(k1 + 1) * r, :] = z[0:r].astype(BF16)
            zi_ref[h, k1 * r:(k1 + 1) * r, :] = z[r:2 * r].astype(BF16)
        y = _dot(zr_ref[h], chan_ref[0:pair, :]) + _dot(zi_ref[h], chan_ref[pair:2 * pair, :])
        for k1 in range(DFT_RADIX):
            for j in range(pair // slab):
                y_ref[(lo + j * slab) // slab, pl.ds(k1, r, stride=DFT_RADIX), :] = (
                    y[k1 * r:(k1 + 1) * r, j * slab:(j + 1) * slab])
    for j in range(w // slab):
        o_ref[0, :, j * slab:(j + 1) * slab] = y_ref[j].astype(BF16)


def _fourier(f):
    nb, n, w = f.shape
    assert w == FOURIER_WIDTH and n % (DFT_RADIX * V7X_MXU_DIM) == 0, f.shape
    pos, chan = _dft_tables(n)
    r = n // DFT_RADIX
    blk = pl.BlockSpec((1, n, w), lambda b: (b, 0, 0))
    lanes = V7X_LANES
    return pl.pallas_call(
        _fourier_kernel,
        out_shape=jax.ShapeDtypeStruct(f.shape, BF16),
        grid=(nb,),
        in_specs=[blk, _resident(pos.shape), _resident(chan.shape)],
        out_specs=blk,
        scratch_shapes=[pltpu.VMEM((DFT_RADIX, 2 * r, w), BF16),
                        pltpu.VMEM((w // V7X_MXU_DIM, n, V7X_MXU_DIM), BF16),
                        pltpu.VMEM((w // V7X_MXU_DIM, n, V7X_MXU_DIM), BF16),
                        pltpu.VMEM((w // lanes, n, lanes), F32)],
        compiler_params=_params("parallel"),
        name="fourier",
    )(f, pos, chan)


def _merge_kernel(x_ref, mod_ref, ya_ref, yf_ref, ga_ref, gf_ref, wab_ref, wfb_ref, wo_ref, o_ref,
                  merged_ref):
    gate = mod_ref[0, 5:6, :]
    for lo in range(0, D_MODEL, V7X_MXU_DIM):
        cols = slice(lo, lo + V7X_MXU_DIM)
        merged_ref[:, cols] = (ga_ref[0, :, cols].astype(F32) * _dot(ya_ref[0], wab_ref[:, cols])
                               + gf_ref[0, :, cols].astype(F32) * _dot(yf_ref[0], wfb_ref[:, cols])
                               ).astype(BF16)
    mix = _dot(merged_ref[...], wo_ref[...])
    o_ref[0] = x_ref[0] + gate * mix


def _merge(x, mod, y_attn, y_four, sig_a, sig_f, w_ab, w_fb, w_o):
    nb, t, _ = x.shape
    tm = WIDE_TOKEN_TILE
    assert t % tm == 0, x.shape
    tok = lambda w: pl.BlockSpec((1, tm, w), lambda b, i: (b, i, 0))
    return pl.pallas_call(
        _merge_kernel,
        out_shape=jax.ShapeDtypeStruct(x.shape, F32),
        grid=(nb, t // tm),
        in_specs=[
            tok(D_MODEL),
            pl.BlockSpec((1, N_MOD, D_MODEL), lambda b, i: (b, 0, 0)),
            tok(Q_WIDTH), tok(FOURIER_WIDTH), tok(D_MODEL), tok(D_MODEL),
            _resident(w_ab.shape), _resident(w_fb.shape), _resident(w_o.shape),
        ],
        out_specs=tok(D_MODEL),
        scratch_shapes=[pltpu.VMEM((tm, D_MODEL), BF16)],
        compiler_params=_params("parallel", "parallel"),
        name="merge",
    )(x, mod, y_attn, y_four, sig_a, sig_f, w_ab, w_fb, w_o)


def _rope_tables(n_tokens):
    f32 = np.float32
    rows = n_tokens // GRID_W
    row_ids = np.repeat(np.arange(rows, dtype=f32), GRID_W)
    col_ids = np.tile(np.arange(GRID_W, dtype=f32), rows)
    inv_freq = f32(ROPE_THETA) ** (-np.arange(0, AXIS_ROPE_DIM, 2, dtype=f32) / f32(AXIS_ROPE_DIM))
    ang = np.concatenate([row_ids[:, None] * inv_freq, col_ids[:, None] * inv_freq], axis=-1)
    cos, sin = np.cos(ang), np.sin(ang)
    assert ang.dtype == f32 and cos.dtype == f32
    return (jnp.asarray(np.concatenate([cos, cos], axis=-1)),
            jnp.asarray(np.concatenate([-sin, sin], axis=-1)))


def kernel(x, c, ctx, c_ctx, w_ada, b_ada, norm_ffn1, w_ffn1_in, w_ffn1_out, norm_mix, w_in,
           q_norm, k_norm, w_attn_branch, w_fourier_branch, w_out, norm_ffn2, w_ffn2_in, w_ffn2_out):
    nb, n_lat, _ = x.shape
    n_ctx = ctx.shape[1]
    depth = w_ada.shape[0]
    assert depth == 1, "the context stream is only carried through the (single) last layer"
    assert x.shape[2] == D_MODEL and ctx.shape == (nb, n_ctx, D_MODEL) and n_lat % GRID_W == 0
    assert w_in.shape[1:] == (D_MODEL, GF_OFF + D_MODEL) and w_ffn1_in.shape[1:] == (D_MODEL, 2 * D_FF)
    assert (nb * n_ctx) % TOKEN_TILE == 0, "context tokens are processed as one flat stream of tiles"
    cos2, sin2 = _rope_tables(n_lat)
    ctx_row = nb
    pad_rows = -(nb + 1) % V7X_SUBLANES
    c_all = jnp.concatenate([c, c_ctx[None, :], jnp.zeros((pad_rows, D_MODEL), F32)], axis=0)

    i = 0
    mod = _adaln(c_all, w_ada[i], b_ada[i][None, :]).reshape(c_all.shape[0], N_MOD, D_MODEL)
    own_row, shared_row = (lambda b: b), (lambda b: ctx_row)
    x, ctx_flat = _ffn([x, ctx.reshape(1, nb * n_ctx, D_MODEL)], mod, [own_row, shared_row], 0,
                       norm_ffn1[i][None, :], w_ffn1_in[i], w_ffn1_out[i])

    q, k, vx, f, sig_a, sig_f, k_c, vx_c = _inproj(
        x, ctx_flat, mod, [own_row, shared_row], norm_mix[i][None, :], w_in[i],
        q_norm[i][None, :], k_norm[i][None, :], cos2, sin2)
    y_attn = _attention(q, k, vx, k_c.reshape(nb, n_ctx, KV_WIDTH), vx_c.reshape(nb, n_ctx, VX_WIDTH))
    y_four = _fourier(f)
    x = _merge(x, mod, y_attn, y_four, sig_a, sig_f, w_attn_branch[i], w_fourier_branch[i], w_out[i])

    return _ffn([x], mod, [own_row], 6, norm_ffn2[i][None, :], w_ffn2_in[i], w_ffn2_out[i])[0]
```

```python
import functools

import numpy as np
import jax
import jax.numpy as jnp
from jax import lax
from jax.experimental import pallas as pl
from jax.experimental.pallas import tpu as pltpu

D_MODEL = 1024
GRID_W = 64
HEAD_DIM = 128
N_Q_HEADS = D_MODEL // HEAD_DIM
N_KV_HEADS = N_Q_HEADS // 4
GQA_GROUP = N_Q_HEADS // N_KV_HEADS
Q_WIDTH = N_Q_HEADS * HEAD_DIM
KV_WIDTH = N_KV_HEADS * HEAD_DIM
FOURIER_GROUP = 128
N_FOURIER_GROUPS = 4
FOURIER_WIDTH = N_FOURIER_GROUPS * FOURIER_GROUP
D_FF = 2816
AXIS_ROPE_DIM = HEAD_DIM // 2
ROPE_THETA = 10000.0
EPS = 1e-6
N_MOD = 9
ATTN_SCALE = HEAD_DIM ** -0.5
LOG2_E = 1.4426950408889634

K_OFF = Q_WIDTH
V_OFF = K_OFF + KV_WIDTH
F_OFF = V_OFF + KV_WIDTH
GA_OFF = F_OFF + FOURIER_WIDTH
GF_OFF = GA_OFF + D_MODEL

V7X_MXU_DIM = 256
V7X_LANES = 128
V7X_SUBLANES = 8
VX_WIDTH = N_KV_HEADS * V7X_MXU_DIM
V7X_VMEM_BYTES = 64 * 2**20
VMEM_LIMIT = V7X_VMEM_BYTES - 8 * 2**20

TOKEN_TILE = 512
WIDE_TOKEN_TILE = 1024
FF_CHUNK = V7X_MXU_DIM
ATTN_Q_TILE = 512
ADALN_TILE = 2304

F32 = jnp.float32
BF16 = jnp.bfloat16


def _params(*sem):
    return pltpu.CompilerParams(dimension_semantics=sem, vmem_limit_bytes=VMEM_LIMIT)


def _resident(shape):
    return pl.BlockSpec(shape, lambda *_: (0,) * len(shape), pipeline_mode=pl.Buffered(1))


def _rms(x, gain):
    return x * lax.rsqrt(jnp.mean(x * x, axis=-1, keepdims=True) + EPS) * gain


def _dot(a, b):
    return jnp.dot(a, b.astype(BF16), preferred_element_type=F32)


def _sigmoid(x):
    return 0.5 * jnp.tanh(0.5 * x) + 0.5


def _adaln_kernel(c_ref, w_ref, b_ref, o_ref):
    c = c_ref[...]
    a = (c * _sigmoid(c)).astype(BF16)
    o_ref[...] = _dot(a, w_ref[...]) + b_ref[...]


def _adaln(c_all, w, b):
    rows, n_out = c_all.shape[0], w.shape[1]
    assert rows % V7X_SUBLANES == 0 and n_out % ADALN_TILE == 0, (rows, n_out)
    return pl.pallas_call(
        _adaln_kernel,
        out_shape=jax.ShapeDtypeStruct((rows, n_out), F32),
        grid=(n_out // ADALN_TILE,),
        in_specs=[
            pl.BlockSpec((rows, D_MODEL), lambda j: (0, 0)),
            pl.BlockSpec((D_MODEL, ADALN_TILE), lambda j: (0, j)),
            pl.BlockSpec((1, ADALN_TILE), lambda j: (0, j)),
        ],
        out_specs=pl.BlockSpec((rows, ADALN_TILE), lambda j: (0, j)),
        compiler_params=_params("arbitrary"),
        name="adaln",
    )(c_all, w, b)


class _TokenStreams:
    def __init__(self, arrays, tile):
        assert all(a.ndim == 3 and a.shape[1] % tile == 0 for a in arrays), [a.shape for a in arrays]
        self.tile = tile
        self.per_batch = [a.shape[1] // tile for a in arrays]
        counts = [a.shape[0] * p for a, p in zip(arrays, self.per_batch)]
        self.starts = [sum(counts[:s]) for s in range(len(arrays) + 1)]
        self.steps = self.starts[-1]

    def batch_and_tile(self, s, t):
        local = jnp.clip(t - self.starts[s], 0, self.starts[s + 1] - self.starts[s] - 1)
        return local // self.per_batch[s], local % self.per_batch[s]

    def token_spec(self, s, width):
        def index(t):
            b, i = self.batch_and_tile(s, t)
            return b, i, 0
        return pl.BlockSpec((1, self.tile, width), index)

    def mod_spec(self, mod_rows):
        def index(t):
            row = mod_rows[0](self.batch_and_tile(0, t)[0])
            for s in range(1, len(mod_rows)):
                row = jnp.where(t >= self.starts[s], mod_rows[s](self.batch_and_tile(s, t)[0]), row)
            return row, 0, 0
        return pl.BlockSpec((1, N_MOD, D_MODEL), index)

    def active(self, s):
        t = pl.program_id(0)
        return jnp.logical_and(t >= self.starts[s], t < self.starts[s + 1])


def _ffn_kernel(*refs, streams, mod_base):
    n = len(streams.per_batch)
    x_refs, (mod_ref, gain_ref, win_ref, wout_ref), o_refs, act_ref = (
        refs[:n], refs[n:n + 4], refs[n + 4:2 * n + 4], refs[2 * n + 4])

    def half_step(x_ref, o_ref):
        x = x_ref[0]
        shift = mod_ref[0, mod_base:mod_base + 1, :]
        scale = mod_ref[0, mod_base + 1:mod_base + 2, :]
        gate = mod_ref[0, mod_base + 2:mod_base + 3, :]
        h = (_rms(x, gain_ref[...]) * (1.0 + scale) + shift).astype(BF16)
        for j in range(D_FF // FF_CHUNK):
            lo = j * FF_CHUNK
            g = _dot(h, win_ref[:, lo:lo + FF_CHUNK])
            u = _dot(h, win_ref[:, D_FF + lo:D_FF + lo + FF_CHUNK])
            act_ref[:, lo:lo + FF_CHUNK] = (g * _sigmoid(g) * u).astype(BF16)
        y = _dot(act_ref[...], wout_ref[...])
        o_ref[0] = x + 0.5 * gate * y

    if n == 1:
        half_step(x_refs[0], o_refs[0])
    else:
        for s in range(n):
            pl.when(streams.active(s))(functools.partial(half_step, x_refs[s], o_refs[s]))


def _ffn(xs, mod, mod_rows, mod_base, gain, w_in, w_out):
    tile = TOKEN_TILE
    streams = _TokenStreams(xs, tile)
    tok = [streams.token_spec(s, D_MODEL) for s in range(len(xs))]
    return pl.pallas_call(
        functools.partial(_ffn_kernel, streams=streams, mod_base=mod_base),
        out_shape=[jax.ShapeDtypeStruct(x.shape, F32) for x in xs],
        grid=(streams.steps,),
        in_specs=tok + [
            streams.mod_spec(mod_rows),
            _resident((1, D_MODEL)),
            _resident((D_MODEL, 2 * D_FF)),
            _resident((D_FF, D_MODEL)),
        ],
        out_specs=tok,
        scratch_shapes=[pltpu.VMEM((tile, D_FF), BF16)],
        compiler_params=_params("arbitrary"),
        name="ffn",
    )(*xs, mod, gain, w_in, w_out)


def _head_norm(t, gain):
    return t * lax.rsqrt(jnp.mean(t * t, axis=-1, keepdims=True) + EPS) * gain


def _rope(t, cos2, sin2):
    return t * cos2 + pltpu.roll(t, AXIS_ROPE_DIM, 1) * sin2


def _store_widened_values(v_ref, v):
    ones = jnp.ones((v.shape[0], V7X_MXU_DIM - HEAD_DIM), BF16)
    for hd in range(N_KV_HEADS):
        lo = hd * V7X_MXU_DIM
        v_ref[0, :, lo:lo + HEAD_DIM] = v[:, hd * HEAD_DIM:(hd + 1) * HEAD_DIM].astype(BF16)
        v_ref[0, :, lo + HEAD_DIM:lo + V7X_MXU_DIM] = ones


def _inproj_kernel(x_ref, ctx_ref, mod_ref, gain_ref, w_ref, qn_ref, kn_ref, cos_ref, sin_ref,
                   q_ref, k_ref, v_ref, f_ref, ga_ref, gf_ref, kc_ref, vc_ref, *, streams):
    kn = kn_ref[...]
    heads_per_dot = V7X_MXU_DIM // HEAD_DIM

    def normed_input(tok_ref):
        shift = mod_ref[0, 3:4, :]
        scale = mod_ref[0, 4:5, :]
        return (_rms(tok_ref[0], gain_ref[...]) * (1.0 + scale) + shift).astype(BF16)

    @pl.when(streams.active(0))
    def _latent_tokens():
        h = normed_input(x_ref)

        def wide_dot(col):
            return _dot(h, w_ref[:, col:col + V7X_MXU_DIM])

        cos2 = cos_ref[...]
        sin2 = sin_ref[...]
        qn = qn_ref[...] * (ATTN_SCALE * LOG2_E)
        for c in range(Q_WIDTH // V7X_MXU_DIM):
            t = wide_dot(c * V7X_MXU_DIM)
            for j in range(heads_per_dot):
                tj = t[:, j * HEAD_DIM:(j + 1) * HEAD_DIM]
                q_ref[0, c * heads_per_dot + j] = _rope(_head_norm(tj, qn), cos2, sin2).astype(BF16)
        for c in range(KV_WIDTH // V7X_MXU_DIM):
            t = wide_dot(K_OFF + c * V7X_MXU_DIM)
            for j in range(heads_per_dot):
                lo = c * V7X_MXU_DIM + j * HEAD_DIM
                tj = t[:, j * HEAD_DIM:(j + 1) * HEAD_DIM]
                k_ref[0, :, lo:lo + HEAD_DIM] = _rope(_head_norm(tj, kn), cos2, sin2).astype(BF16)
        _store_widened_values(v_ref, _dot(h, w_ref[:, V_OFF:V_OFF + KV_WIDTH]))
        for c in range(FOURIER_WIDTH // V7X_MXU_DIM):
            lo = c * V7X_MXU_DIM
            f_ref[0, :, lo:lo + V7X_MXU_DIM] = wide_dot(F_OFF + lo).astype(BF16)
        for c in range(D_MODEL // V7X_MXU_DIM):
            lo = c * V7X_MXU_DIM
            ga_ref[0, :, lo:lo + V7X_MXU_DIM] = _sigmoid(wide_dot(GA_OFF + lo)).astype(BF16)
            gf_ref[0, :, lo:lo + V7X_MXU_DIM] = _sigmoid(wide_dot(GF_OFF + lo)).astype(BF16)

    @pl.when(streams.active(1))
    def _context_tokens():
        h = normed_input(ctx_ref)
        t = _dot(h, w_ref[:, K_OFF:K_OFF + KV_WIDTH])
        for hd in range(N_KV_HEADS):
            lo = hd * HEAD_DIM
            kc_ref[0, :, lo:lo + HEAD_DIM] = _head_norm(t[:, lo:lo + HEAD_DIM], kn).astype(BF16)
        _store_widened_values(vc_ref, _dot(h, w_ref[:, V_OFF:V_OFF + KV_WIDTH]))


def _inproj(x, ctx, mod, mod_rows, gain, w_in, q_norm, k_norm, cos2, sin2):
    nb, t, _ = x.shape
    streams = _TokenStreams([x, ctx], TOKEN_TILE)
    tm = streams.tile
    tok = lambda w: streams.token_spec(0, w)
    ctx_tok = lambda w: streams.token_spec(1, w)
    shp = lambda w: jax.ShapeDtypeStruct((nb, t, w), BF16)
    ctx_shp = lambda w: jax.ShapeDtypeStruct(ctx.shape[:2] + (w,), BF16)

    def q_index(step):
        b, i = streams.batch_and_tile(0, step)
        return b, 0, i, 0

    def rope_index(step):
        return streams.batch_and_tile(0, step)[1], 0

    return pl.pallas_call(
        functools.partial(_inproj_kernel, streams=streams),
        out_shape=[jax.ShapeDtypeStruct((nb, N_Q_HEADS, t, HEAD_DIM), BF16), shp(KV_WIDTH),
                   shp(VX_WIDTH), shp(FOURIER_WIDTH), shp(D_MODEL), shp(D_MODEL),
                   ctx_shp(KV_WIDTH), ctx_shp(VX_WIDTH)],
        grid=(streams.steps,),
        in_specs=[
            tok(D_MODEL),
            ctx_tok(D_MODEL),
            streams.mod_spec(mod_rows),
            _resident((1, D_MODEL)),
            _resident(w_in.shape),
            _resident((1, HEAD_DIM)),
            _resident((1, HEAD_DIM)),
            pl.BlockSpec((tm, HEAD_DIM), rope_index),
            pl.BlockSpec((tm, HEAD_DIM), rope_index),
        ],
        out_specs=[pl.BlockSpec((1, N_Q_HEADS, tm, HEAD_DIM), q_index),
                   tok(KV_WIDTH), tok(VX_WIDTH), tok(FOURIER_WIDTH), tok(D_MODEL), tok(D_MODEL),
                   ctx_tok(KV_WIDTH), ctx_tok(VX_WIDTH)],
        compiler_params=_params("arbitrary"),
        name="inproj",
    )(x, ctx, mod, gain, w_in, q_norm, k_norm, cos2, sin2)


_NT = (((1,), (1,)), ((), ()))


ATTN_SCRATCH_PER_HEAD = 5


def _attn_kernel(q_ref, kc_ref, k_ref, vcx_ref, vx_ref, o_ref, *scratch):
    heads = [scratch[ATTN_SCRATCH_PER_HEAD * g:ATTN_SCRATCH_PER_HEAD * (g + 1)] for g in range(GQA_GROUP)]

    @pl.when(pl.program_id(0) == 0)
    def _no_previous_item():
        for pc_ref, pl_ref, sc_ref, sl_ref, m_ref in heads:
            pc_ref[...] = jnp.ones(pc_ref.shape, BF16)
            pl_ref[...] = jnp.ones(pl_ref.shape, BF16)
            sc_ref[...] = jnp.zeros(sc_ref.shape, F32)
            sl_ref[...] = jnp.zeros(sl_ref.shape, F32)
            m_ref[...] = jnp.zeros(m_ref.shape, F32)

    def weights_from_scores(pc_ref, pl_ref, sc_ref, sl_ref, m_ref):
        m = m_ref[...]
        for s_ref, p_ref in ((sc_ref, pc_ref), (sl_ref, pl_ref)):
            for lo in range(0, s_ref.shape[1], V7X_MXU_DIM):
                p_ref[:, lo:lo + V7X_MXU_DIM] = jnp.exp2(s_ref[:, lo:lo + V7X_MXU_DIM] - m).astype(BF16)

    weights_from_scores(*heads[-1])

    for g, (pc_ref, pl_ref, sc_ref, sl_ref, m_ref) in enumerate(heads):
        oe = _dot(pc_ref[...], vcx_ref[0]) + _dot(pl_ref[...], vx_ref[0])
        o_ref[0, :, g * HEAD_DIM:(g + 1) * HEAD_DIM] = (oe[:, :HEAD_DIM] / oe[:, HEAD_DIM:]).astype(BF16)

        q = q_ref[0, g]
        s_c = lax.dot_general(q, kc_ref[0], _NT, preferred_element_type=F32)
        s_l = lax.dot_general(q, k_ref[0], _NT, preferred_element_type=F32)
        m_ref[...] = jnp.maximum(jnp.max(s_c, axis=-1, keepdims=True), jnp.max(s_l, axis=-1, keepdims=True))
        sc_ref[...] = s_c
        sl_ref[...] = s_l

    for head in heads[:-1]:
        weights_from_scores(*head)


def _attention(q, k, vx, k_c, vx_c):
    nb, _, n, _ = q.shape
    n_ctx = k_c.shape[1]
    tq = ATTN_Q_TILE
    assert n % tq == 0 and n % V7X_MXU_DIM == 0 and n_ctx % V7X_MXU_DIM == 0, (n, n_ctx)
    tiles = n // tq
    items = nb * N_KV_HEADS * tiles

    def item(t):
        return t // (N_KV_HEADS * tiles), (t // tiles) % N_KV_HEADS, t % tiles

    def scored(t):
        return item(jnp.minimum(t, items - 1))

    def weighted(t):
        return item(jnp.maximum(t - 1, 0))

    def q_map(t):
        b, h, i = scored(t)
        return b, h, i, 0

    def k_map(t):
        b, h, _ = scored(t)
        return b, 0, h

    def v_map(t):
        b, h, _ = weighted(t)
        return b, 0, h

    def o_map(t):
        b, h, i = weighted(t)
        return b, i, h

    return pl.pallas_call(
        _attn_kernel,
        out_shape=jax.ShapeDtypeStruct((nb, n, Q_WIDTH), BF16),
        grid=(items + 1,),
        in_specs=[
            pl.BlockSpec((1, GQA_GROUP, tq, HEAD_DIM), q_map),
            pl.BlockSpec((1, n_ctx, HEAD_DIM), k_map),
            pl.BlockSpec((1, n, HEAD_DIM), k_map),
            pl.BlockSpec((1, n_ctx, V7X_MXU_DIM), v_map),
            pl.BlockSpec((1, n, V7X_MXU_DIM), v_map),
        ],
        out_specs=pl.BlockSpec((1, tq, GQA_GROUP * HEAD_DIM), o_map),
        scratch_shapes=GQA_GROUP * [
            pltpu.VMEM((tq, n_ctx), BF16), pltpu.VMEM((tq, n), BF16),
            pltpu.VMEM((tq, n_ctx), F32), pltpu.VMEM((tq, n), F32), pltpu.VMEM((tq, 1), F32)],
        compiler_params=_params("arbitrary"),
        name="attn",
    )(q, k_c, k, vx_c, vx)


DFT_RADIX = 8


def _dft_tables(n):
    r = n // DFT_RADIX
    k = np.arange(DFT_RADIX)[:, None, None] + DFT_RADIX * np.arange(r)[None, :, None]
    ang = 2.0 * np.pi * ((k * np.arange(r)[None, None, :]) % n) / n
    cos, sin = np.cos(ang), np.sin(ang)
    pos = np.concatenate([np.concatenate([cos, sin], axis=2),
                          np.concatenate([-sin, cos], axis=2)], axis=1)
    c = FOURIER_GROUP
    scale = 1.0 / np.sqrt(float(n * c))
    ang_c = 2.0 * np.pi * (np.outer(np.arange(c), np.arange(c)) % c) / c
    eye = np.eye(V7X_MXU_DIM // c)
    chan = np.concatenate([np.kron(eye, np.cos(ang_c) * scale),
                           np.kron(eye, np.sin(ang_c) * scale)], axis=0)
    return jnp.asarray(pos, F32).astype(BF16), jnp.asarray(chan, F32).astype(BF16)


def _fold_radix8(x):
    c = 0.5 ** 0.5
    e02p, e02m, e13p, e13m = x[0] + x[4], x[0] - x[4], x[2] + x[6], x[2] - x[6]
    o02p, o02m, o13p, o13m = x[1] + x[5], x[1] - x[5], x[3] + x[7], x[3] - x[7]
    e0, e2, o0, o2 = e02p + e13p, e02p - e13p, o02p + o13p, o02p - o13p
    p, q = c * (o02m - o13m), c * (o02m + o13m)
    re = [e0 + o0, e02m + p, e2, e02m - p, e0 - o0, e02m - p, e2, e02m + p]
    im1, im3 = -e13m - q, e13m - q
    im = [None, im1, -o2, im3, None, -im3, o2, -im1]
    return re, im


def _fourier_kernel(f_ref, pos_ref, chan_ref, o_ref, t_ref, zr_ref, zi_ref, y_ref):
    r = f_ref.shape[1] // DFT_RADIX
    w = FOURIER_WIDTH
    pair = V7X_MXU_DIM
    slab = y_ref.shape[2]
    for lo in range(0, w, pair):
        for sub in range(lo, lo + pair, FOURIER_GROUP):
            lanes = slice(sub, sub + FOURIER_GROUP)
            x = [f_ref[0, n1 * r:(n1 + 1) * r, lanes].astype(F32) for n1 in range(DFT_RADIX)]
            re, im = _fold_radix8(x)
            for k1 in range(DFT_RADIX):
                t_ref[k1, 0:r, lanes] = re[k1].astype(BF16)
                t_ref[k1, r:2 * r, lanes] = (
                    jnp.zeros((r, FOURIER_GROUP), BF16) if im[k1] is None else im[k1].astype(BF16))
        h = lo // pair
        for k1 in range(DFT_RADIX):
            z = _dot(pos_ref[k1], t_ref[k1, :, lo:lo + pair])
            zr_ref[h, k1 * r:(k1 + 1) * r, :] = z[0:r].astype(BF16)
            zi_ref[h, k1 * r:(k1 + 1) * r, :] = z[r:2 * r].astype(BF16)
        rows_per_dot = (DFT_RADIX // 2) * r
        for top in range(0, DFT_RADIX * r, rows_per_dot):
            rows = slice(top, top + rows_per_dot)
            y = (_dot(zr_ref[h, rows, :], chan_ref[0:pair, :])
                 + _dot(zi_ref[h, rows, :], chan_ref[pair:2 * pair, :]))
            for k1 in range(top // r, (top + rows_per_dot) // r):
                for j in range(pair // slab):
                    y_ref[(lo + j * slab) // slab, pl.ds(k1, r, stride=DFT_RADIX), :] = (
                        y[k1 * r - top:(k1 + 1) * r - top, j * slab:(j + 1) * slab])
    for j in range(w // slab):
        o_ref[0, :, j * slab:(j + 1) * slab] = y_ref[j].astype(BF16)


def _fourier(f):
    nb, n, w = f.shape
    assert w == FOURIER_WIDTH and n % (DFT_RADIX * V7X_MXU_DIM) == 0, f.shape
    pos, chan = _dft_tables(n)
    r = n // DFT_RADIX
    blk = pl.BlockSpec((1, n, w), lambda b: (b, 0, 0))
    lanes = V7X_LANES
    return pl.pallas_call(
        _fourier_kernel,
        out_shape=jax.ShapeDtypeStruct(f.shape, BF16),
        grid=(nb,),
        in_specs=[blk, _resident(pos.shape), _resident(chan.shape)],
        out_specs=blk,
        scratch_shapes=[pltpu.VMEM((DFT_RADIX, 2 * r, w), BF16),
                        pltpu.VMEM((w // V7X_MXU_DIM, n, V7X_MXU_DIM), BF16),
                        pltpu.VMEM((w // V7X_MXU_DIM, n, V7X_MXU_DIM), BF16),
                        pltpu.VMEM((w // lanes, n, lanes), F32)],
        compiler_params=_params("parallel"),
        name="fourier",
    )(f, pos, chan)


def _merge_kernel(x_ref, mod_ref, ya_ref, yf_ref, ga_ref, gf_ref, wab_ref, wfb_ref, wo_ref, o_ref,
                  merged_ref):
    gate = mod_ref[0, 5:6, :]
    for lo in range(0, D_MODEL, V7X_MXU_DIM):
        cols = slice(lo, lo + V7X_MXU_DIM)
        merged_ref[:, cols] = (ga_ref[0, :, cols].astype(F32) * _dot(ya_ref[0], wab_ref[:, cols])
                               + gf_ref[0, :, cols].astype(F32) * _dot(yf_ref[0], wfb_ref[:, cols])
                               ).astype(BF16)
    mix = _dot(merged_ref[...], wo_ref[...])
    o_ref[0] = x_ref[0] + gate * mix


def _merge(x, mod, y_attn, y_four, sig_a, sig_f, w_ab, w_fb, w_o):
    nb, t, _ = x.shape
    tm = WIDE_TOKEN_TILE
    assert t % tm == 0, x.shape
    tok = lambda w: pl.BlockSpec((1, tm, w), lambda b, i: (b, i, 0))
    return pl.pallas_call(
        _merge_kernel,
        out_shape=jax.ShapeDtypeStruct(x.shape, F32),
        grid=(nb, t // tm),
        in_specs=[
            tok(D_MODEL),
            pl.BlockSpec((1, N_MOD, D_MODEL), lambda b, i: (b, 0, 0)),
            tok(Q_WIDTH), tok(FOURIER_WIDTH), tok(D_MODEL), tok(D_MODEL),
            _resident(w_ab.shape), _resident(w_fb.shape), _resident(w_o.shape),
        ],
        out_specs=tok(D_MODEL),
        scratch_shapes=[pltpu.VMEM((tm, D_MODEL), BF16)],
        compiler_params=_params("parallel", "parallel"),
        name="merge",
    )(x, mod, y_attn, y_four, sig_a, sig_f, w_ab, w_fb, w_o)


def _rope_tables(n_tokens):
    f32 = np.float32
    rows = n_tokens // GRID_W
    row_ids = np.repeat(np.arange(rows, dtype=f32), GRID_W)
    col_ids = np.tile(np.arange(GRID_W, dtype=f32), rows)
    inv_freq = f32(ROPE_THETA) ** (-np.arange(0, AXIS_ROPE_DIM, 2, dtype=f32) / f32(AXIS_ROPE_DIM))
    ang = np.concatenate([row_ids[:, None] * inv_freq, col_ids[:, None] * inv_freq], axis=-1)
    cos, sin = np.cos(ang), np.sin(ang)
    assert ang.dtype == f32 and cos.dtype == f32
    return (jnp.asarray(np.concatenate([cos, cos], axis=-1)),
            jnp.asarray(np.concatenate([-sin, sin], axis=-1)))


def kernel(x, c, ctx, c_ctx, w_ada, b_ada, norm_ffn1, w_ffn1_in, w_ffn1_out, norm_mix, w_in,
           q_norm, k_norm, w_attn_branch, w_fourier_branch, w_out, norm_ffn2, w_ffn2_in, w_ffn2_out):
    nb, n_lat, _ = x.shape
    n_ctx = ctx.shape[1]
    depth = w_ada.shape[0]
    assert depth == 1, "the context stream is only carried through the (single) last layer"
    assert x.shape[2] == D_MODEL and ctx.shape == (nb, n_ctx, D_MODEL) and n_lat % GRID_W == 0
    assert w_in.shape[1:] == (D_MODEL, GF_OFF + D_MODEL) and w_ffn1_in.shape[1:] == (D_MODEL, 2 * D_FF)
    assert (nb * n_ctx) % TOKEN_TILE == 0, "context tokens are processed as one flat stream of tiles"
    cos2, sin2 = _rope_tables(n_lat)
    ctx_row = nb
    pad_rows = -(nb + 1) % V7X_SUBLANES
    c_all = jnp.concatenate([c, c_ctx[None, :], jnp.zeros((pad_rows, D_MODEL), F32)], axis=0)

    i = 0
    mod = _adaln(c_all, w_ada[i], b_ada[i][None, :]).reshape(c_all.shape[0], N_MOD, D_MODEL)
    own_row, shared_row = (lambda b: b), (lambda b: ctx_row)
    x, ctx_flat = _ffn([x, ctx.reshape(1, nb * n_ctx, D_MODEL)], mod, [own_row, shared_row], 0,
                       norm_ffn1[i][None, :], w_ffn1_in[i], w_ffn1_out[i])

    q, k, vx, f, sig_a, sig_f, k_c, vx_c = _inproj(
        x, ctx_flat, mod, [own_row, shared_row], norm_mix[i][None, :], w_in[i],
        q_norm[i][None, :], k_norm[i][None, :], cos2, sin2)
    y_attn = _attention(q, k, vx, k_c.reshape(nb, n_ctx, KV_WIDTH), vx_c.reshape(nb, n_ctx, VX_WIDTH))
    y_four = _fourier(f)
    x = _merge(x, mod, y_attn, y_four, sig_a, sig_f, w_attn_branch[i], w_fourier_branch[i], w_out[i])

    return _ffn([x], mod, [own_row], 6, norm_ffn2[i][None, :], w_ffn2_in[i], w_ffn2_out[i])[0]
```

```python
import functools

import numpy as np
import jax
import jax.numpy as jnp
from jax import lax
from jax.experimental import pallas as pl
from jax.experimental.pallas import tpu as pltpu

D_MODEL = 1024
GRID_W = 64
HEAD_DIM = 128
N_Q_HEADS = D_MODEL // HEAD_DIM
N_KV_HEADS = N_Q_HEADS // 4
GQA_GROUP = N_Q_HEADS // N_KV_HEADS
Q_WIDTH = N_Q_HEADS * HEAD_DIM
KV_WIDTH = N_KV_HEADS * HEAD_DIM
FOURIER_GROUP = 128
N_FOURIER_GROUPS = 4
FOURIER_WIDTH = N_FOURIER_GROUPS * FOURIER_GROUP
D_FF = 2816
AXIS_ROPE_DIM = HEAD_DIM // 2
ROPE_THETA = 10000.0
EPS = 1e-6
N_MOD = 9
ATTN_SCALE = HEAD_DIM ** -0.5
LOG2_E = 1.4426950408889634

K_OFF = Q_WIDTH
V_OFF = K_OFF + KV_WIDTH
F_OFF = V_OFF + KV_WIDTH
GA_OFF = F_OFF + FOURIER_WIDTH
GF_OFF = GA_OFF + D_MODEL

V7X_MXU_DIM = 256
V7X_LANES = 128
V7X_SUBLANES = 8
VX_WIDTH = N_KV_HEADS * V7X_MXU_DIM
V7X_VMEM_BYTES = 64 * 2**20
VMEM_LIMIT = V7X_VMEM_BYTES - 8 * 2**20

TOKEN_TILE = 512
WIDE_TOKEN_TILE = 1024
FF_CHUNK = V7X_MXU_DIM
ATTN_Q_TILE = 1024
ADALN_TILE = 2304

F32 = jnp.float32
BF16 = jnp.bfloat16


def _params(*sem):
    return pltpu.CompilerParams(dimension_semantics=sem, vmem_limit_bytes=VMEM_LIMIT)


def _resident(shape):
    return pl.BlockSpec(shape, lambda *_: (0,) * len(shape), pipeline_mode=pl.Buffered(1))


def _rms(x, gain):
    return x * lax.rsqrt(jnp.mean(x * x, axis=-1, keepdims=True) + EPS) * gain


def _dot(a, b):
    return jnp.dot(a, b.astype(BF16), preferred_element_type=F32)


def _sigmoid(x):
    return 0.5 * jnp.tanh(0.5 * x) + 0.5


def _adaln_kernel(c_ref, w_ref, b_ref, o_ref):
    c = c_ref[...]
    a = (c * _sigmoid(c)).astype(BF16)
    o_ref[...] = _dot(a, w_ref[...]) + b_ref[...]


def _adaln(c_all, w, b):
    rows, n_out = c_all.shape[0], w.shape[1]
    assert rows % V7X_SUBLANES == 0 and n_out % ADALN_TILE == 0, (rows, n_out)
    return pl.pallas_call(
        _adaln_kernel,
        out_shape=jax.ShapeDtypeStruct((rows, n_out), F32),
        grid=(n_out // ADALN_TILE,),
        in_specs=[
            pl.BlockSpec((rows, D_MODEL), lambda j: (0, 0)),
            pl.BlockSpec((D_MODEL, ADALN_TILE), lambda j: (0, j)),
            pl.BlockSpec((1, ADALN_TILE), lambda j: (0, j)),
        ],
        out_specs=pl.BlockSpec((rows, ADALN_TILE), lambda j: (0, j)),
        compiler_params=_params("arbitrary"),
        name="adaln",
    )(c_all, w, b)


class _TokenStreams:
    def __init__(self, arrays, tile):
        assert all(a.ndim == 3 and a.shape[1] % tile == 0 for a in arrays), [a.shape for a in arrays]
        self.tile = tile
        self.per_batch = [a.shape[1] // tile for a in arrays]
        counts = [a.shape[0] * p for a, p in zip(arrays, self.per_batch)]
        self.starts = [sum(counts[:s]) for s in range(len(arrays) + 1)]
        self.steps = self.starts[-1]

    def batch_and_tile(self, s, t):
        local = jnp.clip(t - self.starts[s], 0, self.starts[s + 1] - self.starts[s] - 1)
        return local // self.per_batch[s], local % self.per_batch[s]

    def token_spec(self, s, width):
        def index(t):
            b, i = self.batch_and_tile(s, t)
            return b, i, 0
        return pl.BlockSpec((1, self.tile, width), index)

    def mod_spec(self, mod_rows):
        def index(t):
            row = mod_rows[0](self.batch_and_tile(0, t)[0])
            for s in range(1, len(mod_rows)):
                row = jnp.where(t >= self.starts[s], mod_rows[s](self.batch_and_tile(s, t)[0]), row)
            return row, 0, 0
        return pl.BlockSpec((1, N_MOD, D_MODEL), index)

    def active(self, s):
        t = pl.program_id(0)
        return jnp.logical_and(t >= self.starts[s], t < self.starts[s + 1])


def _ffn_kernel(*refs, streams, mod_base):
    n = len(streams.per_batch)
    x_refs, (mod_ref, gain_ref, win_ref, wout_ref), o_refs, act_ref = (
        refs[:n], refs[n:n + 4], refs[n + 4:2 * n + 4], refs[2 * n + 4])

    def half_step(x_ref, o_ref):
        x = x_ref[0]
        shift = mod_ref[0, mod_base:mod_base + 1, :]
        scale = mod_ref[0, mod_base + 1:mod_base + 2, :]
        gate = mod_ref[0, mod_base + 2:mod_base + 3, :]
        h = (_rms(x, gain_ref[...]) * (1.0 + scale) + shift).astype(BF16)
        for j in range(D_FF // FF_CHUNK):
            lo = j * FF_CHUNK
            g = _dot(h, win_ref[:, lo:lo + FF_CHUNK])
            u = _dot(h, win_ref[:, D_FF + lo:D_FF + lo + FF_CHUNK])
            act_ref[:, lo:lo + FF_CHUNK] = (g * _sigmoid(g) * u).astype(BF16)
        y = _dot(act_ref[...], wout_ref[...])
        o_ref[0] = x + 0.5 * gate * y

    if n == 1:
        half_step(x_refs[0], o_refs[0])
    else:
        for s in range(n):
            pl.when(streams.active(s))(functools.partial(half_step, x_refs[s], o_refs[s]))


def _ffn(xs, mod, mod_rows, mod_base, gain, w_in, w_out):
    tile = TOKEN_TILE
    streams = _TokenStreams(xs, tile)
    tok = [streams.token_spec(s, D_MODEL) for s in range(len(xs))]
    return pl.pallas_call(
        functools.partial(_ffn_kernel, streams=streams, mod_base=mod_base),
        out_shape=[jax.ShapeDtypeStruct(x.shape, F32) for x in xs],
        grid=(streams.steps,),
        in_specs=tok + [
            streams.mod_spec(mod_rows),
            _resident((1, D_MODEL)),
            _resident((D_MODEL, 2 * D_FF)),
            _resident((D_FF, D_MODEL)),
        ],
        out_specs=tok,
        scratch_shapes=[pltpu.VMEM((tile, D_FF), BF16)],
        compiler_params=_params("arbitrary"),
        name="ffn",
    )(*xs, mod, gain, w_in, w_out)


def _head_norm(t, gain):
    return t * lax.rsqrt(jnp.mean(t * t, axis=-1, keepdims=True) + EPS) * gain


def _rope(t, cos2, sin2):
    return t * cos2 + pltpu.roll(t, AXIS_ROPE_DIM, 1) * sin2


def _store_widened_values(v_ref, v):
    ones = jnp.ones((v.shape[0], V7X_MXU_DIM - HEAD_DIM), BF16)
    for hd in range(N_KV_HEADS):
        lo = hd * V7X_MXU_DIM
        v_ref[0, :, lo:lo + HEAD_DIM] = v[:, hd * HEAD_DIM:(hd + 1) * HEAD_DIM].astype(BF16)
        v_ref[0, :, lo + HEAD_DIM:lo + V7X_MXU_DIM] = ones


def _inproj_kernel(x_ref, ctx_ref, mod_ref, gain_ref, w_ref, qn_ref, kn_ref, cos_ref, sin_ref,
                   q_ref, k_ref, v_ref, f_ref, ga_ref, gf_ref, kc_ref, vc_ref, *, streams):
    kn = kn_ref[...]
    heads_per_dot = V7X_MXU_DIM // HEAD_DIM

    def normed_input(tok_ref):
        shift = mod_ref[0, 3:4, :]
        scale = mod_ref[0, 4:5, :]
        return (_rms(tok_ref[0], gain_ref[...]) * (1.0 + scale) + shift).astype(BF16)

    @pl.when(streams.active(0))
    def _latent_tokens():
        h = normed_input(x_ref)

        def wide_dot(col):
            return _dot(h, w_ref[:, col:col + V7X_MXU_DIM])

        cos2 = cos_ref[...]
        sin2 = sin_ref[...]
        qn = qn_ref[...] * (ATTN_SCALE * LOG2_E)
        for c in range(Q_WIDTH // V7X_MXU_DIM):
            t = wide_dot(c * V7X_MXU_DIM)
            for j in range(heads_per_dot):
                tj = t[:, j * HEAD_DIM:(j + 1) * HEAD_DIM]
                q_ref[0, c * heads_per_dot + j] = _rope(_head_norm(tj, qn), cos2, sin2).astype(BF16)
        for c in range(KV_WIDTH // V7X_MXU_DIM):
            t = wide_dot(K_OFF + c * V7X_MXU_DIM)
            for j in range(heads_per_dot):
                lo = c * V7X_MXU_DIM + j * HEAD_DIM
                tj = t[:, j * HEAD_DIM:(j + 1) * HEAD_DIM]
                k_ref[0, :, lo:lo + HEAD_DIM] = _rope(_head_norm(tj, kn), cos2, sin2).astype(BF16)
        _store_widened_values(v_ref, _dot(h, w_ref[:, V_OFF:V_OFF + KV_WIDTH]))
        for c in range(FOURIER_WIDTH // V7X_MXU_DIM):
            lo = c * V7X_MXU_DIM
            f_ref[0, :, lo:lo + V7X_MXU_DIM] = wide_dot(F_OFF + lo).astype(BF16)
        for c in range(D_MODEL // V7X_MXU_DIM):
            lo = c * V7X_MXU_DIM
            ga_ref[0, :, lo:lo + V7X_MXU_DIM] = _sigmoid(wide_dot(GA_OFF + lo)).astype(BF16)
            gf_ref[0, :, lo:lo + V7X_MXU_DIM] = _sigmoid(wide_dot(GF_OFF + lo)).astype(BF16)

    @pl.when(streams.active(1))
    def _context_tokens():
        h = normed_input(ctx_ref)
        t = _dot(h, w_ref[:, K_OFF:K_OFF + KV_WIDTH])
        for hd in range(N_KV_HEADS):
            lo = hd * HEAD_DIM
            kc_ref[0, :, lo:lo + HEAD_DIM] = _head_norm(t[:, lo:lo + HEAD_DIM], kn).astype(BF16)
        _store_widened_values(vc_ref, _dot(h, w_ref[:, V_OFF:V_OFF + KV_WIDTH]))


def _inproj(x, ctx, mod, mod_rows, gain, w_in, q_norm, k_norm, cos2, sin2):
    nb, t, _ = x.shape
    streams = _TokenStreams([x, ctx], TOKEN_TILE)
    tm = streams.tile
    tok = lambda w: streams.token_spec(0, w)
    ctx_tok = lambda w: streams.token_spec(1, w)
    shp = lambda w: jax.ShapeDtypeStruct((nb, t, w), BF16)
    ctx_shp = lambda w: jax.ShapeDtypeStruct(ctx.shape[:2] + (w,), BF16)

    def q_index(step):
        b, i = streams.batch_and_tile(0, step)
        return b, 0, i, 0

    def rope_index(step):
        return streams.batch_and_tile(0, step)[1], 0

    return pl.pallas_call(
        functools.partial(_inproj_kernel, streams=streams),
        out_shape=[jax.ShapeDtypeStruct((nb, N_Q_HEADS, t, HEAD_DIM), BF16), shp(KV_WIDTH),
                   shp(VX_WIDTH), shp(FOURIER_WIDTH), shp(D_MODEL), shp(D_MODEL),
                   ctx_shp(KV_WIDTH), ctx_shp(VX_WIDTH)],
        grid=(streams.steps,),
        in_specs=[
            tok(D_MODEL),
            ctx_tok(D_MODEL),
            streams.mod_spec(mod_rows),
            _resident((1, D_MODEL)),
            _resident(w_in.shape),
            _resident((1, HEAD_DIM)),
            _resident((1, HEAD_DIM)),
            pl.BlockSpec((tm, HEAD_DIM), rope_index),
            pl.BlockSpec((tm, HEAD_DIM), rope_index),
        ],
        out_specs=[pl.BlockSpec((1, N_Q_HEADS, tm, HEAD_DIM), q_index),
                   tok(KV_WIDTH), tok(VX_WIDTH), tok(FOURIER_WIDTH), tok(D_MODEL), tok(D_MODEL),
                   ctx_tok(KV_WIDTH), ctx_tok(VX_WIDTH)],
        compiler_params=_params("arbitrary"),
        name="inproj",
    )(x, ctx, mod, gain, w_in, q_norm, k_norm, cos2, sin2)


_NT = (((1,), (1,)), ((), ()))


ATTN_SCRATCH_PER_HEAD = 3
ATTN_SCORE_BUFFERS = 3


def _attn_kernel(q_ref, kc_ref, k_ref, vcx_ref, vx_ref, o_ref, *scratch):
    n_head_refs = ATTN_SCRATCH_PER_HEAD * GQA_GROUP
    heads = [scratch[ATTN_SCRATCH_PER_HEAD * g:ATTN_SCRATCH_PER_HEAD * (g + 1)] for g in range(GQA_GROUP)]
    scores = [scratch[n_head_refs + 2 * b:n_head_refs + 2 * b + 2] for b in range(ATTN_SCORE_BUFFERS)]
    score_refs = lambda g: scores[g % ATTN_SCORE_BUFFERS]

    @pl.when(pl.program_id(0) == 0)
    def _no_previous_item():
        for pc_ref, pl_ref, m_ref in heads:
            pc_ref[...] = jnp.ones(pc_ref.shape, BF16)
            pl_ref[...] = jnp.ones(pl_ref.shape, BF16)
            m_ref[...] = jnp.zeros(m_ref.shape, F32)
        for sc_ref, sl_ref in scores:
            sc_ref[...] = jnp.zeros(sc_ref.shape, F32)
            sl_ref[...] = jnp.zeros(sl_ref.shape, F32)

    def weights_from_scores(g):
        pc_ref, pl_ref, m_ref = heads[g]
        m = m_ref[...]
        for s_ref, p_ref in zip(score_refs(g), (pc_ref, pl_ref)):
            for lo in range(0, s_ref.shape[1], V7X_MXU_DIM):
                p_ref[:, lo:lo + V7X_MXU_DIM] = jnp.exp2(s_ref[:, lo:lo + V7X_MXU_DIM] - m).astype(BF16)

    def values_then_scores(g):
        pc_ref, pl_ref, m_ref = heads[g]
        sc_ref, sl_ref = score_refs(g)
        oe = _dot(pc_ref[...], vcx_ref[0]) + _dot(pl_ref[...], vx_ref[0])
        o_ref[0, :, g * HEAD_DIM:(g + 1) * HEAD_DIM] = (oe[:, :HEAD_DIM] / oe[:, HEAD_DIM:]).astype(BF16)

        q = q_ref[0, g]
        s_c = lax.dot_general(q, kc_ref[0], _NT, preferred_element_type=F32)
        s_l = lax.dot_general(q, k_ref[0], _NT, preferred_element_type=F32)
        m_ref[...] = jnp.maximum(jnp.max(s_c, axis=-1, keepdims=True), jnp.max(s_l, axis=-1, keepdims=True))
        sc_ref[...] = s_c
        sl_ref[...] = s_l

    last = GQA_GROUP - 1
    weights_from_scores(last)
    for g in range(GQA_GROUP):
        if g >= ATTN_SCORE_BUFFERS:
            weights_from_scores(g - ATTN_SCORE_BUFFERS)
        values_then_scores(g)
    for g in range(GQA_GROUP - ATTN_SCORE_BUFFERS, last):
        weights_from_scores(g)


def _attention(q, k, vx, k_c, vx_c):
    nb, _, n, _ = q.shape
    n_ctx = k_c.shape[1]
    tq = ATTN_Q_TILE
    assert n % tq == 0 and n % V7X_MXU_DIM == 0 and n_ctx % V7X_MXU_DIM == 0, (n, n_ctx)
    tiles = n // tq
    items = nb * N_KV_HEADS * tiles

    def item(t):
        return t // (N_KV_HEADS * tiles), (t // tiles) % N_KV_HEADS, t % tiles

    def scored(t):
        return item(jnp.minimum(t, items - 1))

    def weighted(t):
        return item(jnp.maximum(t - 1, 0))

    def q_map(t):
        b, h, i = scored(t)
        return b, h, i, 0

    def k_map(t):
        b, h, _ = scored(t)
        return b, 0, h

    def v_map(t):
        b, h, _ = weighted(t)
        return b, 0, h

    def o_map(t):
        b, h, i = weighted(t)
        return b, i, h

    return pl.pallas_call(
        _attn_kernel,
        out_shape=jax.ShapeDtypeStruct((nb, n, Q_WIDTH), BF16),
        grid=(items + 1,),
        in_specs=[
            pl.BlockSpec((1, GQA_GROUP, tq, HEAD_DIM), q_map),
            pl.BlockSpec((1, n_ctx, HEAD_DIM), k_map),
            pl.BlockSpec((1, n, HEAD_DIM), k_map),
            pl.BlockSpec((1, n_ctx, V7X_MXU_DIM), v_map),
            pl.BlockSpec((1, n, V7X_MXU_DIM), v_map),
        ],
        out_specs=pl.BlockSpec((1, tq, GQA_GROUP * HEAD_DIM), o_map),
        scratch_shapes=(
            GQA_GROUP * [pltpu.VMEM((tq, n_ctx), BF16), pltpu.VMEM((tq, n), BF16), pltpu.VMEM((tq, 1), F32)]
            + ATTN_SCORE_BUFFERS * [pltpu.VMEM((tq, n_ctx), F32), pltpu.VMEM((tq, n), F32)]),
        compiler_params=pltpu.CompilerParams(dimension_semantics=("arbitrary",),
                                             vmem_limit_bytes=V7X_VMEM_BYTES - 3 * 2**20),
        name="attn",
    )(q, k_c, k, vx_c, vx)


DFT_RADIX = 8


def _dft_tables(n):
    r = n // DFT_RADIX
    k = np.arange(DFT_RADIX)[:, None, None] + DFT_RADIX * np.arange(r)[None, :, None]
    ang = 2.0 * np.pi * ((k * np.arange(r)[None, None, :]) % n) / n
    cos, sin = np.cos(ang), np.sin(ang)
    pos = np.concatenate([np.concatenate([cos, sin], axis=2),
                          np.concatenate([-sin, cos], axis=2)], axis=1)
    c = FOURIER_GROUP
    scale = 1.0 / np.sqrt(float(n * c))
    ang_c = 2.0 * np.pi * (np.outer(np.arange(c), np.arange(c)) % c) / c
    eye = np.eye(V7X_MXU_DIM // c)
    chan = np.concatenate([np.kron(eye, np.cos(ang_c) * scale),
                           np.kron(eye, np.sin(ang_c) * scale)], axis=0)
    return jnp.asarray(pos, F32).astype(BF16), jnp.asarray(chan, F32).astype(BF16)


def _fold_radix8(x):
    c = 0.5 ** 0.5
    e02p, e02m, e13p, e13m = x[0] + x[4], x[0] - x[4], x[2] + x[6], x[2] - x[6]
    o02p, o02m, o13p, o13m = x[1] + x[5], x[1] - x[5], x[3] + x[7], x[3] - x[7]
    e0, e2, o0, o2 = e02p + e13p, e02p - e13p, o02p + o13p, o02p - o13p
    p, q = c * (o02m - o13m), c * (o02m + o13m)
    re = [e0 + o0, e02m + p, e2, e02m - p, e0 - o0, e02m - p, e2, e02m + p]
    im1, im3 = -e13m - q, e13m - q
    im = [None, im1, -o2, im3, None, -im3, o2, -im1]
    return re, im


def _fourier_kernel(f_ref, pos_ref, chan_ref, o_ref, t_ref, zr_ref, zi_ref, y_ref):
    r = f_ref.shape[1] // DFT_RADIX
    w = FOURIER_WIDTH
    pair = V7X_MXU_DIM
    slab = y_ref.shape[2]
    for lo in range(0, w, pair):
        for sub in range(lo, lo + pair, FOURIER_GROUP):
            lanes = slice(sub, sub + FOURIER_GROUP)
            x = [f_ref[0, n1 * r:(n1 + 1) * r, lanes].astype(F32) for n1 in range(DFT_RADIX)]
            re, im = _fold_radix8(x)
            for k1 in range(DFT_RADIX):
                t_ref[k1, 0:r, lanes] = re[k1].astype(BF16)
                t_ref[k1, r:2 * r, lanes] = (
                    jnp.zeros((r, FOURIER_GROUP), BF16) if im[k1] is None else im[k1].astype(BF16))
        h = lo // pair
        for k1 in range(DFT_RADIX):
            z = _dot(pos_ref[k1], t_ref[k1, :, lo:lo + pair])
            zr_ref[h, k1 * r:(k1 + 1) * r, :] = z[0:r].astype(BF16)
            zi_ref[h, k1 * r:(k1 + 1) * r, :] = z[r:2 * r].astype(BF16)
        rows_per_dot = (DFT_RADIX // 2) * r
        for top in range(0, DFT_RADIX * r, rows_per_dot):
            rows = slice(top, top + rows_per_dot)
            y = (_dot(zr_ref[h, rows, :], chan_ref[0:pair, :])
                 + _dot(zi_ref[h, rows, :], chan_ref[pair:2 * pair, :]))
            for k1 in range(top // r, (top + rows_per_dot) // r):
                for j in range(pair // slab):
                    y_ref[(lo + j * slab) // slab, pl.ds(k1, r, stride=DFT_RADIX), :] = (
                        y[k1 * r - top:(k1 + 1) * r - top, j * slab:(j + 1) * slab])
    for j in range(w // slab):
        o_ref[0, :, j * slab:(j + 1) * slab] = y_ref[j].astype(BF16)


def _fourier(f):
    nb, n, w = f.shape
    assert w == FOURIER_WIDTH and n % (DFT_RADIX * V7X_MXU_DIM) == 0, f.shape
    pos, chan = _dft_tables(n)
    r = n // DFT_RADIX
    blk = pl.BlockSpec((1, n, w), lambda b: (b, 0, 0))
    lanes = V7X_LANES
    return pl.pallas_call(
        _fourier_kernel,
        out_shape=jax.ShapeDtypeStruct(f.shape, BF16),
        grid=(nb,),
        in_specs=[blk, _resident(pos.shape), _resident(chan.shape)],
        out_specs=blk,
        scratch_shapes=[pltpu.VMEM((DFT_RADIX, 2 * r, w), BF16),
                        pltpu.VMEM((w // V7X_MXU_DIM, n, V7X_MXU_DIM), BF16),
                        pltpu.VMEM((w // V7X_MXU_DIM, n, V7X_MXU_DIM), BF16),
                        pltpu.VMEM((w // lanes, n, lanes), F32)],
        compiler_params=_params("parallel"),
        name="fourier",
    )(f, pos, chan)


def _merge_kernel(x_ref, mod_ref, ya_ref, yf_ref, ga_ref, gf_ref, wab_ref, wfb_ref, wo_ref, o_ref,
                  merged_ref):
    gate = mod_ref[0, 5:6, :]
    for lo in range(0, D_MODEL, V7X_MXU_DIM):
        cols = slice(lo, lo + V7X_MXU_DIM)
        merged_ref[:, cols] = (ga_ref[0, :, cols].astype(F32) * _dot(ya_ref[0], wab_ref[:, cols])
                               + gf_ref[0, :, cols].astype(F32) * _dot(yf_ref[0], wfb_ref[:, cols])
                               ).astype(BF16)
    mix = _dot(merged_ref[...], wo_ref[...])
    o_ref[0] = x_ref[0] + gate * mix


def _merge(x, mod, y_attn, y_four, sig_a, sig_f, w_ab, w_fb, w_o):
    nb, t, _ = x.shape
    tm = WIDE_TOKEN_TILE
    assert t % tm == 0, x.shape
    tok = lambda w: pl.BlockSpec((1, tm, w), lambda b, i: (b, i, 0))
    return pl.pallas_call(
        _merge_kernel,
        out_shape=jax.ShapeDtypeStruct(x.shape, F32),
        grid=(nb, t // tm),
        in_specs=[
            tok(D_MODEL),
            pl.BlockSpec((1, N_MOD, D_MODEL), lambda b, i: (b, 0, 0)),
            tok(Q_WIDTH), tok(FOURIER_WIDTH), tok(D_MODEL), tok(D_MODEL),
            _resident(w_ab.shape), _resident(w_fb.shape), _resident(w_o.shape),
        ],
        out_specs=tok(D_MODEL),
        scratch_shapes=[pltpu.VMEM((tm, D_MODEL), BF16)],
        compiler_params=_params("parallel", "parallel"),
        name="merge",
    )(x, mod, y_attn, y_four, sig_a, sig_f, w_ab, w_fb, w_o)


def _rope_tables(n_tokens):
    f32 = np.float32
    rows = n_tokens // GRID_W
    row_ids = np.repeat(np.arange(rows, dtype=f32), GRID_W)
    col_ids = np.tile(np.arange(GRID_W, dtype=f32), rows)
    inv_freq = f32(ROPE_THETA) ** (-np.arange(0, AXIS_ROPE_DIM, 2, dtype=f32) / f32(AXIS_ROPE_DIM))
    ang = np.concatenate([row_ids[:, None] * inv_freq, col_ids[:, None] * inv_freq], axis=-1)
    cos, sin = np.cos(ang), np.sin(ang)
    assert ang.dtype == f32 and cos.dtype == f32
    return (jnp.asarray(np.concatenate([cos, cos], axis=-1)),
            jnp.asarray(np.concatenate([-sin, sin], axis=-1)))


def kernel(x, c, ctx, c_ctx, w_ada, b_ada, norm_ffn1, w_ffn1_in, w_ffn1_out, norm_mix, w_in,
           q_norm, k_norm, w_attn_branch, w_fourier_branch, w_out, norm_ffn2, w_ffn2_in, w_ffn2_out):
    nb, n_lat, _ = x.shape
    n_ctx = ctx.shape[1]
    depth = w_ada.shape[0]
    assert depth == 1, "the context stream is only carried through the (single) last layer"
    assert x.shape[2] == D_MODEL and ctx.shape == (nb, n_ctx, D_MODEL) and n_lat % GRID_W == 0
    assert w_in.shape[1:] == (D_MODEL, GF_OFF + D_MODEL) and w_ffn1_in.shape[1:] == (D_MODEL, 2 * D_FF)
    assert (nb * n_ctx) % TOKEN_TILE == 0, "context tokens are processed as one flat stream of tiles"
    cos2, sin2 = _rope_tables(n_lat)
    ctx_row = nb
    pad_rows = -(nb + 1) % V7X_SUBLANES
    c_all = jnp.concatenate([c, c_ctx[None, :], jnp.zeros((pad_rows, D_MODEL), F32)], axis=0)

    i = 0
    mod = _adaln(c_all, w_ada[i], b_ada[i][None, :]).reshape(c_all.shape[0], N_MOD, D_MODEL)
    own_row, shared_row = (lambda b: b), (lambda b: ctx_row)
    x, ctx_flat = _ffn([x, ctx.reshape(1, nb * n_ctx, D_MODEL)], mod, [own_row, shared_row], 0,
                       norm_ffn1[i][None, :], w_ffn1_in[i], w_ffn1_out[i])

    q, k, vx, f, sig_a, sig_f, k_c, vx_c = _inproj(
        x, ctx_flat, mod, [own_row, shared_row], norm_mix[i][None, :], w_in[i],
        q_norm[i][None, :], k_norm[i][None, :], cos2, sin2)
    y_attn = _attention(q, k, vx, k_c.reshape(nb, n_ctx, KV_WIDTH), vx_c.reshape(nb, n_ctx, VX_WIDTH))
    y_four = _fourier(f)
    x = _merge(x, mod, y_attn, y_four, sig_a, sig_f, w_attn_branch[i], w_fourier_branch[i], w_out[i])

    return _ffn([x], mod, [own_row], 6, norm_ffn2[i][None, :], w_ffn2_in[i], w_ffn2_out[i])[0]
```

```python
import functools

import numpy as np
import jax
import jax.numpy as jnp
from jax import lax
from jax.experimental import pallas as pl
from jax.experimental.pallas import tpu as pltpu

D_MODEL = 1024
GRID_W = 64
HEAD_DIM = 128
N_Q_HEADS = D_MODEL // HEAD_DIM
N_KV_HEADS = N_Q_HEADS // 4
GQA_GROUP = N_Q_HEADS // N_KV_HEADS
Q_WIDTH = N_Q_HEADS * HEAD_DIM
KV_WIDTH = N_KV_HEADS * HEAD_DIM
FOURIER_GROUP = 128
N_FOURIER_GROUPS = 4
FOURIER_WIDTH = N_FOURIER_GROUPS * FOURIER_GROUP
D_FF = 2816
AXIS_ROPE_DIM = HEAD_DIM // 2
ROPE_THETA = 10000.0
EPS = 1e-6
N_MOD = 9
ATTN_SCALE = HEAD_DIM ** -0.5
LOG2_E = 1.4426950408889634

K_OFF = Q_WIDTH
V_OFF = K_OFF + KV_WIDTH
F_OFF = V_OFF + KV_WIDTH
GA_OFF = F_OFF + FOURIER_WIDTH
GF_OFF = GA_OFF + D_MODEL

V7X_MXU_DIM = 256
V7X_LANES = 128
V7X_SUBLANES = 8
VX_WIDTH = N_KV_HEADS * V7X_MXU_DIM
V7X_VMEM_BYTES = 64 * 2**20
VMEM_LIMIT = V7X_VMEM_BYTES - 8 * 2**20

TOKEN_TILE = 512
WIDE_TOKEN_TILE = 1024
FF_CHUNK = V7X_MXU_DIM
ATTN_Q_TILE = 1024
ADALN_TILE = 2304

F32 = jnp.float32
BF16 = jnp.bfloat16


def _params(*sem):
    return pltpu.CompilerParams(dimension_semantics=sem, vmem_limit_bytes=VMEM_LIMIT)


def _resident(shape):
    return pl.BlockSpec(shape, lambda *_: (0,) * len(shape), pipeline_mode=pl.Buffered(1))


def _rms(x, gain):
    return x * lax.rsqrt(jnp.mean(x * x, axis=-1, keepdims=True) + EPS) * gain


def _dot(a, b):
    return jnp.dot(a, b.astype(BF16), preferred_element_type=F32)


def _sigmoid(x):
    return 0.5 * jnp.tanh(0.5 * x) + 0.5


def _adaln_kernel(c_ref, w_ref, b_ref, o_ref):
    c = c_ref[...]
    a = (c * _sigmoid(c)).astype(BF16)
    o_ref[...] = _dot(a, w_ref[...]) + b_ref[...]


def _adaln(c_all, w, b):
    rows, n_out = c_all.shape[0], w.shape[1]
    assert rows % V7X_SUBLANES == 0 and n_out % ADALN_TILE == 0, (rows, n_out)
    return pl.pallas_call(
        _adaln_kernel,
        out_shape=jax.ShapeDtypeStruct((rows, n_out), F32),
        grid=(n_out // ADALN_TILE,),
        in_specs=[
            pl.BlockSpec((rows, D_MODEL), lambda j: (0, 0)),
            pl.BlockSpec((D_MODEL, ADALN_TILE), lambda j: (0, j)),
            pl.BlockSpec((1, ADALN_TILE), lambda j: (0, j)),
        ],
        out_specs=pl.BlockSpec((rows, ADALN_TILE), lambda j: (0, j)),
        compiler_params=_params("arbitrary"),
        name="adaln",
    )(c_all, w, b)


class _TokenStreams:
    def __init__(self, arrays, tile):
        assert all(a.ndim == 3 and a.shape[1] % tile == 0 for a in arrays), [a.shape for a in arrays]
        self.tile = tile
        self.per_batch = [a.shape[1] // tile for a in arrays]
        counts = [a.shape[0] * p for a, p in zip(arrays, self.per_batch)]
        self.starts = [sum(counts[:s]) for s in range(len(arrays) + 1)]
        self.steps = self.starts[-1]

    def batch_and_tile(self, s, t):
        local = jnp.clip(t - self.starts[s], 0, self.starts[s + 1] - self.starts[s] - 1)
        return local // self.per_batch[s], local % self.per_batch[s]

    def token_spec(self, s, width):
        def index(t):
            b, i = self.batch_and_tile(s, t)
            return b, i, 0
        return pl.BlockSpec((1, self.tile, width), index)

    def mod_spec(self, mod_rows):
        def index(t):
            row = mod_rows[0](self.batch_and_tile(0, t)[0])
            for s in range(1, len(mod_rows)):
                row = jnp.where(t >= self.starts[s], mod_rows[s](self.batch_and_tile(s, t)[0]), row)
            return row, 0, 0
        return pl.BlockSpec((1, N_MOD, D_MODEL), index)

    def active(self, s):
        t = pl.program_id(0)
        return jnp.logical_and(t >= self.starts[s], t < self.starts[s + 1])


def _ffn_kernel(*refs, streams, mod_base):
    n = len(streams.per_batch)
    x_refs, (mod_ref, gain_ref, win_ref, wout_ref), o_refs, act_ref = (
        refs[:n], refs[n:n + 4], refs[n + 4:2 * n + 4], refs[2 * n + 4])

    def half_step(x_ref, o_ref):
        x = x_ref[0]
        shift = mod_ref[0, mod_base:mod_base + 1, :]
        scale = mod_ref[0, mod_base + 1:mod_base + 2, :]
        gate = mod_ref[0, mod_base + 2:mod_base + 3, :]
        h = (_rms(x, gain_ref[...]) * (1.0 + scale) + shift).astype(BF16)
        for j in range(D_FF // FF_CHUNK):
            lo = j * FF_CHUNK
            g = _dot(h, win_ref[:, lo:lo + FF_CHUNK])
            u = _dot(h, win_ref[:, D_FF + lo:D_FF + lo + FF_CHUNK])
            act_ref[:, lo:lo + FF_CHUNK] = (g * _sigmoid(g) * u).astype(BF16)
        y = _dot(act_ref[...], wout_ref[...])
        o_ref[0] = x + 0.5 * gate * y

    if n == 1:
        half_step(x_refs[0], o_refs[0])
    else:
        for s in range(n):
            pl.when(streams.active(s))(functools.partial(half_step, x_refs[s], o_refs[s]))


def _ffn(xs, mod, mod_rows, mod_base, gain, w_in, w_out):
    tile = TOKEN_TILE
    streams = _TokenStreams(xs, tile)
    tok = [streams.token_spec(s, D_MODEL) for s in range(len(xs))]
    return pl.pallas_call(
        functools.partial(_ffn_kernel, streams=streams, mod_base=mod_base),
        out_shape=[jax.ShapeDtypeStruct(x.shape, F32) for x in xs],
        grid=(streams.steps,),
        in_specs=tok + [
            streams.mod_spec(mod_rows),
            _resident((1, D_MODEL)),
            _resident((D_MODEL, 2 * D_FF)),
            _resident((D_FF, D_MODEL)),
        ],
        out_specs=tok,
        scratch_shapes=[pltpu.VMEM((tile, D_FF), BF16)],
        compiler_params=_params("arbitrary"),
        name="ffn",
    )(*xs, mod, gain, w_in, w_out)


def _head_norm(t, gain):
    return t * lax.rsqrt(jnp.mean(t * t, axis=-1, keepdims=True) + EPS) * gain


def _rope(t, cos2, sin2):
    return t * cos2 + pltpu.roll(t, AXIS_ROPE_DIM, 1) * sin2


def _store_widened_values(v_ref, v):
    ones = jnp.ones((v.shape[0], V7X_MXU_DIM - HEAD_DIM), BF16)
    for hd in range(N_KV_HEADS):
        lo = hd * V7X_MXU_DIM
        v_ref[0, :, lo:lo + HEAD_DIM] = v[:, hd * HEAD_DIM:(hd + 1) * HEAD_DIM].astype(BF16)
        v_ref[0, :, lo + HEAD_DIM:lo + V7X_MXU_DIM] = ones


def _inproj_kernel(x_ref, ctx_ref, mod_ref, gain_ref, w_ref, qn_ref, kn_ref, cos_ref, sin_ref,
                   q_ref, k_ref, v_ref, f_ref, ga_ref, gf_ref, kc_ref, vc_ref, *, streams):
    kn = kn_ref[...]
    heads_per_dot = V7X_MXU_DIM // HEAD_DIM

    def normed_input(tok_ref):
        shift = mod_ref[0, 3:4, :]
        scale = mod_ref[0, 4:5, :]
        return (_rms(tok_ref[0], gain_ref[...]) * (1.0 + scale) + shift).astype(BF16)

    @pl.when(streams.active(0))
    def _latent_tokens():
        h = normed_input(x_ref)

        def wide_dot(col):
            return _dot(h, w_ref[:, col:col + V7X_MXU_DIM])

        cos2 = cos_ref[...]
        sin2 = sin_ref[...]
        qn = qn_ref[...] * (ATTN_SCALE * LOG2_E)
        for c in range(Q_WIDTH // V7X_MXU_DIM):
            t = wide_dot(c * V7X_MXU_DIM)
            for j in range(heads_per_dot):
                tj = t[:, j * HEAD_DIM:(j + 1) * HEAD_DIM]
                q_ref[0, c * heads_per_dot + j] = _rope(_head_norm(tj, qn), cos2, sin2).astype(BF16)
        for c in range(KV_WIDTH // V7X_MXU_DIM):
            t = wide_dot(K_OFF + c * V7X_MXU_DIM)
            for j in range(heads_per_dot):
                lo = c * V7X_MXU_DIM + j * HEAD_DIM
                tj = t[:, j * HEAD_DIM:(j + 1) * HEAD_DIM]
                k_ref[0, :, lo:lo + HEAD_DIM] = _rope(_head_norm(tj, kn), cos2, sin2).astype(BF16)
        _store_widened_values(v_ref, _dot(h, w_ref[:, V_OFF:V_OFF + KV_WIDTH]))
        for c in range(FOURIER_WIDTH // V7X_MXU_DIM):
            lo = c * V7X_MXU_DIM
            f_ref[0, :, lo:lo + V7X_MXU_DIM] = wide_dot(F_OFF + lo).astype(BF16)
        for c in range(D_MODEL // V7X_MXU_DIM):
            lo = c * V7X_MXU_DIM
            ga_ref[0, :, lo:lo + V7X_MXU_DIM] = _sigmoid(wide_dot(GA_OFF + lo)).astype(BF16)
            gf_ref[0, :, lo:lo + V7X_MXU_DIM] = _sigmoid(wide_dot(GF_OFF + lo)).astype(BF16)

    @pl.when(streams.active(1))
    def _context_tokens():
        h = normed_input(ctx_ref)
        t = _dot(h, w_ref[:, K_OFF:K_OFF + KV_WIDTH])
        for hd in range(N_KV_HEADS):
            lo = hd * HEAD_DIM
            kc_ref[0, :, lo:lo + HEAD_DIM] = _head_norm(t[:, lo:lo + HEAD_DIM], kn).astype(BF16)
        _store_widened_values(vc_ref, _dot(h, w_ref[:, V_OFF:V_OFF + KV_WIDTH]))


def _inproj(x, ctx, mod, mod_rows, gain, w_in, q_norm, k_norm, cos2, sin2):
    nb, t, _ = x.shape
    streams = _TokenStreams([x, ctx], TOKEN_TILE)
    tm = streams.tile
    tok = lambda w: streams.token_spec(0, w)
    ctx_tok = lambda w: streams.token_spec(1, w)
    shp = lambda w: jax.ShapeDtypeStruct((nb, t, w), BF16)
    ctx_shp = lambda w: jax.ShapeDtypeStruct(ctx.shape[:2] + (w,), BF16)

    def q_index(step):
        b, i = streams.batch_and_tile(0, step)
        return b, 0, i, 0

    def rope_index(step):
        return streams.batch_and_tile(0, step)[1], 0

    return pl.pallas_call(
        functools.partial(_inproj_kernel, streams=streams),
        out_shape=[jax.ShapeDtypeStruct((nb, N_Q_HEADS, t, HEAD_DIM), BF16), shp(KV_WIDTH),
                   shp(VX_WIDTH), shp(FOURIER_WIDTH), shp(D_MODEL), shp(D_MODEL),
                   ctx_shp(KV_WIDTH), ctx_shp(VX_WIDTH)],
        grid=(streams.steps,),
        in_specs=[
            tok(D_MODEL),
            ctx_tok(D_MODEL),
            streams.mod_spec(mod_rows),
            _resident((1, D_MODEL)),
            _resident(w_in.shape),
            _resident((1, HEAD_DIM)),
            _resident((1, HEAD_DIM)),
            pl.BlockSpec((tm, HEAD_DIM), rope_index),
            pl.BlockSpec((tm, HEAD_DIM), rope_index),
        ],
        out_specs=[pl.BlockSpec((1, N_Q_HEADS, tm, HEAD_DIM), q_index),
                   tok(KV_WIDTH), tok(VX_WIDTH), tok(FOURIER_WIDTH), tok(D_MODEL), tok(D_MODEL),
                   ctx_tok(KV_WIDTH), ctx_tok(VX_WIDTH)],
        compiler_params=_params("arbitrary"),
        name="inproj",
    )(x, ctx, mod, gain, w_in, q_norm, k_norm, cos2, sin2)


_NT = (((1,), (1,)), ((), ()))


ATTN_SCRATCH_PER_HEAD = 3
ATTN_SCORE_BUFFERS = 3


def _attn_kernel(q_ref, kc_ref, k_ref, vcx_ref, vx_ref, o_ref, *scratch):
    n_head_refs = ATTN_SCRATCH_PER_HEAD * GQA_GROUP
    heads = [scratch[ATTN_SCRATCH_PER_HEAD * g:ATTN_SCRATCH_PER_HEAD * (g + 1)] for g in range(GQA_GROUP)]
    scores = [scratch[n_head_refs + 2 * b:n_head_refs + 2 * b + 2] for b in range(ATTN_SCORE_BUFFERS)]
    score_refs = lambda g: scores[g % ATTN_SCORE_BUFFERS]

    @pl.when(pl.program_id(0) == 0)
    def _no_previous_item():
        for pc_ref, pl_ref, _ in heads:
            pc_ref[...] = jnp.ones(pc_ref.shape, BF16)
            pl_ref[...] = jnp.ones(pl_ref.shape, BF16)
        for ref in (*score_refs(GQA_GROUP - 1), heads[-1][2]):
            ref[...] = jnp.zeros(ref.shape, F32)

    def weights_from_scores(g):
        pc_ref, pl_ref, m_ref = heads[g]
        m = m_ref[...]
        for s_ref, p_ref in zip(score_refs(g), (pc_ref, pl_ref)):
            for lo in range(0, s_ref.shape[1], V7X_MXU_DIM):
                p_ref[:, lo:lo + V7X_MXU_DIM] = jnp.exp2(s_ref[:, lo:lo + V7X_MXU_DIM] - m).astype(BF16)

    def values_then_scores(g):
        pc_ref, pl_ref, m_ref = heads[g]
        sc_ref, sl_ref = score_refs(g)
        oe = _dot(pc_ref[...], vcx_ref[0]) + _dot(pl_ref[...], vx_ref[0])
        o_ref[0, :, g * HEAD_DIM:(g + 1) * HEAD_DIM] = (oe[:, :HEAD_DIM] / oe[:, HEAD_DIM:]).astype(BF16)

        q = q_ref[0, g]
        s_c = lax.dot_general(q, kc_ref[0], _NT, preferred_element_type=F32)
        s_l = lax.dot_general(q, k_ref[0], _NT, preferred_element_type=F32)
        m_ref[...] = jnp.maximum(jnp.max(s_c, axis=-1, keepdims=True), jnp.max(s_l, axis=-1, keepdims=True))
        sc_ref[...] = s_c
        sl_ref[...] = s_l

    last = GQA_GROUP - 1
    weights_from_scores(last)
    for g in range(GQA_GROUP):
        if g >= ATTN_SCORE_BUFFERS:
            weights_from_scores(g - ATTN_SCORE_BUFFERS)
        values_then_scores(g)
    for g in range(GQA_GROUP - ATTN_SCORE_BUFFERS, last):
        weights_from_scores(g)


def _attention(q, k, vx, k_c, vx_c):
    nb, _, n, _ = q.shape
    n_ctx = k_c.shape[1]
    tq = ATTN_Q_TILE
    assert n % tq == 0 and n % V7X_MXU_DIM == 0 and n_ctx % V7X_MXU_DIM == 0, (n, n_ctx)
    tiles = n // tq
    items = nb * N_KV_HEADS * tiles

    def item(t):
        return t // (N_KV_HEADS * tiles), (t // tiles) % N_KV_HEADS, t % tiles

    def scored(t):
        return item(jnp.minimum(t, items - 1))

    def weighted(t):
        return item(jnp.maximum(t - 1, 0))

    def q_map(t):
        b, h, i = scored(t)
        return b, h, i, 0

    def k_map(t):
        b, h, _ = scored(t)
        return b, 0, h

    def v_map(t):
        b, h, _ = weighted(t)
        return b, 0, h

    def o_map(t):
        b, h, i = weighted(t)
        return b, i, h

    return pl.pallas_call(
        _attn_kernel,
        out_shape=jax.ShapeDtypeStruct((nb, n, Q_WIDTH), BF16),
        grid=(items + 1,),
        in_specs=[
            pl.BlockSpec((1, GQA_GROUP, tq, HEAD_DIM), q_map),
            pl.BlockSpec((1, n_ctx, HEAD_DIM), k_map),
            pl.BlockSpec((1, n, HEAD_DIM), k_map),
            pl.BlockSpec((1, n_ctx, V7X_MXU_DIM), v_map),
            pl.BlockSpec((1, n, V7X_MXU_DIM), v_map),
        ],
        out_specs=pl.BlockSpec((1, tq, GQA_GROUP * HEAD_DIM), o_map),
        scratch_shapes=(
            GQA_GROUP * [pltpu.VMEM((tq, n_ctx), BF16), pltpu.VMEM((tq, n), BF16), pltpu.VMEM((tq, 1), F32)]
            + ATTN_SCORE_BUFFERS * [pltpu.VMEM((tq, n_ctx), F32), pltpu.VMEM((tq, n), F32)]),
        compiler_params=pltpu.CompilerParams(dimension_semantics=("arbitrary",),
                                             vmem_limit_bytes=V7X_VMEM_BYTES - 3 * 2**20),
        name="attn",
    )(q, k_c, k, vx_c, vx)


DFT_RADIX = 8


def _dft_tables(n):
    r = n // DFT_RADIX
    k = np.arange(DFT_RADIX)[:, None, None] + DFT_RADIX * np.arange(r)[None, :, None]
    ang = 2.0 * np.pi * ((k * np.arange(r)[None, None, :]) % n) / n
    cos, sin = np.cos(ang), np.sin(ang)
    pos = np.concatenate([np.concatenate([cos, sin], axis=2),
                          np.concatenate([-sin, cos], axis=2)], axis=1)
    c = FOURIER_GROUP
    scale = 1.0 / np.sqrt(float(n * c))
    ang_c = 2.0 * np.pi * (np.outer(np.arange(c), np.arange(c)) % c) / c
    eye = np.eye(V7X_MXU_DIM // c)
    chan = np.concatenate([np.kron(eye, np.cos(ang_c) * scale),
                           np.kron(eye, np.sin(ang_c) * scale)], axis=0)
    return jnp.asarray(pos, F32).astype(BF16), jnp.asarray(chan, F32).astype(BF16)


def _fold_radix8(x):
    c = 0.5 ** 0.5
    e02p, e02m, e13p, e13m = x[0] + x[4], x[0] - x[4], x[2] + x[6], x[2] - x[6]
    o02p, o02m, o13p, o13m = x[1] + x[5], x[1] - x[5], x[3] + x[7], x[3] - x[7]
    e0, e2, o0, o2 = e02p + e13p, e02p - e13p, o02p + o13p, o02p - o13p
    p, q = c * (o02m - o13m), c * (o02m + o13m)
    re = [e0 + o0, e02m + p, e2, e02m - p, e0 - o0, e02m - p, e2, e02m + p]
    im1, im3 = -e13m - q, e13m - q
    im = [None, im1, -o2, im3, None, -im3, o2, -im1]
    return re, im


def _fourier_kernel(f_ref, pos_ref, chan_ref, o_ref, t_ref, zr_ref, zi_ref, y_ref):
    r = f_ref.shape[1] // DFT_RADIX
    w = FOURIER_WIDTH
    pair = V7X_MXU_DIM
    slab = y_ref.shape[2]
    for lo in range(0, w, pair):
        for sub in range(lo, lo + pair, FOURIER_GROUP):
            lanes = slice(sub, sub + FOURIER_GROUP)
            x = [f_ref[0, n1 * r:(n1 + 1) * r, lanes].astype(F32) for n1 in range(DFT_RADIX)]
            re, im = _fold_radix8(x)
            for k1 in range(DFT_RADIX):
                t_ref[k1, 0:r, lanes] = re[k1].astype(BF16)
                t_ref[k1, r:2 * r, lanes] = (
                    jnp.zeros((r, FOURIER_GROUP), BF16) if im[k1] is None else im[k1].astype(BF16))
        h = lo // pair
        for k1 in range(DFT_RADIX):
            z = _dot(pos_ref[k1], t_ref[k1, :, lo:lo + pair])
            zr_ref[h, k1 * r:(k1 + 1) * r, :] = z[0:r].astype(BF16)
            zi_ref[h, k1 * r:(k1 + 1) * r, :] = z[r:2 * r].astype(BF16)
        rows_per_dot = (DFT_RADIX // 2) * r
        for top in range(0, DFT_RADIX * r, rows_per_dot):
            rows = slice(top, top + rows_per_dot)
            y = (_dot(zr_ref[h, rows, :], chan_ref[0:pair, :])
                 + _dot(zi_ref[h, rows, :], chan_ref[pair:2 * pair, :]))
            for k1 in range(top // r, (top + rows_per_dot) // r):
                for j in range(pair // slab):
                    y_ref[(lo + j * slab) // slab, pl.ds(k1, r, stride=DFT_RADIX), :] = (
                        y[k1 * r - top:(k1 + 1) * r - top, j * slab:(j + 1) * slab])
    for j in range(w // slab):
        o_ref[0, :, j * slab:(j + 1) * slab] = y_ref[j].astype(BF16)


def _fourier(f):
    nb, n, w = f.shape
    assert w == FOURIER_WIDTH and n % (DFT_RADIX * V7X_MXU_DIM) == 0, f.shape
    pos, chan = _dft_tables(n)
    r = n // DFT_RADIX
    blk = pl.BlockSpec((1, n, w), lambda b: (b, 0, 0))
    lanes = V7X_LANES
    return pl.pallas_call(
        _fourier_kernel,
        out_shape=jax.ShapeDtypeStruct(f.shape, BF16),
        grid=(nb,),
        in_specs=[blk, _resident(pos.shape), _resident(chan.shape)],
        out_specs=blk,
        scratch_shapes=[pltpu.VMEM((DFT_RADIX, 2 * r, w), BF16),
                        pltpu.VMEM((w // V7X_MXU_DIM, n, V7X_MXU_DIM), BF16),
                        pltpu.VMEM((w // V7X_MXU_DIM, n, V7X_MXU_DIM), BF16),
                        pltpu.VMEM((w // lanes, n, lanes), F32)],
        compiler_params=_params("parallel"),
        name="fourier",
    )(f, pos, chan)


def _merge_kernel(x_ref, mod_ref, ya_ref, yf_ref, ga_ref, gf_ref, wab_ref, wfb_ref, wo_ref, o_ref,
                  merged_ref):
    gate = mod_ref[0, 5:6, :]
    for lo in range(0, D_MODEL, V7X_MXU_DIM):
        cols = slice(lo, lo + V7X_MXU_DIM)
        merged_ref[:, cols] = (ga_ref[0, :, cols].astype(F32) * _dot(ya_ref[0], wab_ref[:, cols])
                               + gf_ref[0, :, cols].astype(F32) * _dot(yf_ref[0], wfb_ref[:, cols])
                               ).astype(BF16)
    mix = _dot(merged_ref[...], wo_ref[...])
    o_ref[0] = x_ref[0] + gate * mix


def _merge(x, mod, y_attn, y_four, sig_a, sig_f, w_ab, w_fb, w_o):
    nb, t, _ = x.shape
    tm = WIDE_TOKEN_TILE
    assert t % tm == 0, x.shape
    tok = lambda w: pl.BlockSpec((1, tm, w), lambda b, i: (b, i, 0))
    return pl.pallas_call(
        _merge_kernel,
        out_shape=jax.ShapeDtypeStruct(x.shape, F32),
        grid=(nb, t // tm),
        in_specs=[
            tok(D_MODEL),
            pl.BlockSpec((1, N_MOD, D_MODEL), lambda b, i: (b, 0, 0)),
            tok(Q_WIDTH), tok(FOURIER_WIDTH), tok(D_MODEL), tok(D_MODEL),
            _resident(w_ab.shape), _resident(w_fb.shape), _resident(w_o.shape),
        ],
        out_specs=tok(D_MODEL),
        scratch_shapes=[pltpu.VMEM((tm, D_MODEL), BF16)],
        compiler_params=_params("parallel", "parallel"),
        name="merge",
    )(x, mod, y_attn, y_four, sig_a, sig_f, w_ab, w_fb, w_o)


def _rope_tables(n_tokens):
    f32 = np.float32
    rows = n_tokens // GRID_W
    row_ids = np.repeat(np.arange(rows, dtype=f32), GRID_W)
    col_ids = np.tile(np.arange(GRID_W, dtype=f32), rows)
    inv_freq = f32(ROPE_THETA) ** (-np.arange(0, AXIS_ROPE_DIM, 2, dtype=f32) / f32(AXIS_ROPE_DIM))
    ang = np.concatenate([row_ids[:, None] * inv_freq, col_ids[:, None] * inv_freq], axis=-1)
    cos, sin = np.cos(ang), np.sin(ang)
    assert ang.dtype == f32 and cos.dtype == f32
    return (jnp.asarray(np.concatenate([cos, cos], axis=-1)),
            jnp.asarray(np.concatenate([-sin, sin], axis=-1)))


def kernel(x, c, ctx, c_ctx, w_ada, b_ada, norm_ffn1, w_ffn1_in, w_ffn1_out, norm_mix, w_in,
           q_norm, k_norm, w_attn_branch, w_fourier_branch, w_out, norm_ffn2, w_ffn2_in, w_ffn2_out):
    nb, n_lat, _ = x.shape
    n_ctx = ctx.shape[1]
    depth = w_ada.shape[0]
    assert depth == 1, "the context stream is only carried through the (single) last layer"
    assert x.shape[2] == D_MODEL and ctx.shape == (nb, n_ctx, D_MODEL) and n_lat % GRID_W == 0
    assert w_in.shape[1:] == (D_MODEL, GF_OFF + D_MODEL) and w_ffn1_in.shape[1:] == (D_MODEL, 2 * D_FF)
    assert (nb * n_ctx) % TOKEN_TILE == 0, "context tokens are processed as one flat stream of tiles"
    cos2, sin2 = _rope_tables(n_lat)
    ctx_row = nb
    pad_rows = -(nb + 1) % V7X_SUBLANES
    c_all = jnp.concatenate([c, c_ctx[None, :], jnp.zeros((pad_rows, D_MODEL), F32)], axis=0)

    i = 0
    mod = _adaln(c_all, w_ada[i], b_ada[i][None, :]).reshape(c_all.shape[0], N_MOD, D_MODEL)
    own_row, shared_row = (lambda b: b), (lambda b: ctx_row)
    x, ctx_flat = _ffn([x, ctx.reshape(1, nb * n_ctx, D_MODEL)], mod, [own_row, shared_row], 0,
                       norm_ffn1[i][None, :], w_ffn1_in[i], w_ffn1_out[i])

    q, k, vx, f, sig_a, sig_f, k_c, vx_c = _inproj(
        x, ctx_flat, mod, [own_row, shared_row], norm_mix[i][None, :], w_in[i],
        q_norm[i][None, :], k_norm[i][None, :], cos2, sin2)
    y_attn = _attention(q, k, vx, k_c.reshape(nb, n_ctx, KV_WIDTH), vx_c.reshape(nb, n_ctx, VX_WIDTH))
    y_four = _fourier(f)
    x = _merge(x, mod, y_attn, y_four, sig_a, sig_f, w_attn_branch[i], w_fourier_branch[i], w_out[i])

    return _ffn([x], mod, [own_row], 6, norm_ffn2[i][None, :], w_ffn2_in[i], w_ffn2_out[i])[0]
```

```python
import functools

import numpy as np
import jax
import jax.numpy as jnp
from jax import lax
from jax.experimental import pallas as pl
from jax.experimental.pallas import tpu as pltpu

D_MODEL = 1024
GRID_W = 64
HEAD_DIM = 128
N_Q_HEADS = D_MODEL // HEAD_DIM
N_KV_HEADS = N_Q_HEADS // 4
GQA_GROUP = N_Q_HEADS // N_KV_HEADS
Q_WIDTH = N_Q_HEADS * HEAD_DIM
KV_WIDTH = N_KV_HEADS * HEAD_DIM
FOURIER_GROUP = 128
N_FOURIER_GROUPS = 4
FOURIER_WIDTH = N_FOURIER_GROUPS * FOURIER_GROUP
D_FF = 2816
AXIS_ROPE_DIM = HEAD_DIM // 2
ROPE_THETA = 10000.0
EPS = 1e-6
N_MOD = 9
ATTN_SCALE = HEAD_DIM ** -0.5
LOG2_E = 1.4426950408889634

K_OFF = Q_WIDTH
V_OFF = K_OFF + KV_WIDTH
F_OFF = V_OFF + KV_WIDTH
GA_OFF = F_OFF + FOURIER_WIDTH
GF_OFF = GA_OFF + D_MODEL

V7X_MXU_DIM = 256
V7X_LANES = 128
V7X_SUBLANES = 8
VX_WIDTH = N_KV_HEADS * V7X_MXU_DIM
V7X_VMEM_BYTES = 64 * 2**20
VMEM_LIMIT = V7X_VMEM_BYTES - 8 * 2**20

TOKEN_TILE = 512
WIDE_TOKEN_TILE = 1024
FF_CHUNK = V7X_MXU_DIM
ATTN_Q_TILE = 1024
ADALN_TILE = 256

F32 = jnp.float32
BF16 = jnp.bfloat16


def _params(*sem):
    return pltpu.CompilerParams(dimension_semantics=sem, vmem_limit_bytes=VMEM_LIMIT)


def _resident(shape):
    return pl.BlockSpec(shape, lambda *_: (0,) * len(shape), pipeline_mode=pl.Buffered(1))


def _rms(x, gain):
    return x * lax.rsqrt(jnp.mean(x * x, axis=-1, keepdims=True) + EPS) * gain


def _dot(a, b):
    return jnp.dot(a, b.astype(BF16), preferred_element_type=F32)


def _sigmoid(x):
    return 0.5 * jnp.tanh(0.5 * x) + 0.5


def _adaln_kernel(c_ref, w_ref, b_ref, o_ref):
    c = c_ref[...]
    part = _dot((c * _sigmoid(c)).astype(BF16), w_ref[...])

    @pl.when(pl.program_id(0) == 0)
    def _first():
        o_ref[...] = part + b_ref[...]

    @pl.when(pl.program_id(0) > 0)
    def _rest():
        o_ref[...] += part


def _adaln(c_all, w, b):
    rows, n_out = c_all.shape[0], w.shape[1]
    assert rows % V7X_SUBLANES == 0 and D_MODEL % ADALN_TILE == 0, (rows, n_out)
    return pl.pallas_call(
        _adaln_kernel,
        out_shape=jax.ShapeDtypeStruct((rows, n_out), F32),
        grid=(D_MODEL // ADALN_TILE,),
        in_specs=[
            pl.BlockSpec((rows, ADALN_TILE), lambda k: (0, k)),
            pl.BlockSpec((ADALN_TILE, n_out), lambda k: (k, 0)),
            pl.BlockSpec((1, n_out), lambda k: (0, 0)),
        ],
        out_specs=pl.BlockSpec((rows, n_out), lambda k: (0, 0)),
        compiler_params=_params("arbitrary"),
        name="adaln",
    )(c_all, w, b)


class _TokenStreams:
    def __init__(self, arrays, tile):
        assert all(a.ndim == 3 and a.shape[1] % tile == 0 for a in arrays), [a.shape for a in arrays]
        self.tile = tile
        self.per_batch = [a.shape[1] // tile for a in arrays]
        counts = [a.shape[0] * p for a, p in zip(arrays, self.per_batch)]
        self.starts = [sum(counts[:s]) for s in range(len(arrays) + 1)]
        self.steps = self.starts[-1]

    def batch_and_tile(self, s, t):
        local = jnp.clip(t - self.starts[s], 0, self.starts[s + 1] - self.starts[s] - 1)
        return local // self.per_batch[s], local % self.per_batch[s]

    def token_spec(self, s, width):
        def index(t):
            b, i = self.batch_and_tile(s, t)
            return b, i, 0
        return pl.BlockSpec((1, self.tile, width), index)

    def mod_spec(self, mod_rows):
        def index(t):
            row = mod_rows[0](self.batch_and_tile(0, t)[0])
            for s in range(1, len(mod_rows)):
                row = jnp.where(t >= self.starts[s], mod_rows[s](self.batch_and_tile(s, t)[0]), row)
            return row, 0, 0
        return pl.BlockSpec((1, N_MOD, D_MODEL), index)

    def active(self, s):
        t = pl.program_id(0)
        return jnp.logical_and(t >= self.starts[s], t < self.starts[s + 1])


def _ffn_kernel(*refs, streams, mod_base):
    n = len(streams.per_batch)
    x_refs, (mod_ref, gain_ref, win_ref, wout_ref), o_refs, act_ref = (
        refs[:n], refs[n:n + 4], refs[n + 4:2 * n + 4], refs[2 * n + 4])

    def half_step(x_ref, o_ref):
        x = x_ref[0]
        shift = mod_ref[0, mod_base:mod_base + 1, :]
        scale = mod_ref[0, mod_base + 1:mod_base + 2, :]
        gate = mod_ref[0, mod_base + 2:mod_base + 3, :]
        h = (_rms(x, gain_ref[...]) * (1.0 + scale) + shift).astype(BF16)
        for j in range(D_FF // FF_CHUNK):
            lo = j * FF_CHUNK
            g = _dot(h, win_ref[:, lo:lo + FF_CHUNK])
            u = _dot(h, win_ref[:, D_FF + lo:D_FF + lo + FF_CHUNK])
            act_ref[:, lo:lo + FF_CHUNK] = (g * _sigmoid(g) * u).astype(BF16)
        y = _dot(act_ref[...], wout_ref[...])
        o_ref[0] = x + 0.5 * gate * y

    if n == 1:
        half_step(x_refs[0], o_refs[0])
    else:
        for s in range(n):
            pl.when(streams.active(s))(functools.partial(half_step, x_refs[s], o_refs[s]))


def _ffn(xs, mod, mod_rows, mod_base, gain, w_in, w_out):
    tile = TOKEN_TILE
    streams = _TokenStreams(xs, tile)
    tok = [streams.token_spec(s, D_MODEL) for s in range(len(xs))]
    return pl.pallas_call(
        functools.partial(_ffn_kernel, streams=streams, mod_base=mod_base),
        out_shape=[jax.ShapeDtypeStruct(x.shape, F32) for x in xs],
        grid=(streams.steps,),
        in_specs=tok + [
            streams.mod_spec(mod_rows),
            _resident((1, D_MODEL)),
            _resident((D_MODEL, 2 * D_FF)),
            _resident((D_FF, D_MODEL)),
        ],
        out_specs=tok,
        scratch_shapes=[pltpu.VMEM((tile, D_FF), BF16)],
        compiler_params=_params("arbitrary"),
        name="ffn",
    )(*xs, mod, gain, w_in, w_out)


def _head_norm(t, gain):
    return t * lax.rsqrt(jnp.mean(t * t, axis=-1, keepdims=True) + EPS) * gain


def _rope(t, cos2, sin2):
    return t * cos2 + pltpu.roll(t, AXIS_ROPE_DIM, 1) * sin2


def _store_widened_values(v_ref, v):
    ones = jnp.ones((v.shape[0], V7X_MXU_DIM - HEAD_DIM), BF16)
    for hd in range(N_KV_HEADS):
        lo = hd * V7X_MXU_DIM
        v_ref[0, :, lo:lo + HEAD_DIM] = v[:, hd * HEAD_DIM:(hd + 1) * HEAD_DIM].astype(BF16)
        v_ref[0, :, lo + HEAD_DIM:lo + V7X_MXU_DIM] = ones


def _inproj_kernel(x_ref, ctx_ref, mod_ref, gain_ref, w_ref, qn_ref, kn_ref, cos_ref, sin_ref,
                   q_ref, k_ref, v_ref, f_ref, ga_ref, gf_ref, kc_ref, vc_ref, *, streams):
    kn = kn_ref[...]
    heads_per_dot = V7X_MXU_DIM // HEAD_DIM

    def normed_input(tok_ref):
        shift = mod_ref[0, 3:4, :]
        scale = mod_ref[0, 4:5, :]
        return (_rms(tok_ref[0], gain_ref[...]) * (1.0 + scale) + shift).astype(BF16)

    @pl.when(streams.active(0))
    def _latent_tokens():
        h = normed_input(x_ref)

        def wide_dot(col):
            return _dot(h, w_ref[:, col:col + V7X_MXU_DIM])

        cos2 = cos_ref[...]
        sin2 = sin_ref[...]
        qn = qn_ref[...] * (ATTN_SCALE * LOG2_E)
        for c in range(Q_WIDTH // V7X_MXU_DIM):
            t = wide_dot(c * V7X_MXU_DIM)
            for j in range(heads_per_dot):
                tj = t[:, j * HEAD_DIM:(j + 1) * HEAD_DIM]
                q_ref[0, c * heads_per_dot + j] = _rope(_head_norm(tj, qn), cos2, sin2).astype(BF16)
        for c in range(KV_WIDTH // V7X_MXU_DIM):
            t = wide_dot(K_OFF + c * V7X_MXU_DIM)
            for j in range(heads_per_dot):
                lo = c * V7X_MXU_DIM + j * HEAD_DIM
                tj = t[:, j * HEAD_DIM:(j + 1) * HEAD_DIM]
                k_ref[0, :, lo:lo + HEAD_DIM] = _rope(_head_norm(tj, kn), cos2, sin2).astype(BF16)
        _store_widened_values(v_ref, _dot(h, w_ref[:, V_OFF:V_OFF + KV_WIDTH]))
        for c in range(FOURIER_WIDTH // V7X_MXU_DIM):
            lo = c * V7X_MXU_DIM
            f_ref[0, :, lo:lo + V7X_MXU_DIM] = wide_dot(F_OFF + lo).astype(BF16)
        for c in range(D_MODEL // V7X_MXU_DIM):
            lo = c * V7X_MXU_DIM
            ga_ref[0, :, lo:lo + V7X_MXU_DIM] = _sigmoid(wide_dot(GA_OFF + lo)).astype(BF16)
            gf_ref[0, :, lo:lo + V7X_MXU_DIM] = _sigmoid(wide_dot(GF_OFF + lo)).astype(BF16)

    @pl.when(streams.active(1))
    def _context_tokens():
        h = normed_input(ctx_ref)
        t = _dot(h, w_ref[:, K_OFF:K_OFF + KV_WIDTH])
        for hd in range(N_KV_HEADS):
            lo = hd * HEAD_DIM
            kc_ref[0, :, lo:lo + HEAD_DIM] = _head_norm(t[:, lo:lo + HEAD_DIM], kn).astype(BF16)
        _store_widened_values(vc_ref, _dot(h, w_ref[:, V_OFF:V_OFF + KV_WIDTH]))


def _inproj(x, ctx, mod, mod_rows, gain, w_in, q_norm, k_norm, cos2, sin2):
    nb, t, _ = x.shape
    streams = _TokenStreams([x, ctx], TOKEN_TILE)
    tm = streams.tile
    tok = lambda w: streams.token_spec(0, w)
    ctx_tok = lambda w: streams.token_spec(1, w)
    shp = lambda w: jax.ShapeDtypeStruct((nb, t, w), BF16)
    ctx_shp = lambda w: jax.ShapeDtypeStruct(ctx.shape[:2] + (w,), BF16)

    def q_index(step):
        b, i = streams.batch_and_tile(0, step)
        return b, 0, i, 0

    def rope_index(step):
        return streams.batch_and_tile(0, step)[1], 0

    return pl.pallas_call(
        functools.partial(_inproj_kernel, streams=streams),
        out_shape=[jax.ShapeDtypeStruct((nb, N_Q_HEADS, t, HEAD_DIM), BF16), shp(KV_WIDTH),
                   shp(VX_WIDTH), shp(FOURIER_WIDTH), shp(D_MODEL), shp(D_MODEL),
                   ctx_shp(KV_WIDTH), ctx_shp(VX_WIDTH)],
        grid=(streams.steps,),
        in_specs=[
            tok(D_MODEL),
            ctx_tok(D_MODEL),
            streams.mod_spec(mod_rows),
            _resident((1, D_MODEL)),
            _resident(w_in.shape),
            _resident((1, HEAD_DIM)),
            _resident((1, HEAD_DIM)),
            pl.BlockSpec((tm, HEAD_DIM), rope_index),
            pl.BlockSpec((tm, HEAD_DIM), rope_index),
        ],
        out_specs=[pl.BlockSpec((1, N_Q_HEADS, tm, HEAD_DIM), q_index),
                   tok(KV_WIDTH), tok(VX_WIDTH), tok(FOURIER_WIDTH), tok(D_MODEL), tok(D_MODEL),
                   ctx_tok(KV_WIDTH), ctx_tok(VX_WIDTH)],
        compiler_params=_params("arbitrary"),
        name="inproj",
    )(x, ctx, mod, gain, w_in, q_norm, k_norm, cos2, sin2)


_NT = (((1,), (1,)), ((), ()))


ATTN_SCRATCH_PER_HEAD = 3
ATTN_SCORE_BUFFERS = 3


def _attn_kernel(q_ref, kc_ref, k_ref, vcx_ref, vx_ref, o_ref, *scratch):
    n_head_refs = ATTN_SCRATCH_PER_HEAD * GQA_GROUP
    heads = [scratch[ATTN_SCRATCH_PER_HEAD * g:ATTN_SCRATCH_PER_HEAD * (g + 1)] for g in range(GQA_GROUP)]
    scores = [scratch[n_head_refs + 2 * b:n_head_refs + 2 * b + 2] for b in range(ATTN_SCORE_BUFFERS)]
    score_refs = lambda g: scores[g % ATTN_SCORE_BUFFERS]

    @pl.when(pl.program_id(0) == 0)
    def _no_previous_item():
        for pc_ref, pl_ref, _ in heads:
            pc_ref[...] = jnp.ones(pc_ref.shape, BF16)
            pl_ref[...] = jnp.ones(pl_ref.shape, BF16)
        for ref in (*score_refs(GQA_GROUP - 1), heads[-1][2]):
            ref[...] = jnp.zeros(ref.shape, F32)

    def weights_from_scores(g):
        pc_ref, pl_ref, m_ref = heads[g]
        m = m_ref[...]
        for s_ref, p_ref in zip(score_refs(g), (pc_ref, pl_ref)):
            for lo in range(0, s_ref.shape[1], V7X_MXU_DIM):
                p_ref[:, lo:lo + V7X_MXU_DIM] = jnp.exp2(s_ref[:, lo:lo + V7X_MXU_DIM] - m).astype(BF16)

    def values_then_scores(g):
        pc_ref, pl_ref, m_ref = heads[g]
        sc_ref, sl_ref = score_refs(g)
        oe = _dot(pc_ref[...], vcx_ref[0]) + _dot(pl_ref[...], vx_ref[0])
        o_ref[0, :, g * HEAD_DIM:(g + 1) * HEAD_DIM] = (oe[:, :HEAD_DIM] / oe[:, HEAD_DIM:]).astype(BF16)

        q = q_ref[0, g]
        s_c = lax.dot_general(q, kc_ref[0], _NT, preferred_element_type=F32)
        s_l = lax.dot_general(q, k_ref[0], _NT, preferred_element_type=F32)
        m_ref[...] = jnp.maximum(jnp.max(s_c, axis=-1, keepdims=True), jnp.max(s_l, axis=-1, keepdims=True))
        sc_ref[...] = s_c
        sl_ref[...] = s_l

    last = GQA_GROUP - 1
    weights_from_scores(last)
    for g in range(GQA_GROUP):
        if g >= ATTN_SCORE_BUFFERS:
            weights_from_scores(g - ATTN_SCORE_BUFFERS)
        values_then_scores(g)
    for g in range(GQA_GROUP - ATTN_SCORE_BUFFERS, last):
        weights_from_scores(g)


def _attention(q, k, vx, k_c, vx_c):
    nb, _, n, _ = q.shape
    n_ctx = k_c.shape[1]
    tq = ATTN_Q_TILE
    assert n % tq == 0 and n % V7X_MXU_DIM == 0 and n_ctx % V7X_MXU_DIM == 0, (n, n_ctx)
    tiles = n // tq
    items = nb * N_KV_HEADS * tiles

    def item(t):
        return t // (N_KV_HEADS * tiles), (t // tiles) % N_KV_HEADS, t % tiles

    def scored(t):
        return item(jnp.minimum(t, items - 1))

    def weighted(t):
        return item(jnp.maximum(t - 1, 0))

    def q_map(t):
        b, h, i = scored(t)
        return b, h, i, 0

    def k_map(t):
        b, h, _ = scored(t)
        return b, 0, h

    def v_map(t):
        b, h, _ = weighted(t)
        return b, 0, h

    def o_map(t):
        b, h, i = weighted(t)
        return b, i, h

    return pl.pallas_call(
        _attn_kernel,
        out_shape=jax.ShapeDtypeStruct((nb, n, Q_WIDTH), BF16),
        grid=(items + 1,),
        in_specs=[
            pl.BlockSpec((1, GQA_GROUP, tq, HEAD_DIM), q_map),
            pl.BlockSpec((1, n_ctx, HEAD_DIM), k_map),
            pl.BlockSpec((1, n, HEAD_DIM), k_map),
            pl.BlockSpec((1, n_ctx, V7X_MXU_DIM), v_map),
            pl.BlockSpec((1, n, V7X_MXU_DIM), v_map),
        ],
        out_specs=pl.BlockSpec((1, tq, GQA_GROUP * HEAD_DIM), o_map),
        scratch_shapes=(
            GQA_GROUP * [pltpu.VMEM((tq, n_ctx), BF16), pltpu.VMEM((tq, n), BF16), pltpu.VMEM((tq, 1), F32)]
            + ATTN_SCORE_BUFFERS * [pltpu.VMEM((tq, n_ctx), F32), pltpu.VMEM((tq, n), F32)]),
        compiler_params=pltpu.CompilerParams(dimension_semantics=("arbitrary",),
                                             vmem_limit_bytes=V7X_VMEM_BYTES - 3 * 2**20),
        name="attn",
    )(q, k_c, k, vx_c, vx)


DFT_RADIX = 8


def _dft_tables(n):
    r = n // DFT_RADIX
    k = np.arange(DFT_RADIX)[:, None, None] + DFT_RADIX * np.arange(r)[None, :, None]
    ang = 2.0 * np.pi * ((k * np.arange(r)[None, None, :]) % n) / n
    cos, sin = np.cos(ang), np.sin(ang)
    pos = np.concatenate([np.concatenate([cos, sin], axis=2),
                          np.concatenate([-sin, cos], axis=2)], axis=1)
    c = FOURIER_GROUP
    scale = 1.0 / np.sqrt(float(n * c))
    ang_c = 2.0 * np.pi * (np.outer(np.arange(c), np.arange(c)) % c) / c
    eye = np.eye(V7X_MXU_DIM // c)
    chan = np.concatenate([np.kron(eye, np.cos(ang_c) * scale),
                           np.kron(eye, np.sin(ang_c) * scale)], axis=0)
    return jnp.asarray(pos, F32).astype(BF16), jnp.asarray(chan, F32).astype(BF16)


def _fold_radix8(x):
    c = 0.5 ** 0.5
    e02p, e02m, e13p, e13m = x[0] + x[4], x[0] - x[4], x[2] + x[6], x[2] - x[6]
    o02p, o02m, o13p, o13m = x[1] + x[5], x[1] - x[5], x[3] + x[7], x[3] - x[7]
    e0, e2, o0, o2 = e02p + e13p, e02p - e13p, o02p + o13p, o02p - o13p
    p, q = c * (o02m - o13m), c * (o02m + o13m)
    re = [e0 + o0, e02m + p, e2, e02m - p, e0 - o0, e02m - p, e2, e02m + p]
    im1, im3 = -e13m - q, e13m - q
    im = [None, im1, -o2, im3, None, -im3, o2, -im1]
    return re, im


def _fourier_kernel(f_ref, pos_ref, chan_ref, o_ref, t_ref, zr_ref, zi_ref, y_ref):
    r = f_ref.shape[1] // DFT_RADIX
    w = FOURIER_WIDTH
    pair = V7X_MXU_DIM
    slab = y_ref.shape[2]
    for lo in range(0, w, pair):
        for sub in range(lo, lo + pair, FOURIER_GROUP):
            lanes = slice(sub, sub + FOURIER_GROUP)
            x = [f_ref[0, n1 * r:(n1 + 1) * r, lanes].astype(F32) for n1 in range(DFT_RADIX)]
            re, im = _fold_radix8(x)
            for k1 in range(DFT_RADIX):
                t_ref[k1, 0:r, lanes] = re[k1].astype(BF16)
                t_ref[k1, r:2 * r, lanes] = (
                    jnp.zeros((r, FOURIER_GROUP), BF16) if im[k1] is None else im[k1].astype(BF16))
        h = lo // pair
        for k1 in range(DFT_RADIX):
            z = _dot(pos_ref[k1], t_ref[k1, :, lo:lo + pair])
            zr_ref[h, k1 * r:(k1 + 1) * r, :] = z[0:r].astype(BF16)
            zi_ref[h, k1 * r:(k1 + 1) * r, :] = z[r:2 * r].astype(BF16)
        rows_per_dot = (DFT_RADIX // 2) * r
        for top in range(0, DFT_RADIX * r, rows_per_dot):
            rows = slice(top, top + rows_per_dot)
            y = (_dot(zr_ref[h, rows, :], chan_ref[0:pair, :])
                 + _dot(zi_ref[h, rows, :], chan_ref[pair:2 * pair, :]))
            for k1 in range(top // r, (top + rows_per_dot) // r):
                for j in range(pair // slab):
                    y_ref[(lo + j * slab) // slab, pl.ds(k1, r, stride=DFT_RADIX), :] = (
                        y[k1 * r - top:(k1 + 1) * r - top, j * slab:(j + 1) * slab])
    for j in range(w // slab):
        o_ref[0, :, j * slab:(j + 1) * slab] = y_ref[j].astype(BF16)


def _fourier(f):
    nb, n, w = f.shape
    assert w == FOURIER_WIDTH and n % (DFT_RADIX * V7X_MXU_DIM) == 0, f.shape
    pos, chan = _dft_tables(n)
    r = n // DFT_RADIX
    blk = pl.BlockSpec((1, n, w), lambda b: (b, 0, 0))
    lanes = V7X_LANES
    return pl.pallas_call(
        _fourier_kernel,
        out_shape=jax.ShapeDtypeStruct(f.shape, BF16),
        grid=(nb,),
        in_specs=[blk, _resident(pos.shape), _resident(chan.shape)],
        out_specs=blk,
        scratch_shapes=[pltpu.VMEM((DFT_RADIX, 2 * r, w), BF16),
                        pltpu.VMEM((w // V7X_MXU_DIM, n, V7X_MXU_DIM), BF16),
                        pltpu.VMEM((w // V7X_MXU_DIM, n, V7X_MXU_DIM), BF16),
                        pltpu.VMEM((w // lanes, n, lanes), F32)],
        compiler_params=_params("parallel"),
        name="fourier",
    )(f, pos, chan)


def _merge_kernel(x_ref, mod_ref, ya_ref, yf_ref, ga_ref, gf_ref, wab_ref, wfb_ref, wo_ref, o_ref,
                  merged_ref):
    gate = mod_ref[0, 5:6, :]
    for lo in range(0, D_MODEL, V7X_MXU_DIM):
        cols = slice(lo, lo + V7X_MXU_DIM)
        merged_ref[:, cols] = (ga_ref[0, :, cols].astype(F32) * _dot(ya_ref[0], wab_ref[:, cols])
                               + gf_ref[0, :, cols].astype(F32) * _dot(yf_ref[0], wfb_ref[:, cols])
                               ).astype(BF16)
    mix = _dot(merged_ref[...], wo_ref[...])
    o_ref[0] = x_ref[0] + gate * mix


def _merge(x, mod, y_attn, y_four, sig_a, sig_f, w_ab, w_fb, w_o):
    nb, t, _ = x.shape
    tm = WIDE_TOKEN_TILE
    assert t % tm == 0, x.shape
    tok = lambda w: pl.BlockSpec((1, tm, w), lambda b, i: (b, i, 0))
    return pl.pallas_call(
        _merge_kernel,
        out_shape=jax.ShapeDtypeStruct(x.shape, F32),
        grid=(nb, t // tm),
        in_specs=[
            tok(D_MODEL),
            pl.BlockSpec((1, N_MOD, D_MODEL), lambda b, i: (b, 0, 0)),
            tok(Q_WIDTH), tok(FOURIER_WIDTH), tok(D_MODEL), tok(D_MODEL),
            _resident(w_ab.shape), _resident(w_fb.shape), _resident(w_o.shape),
        ],
        out_specs=tok(D_MODEL),
        scratch_shapes=[pltpu.VMEM((tm, D_MODEL), BF16)],
        compiler_params=_params("parallel", "parallel"),
        name="merge",
    )(x, mod, y_attn, y_four, sig_a, sig_f, w_ab, w_fb, w_o)


def _rope_tables(n_tokens):
    f32 = np.float32
    rows = n_tokens // GRID_W
    row_ids = np.repeat(np.arange(rows, dtype=f32), GRID_W)
    col_ids = np.tile(np.arange(GRID_W, dtype=f32), rows)
    inv_freq = f32(ROPE_THETA) ** (-np.arange(0, AXIS_ROPE_DIM, 2, dtype=f32) / f32(AXIS_ROPE_DIM))
    ang = np.concatenate([row_ids[:, None] * inv_freq, col_ids[:, None] * inv_freq], axis=-1)
    cos, sin = np.cos(ang), np.sin(ang)
    assert ang.dtype == f32 and cos.dtype == f32
    return (jnp.asarray(np.concatenate([cos, cos], axis=-1)),
            jnp.asarray(np.concatenate([-sin, sin], axis=-1)))


def kernel(x, c, ctx, c_ctx, w_ada, b_ada, norm_ffn1, w_ffn1_in, w_ffn1_out, norm_mix, w_in,
           q_norm, k_norm, w_attn_branch, w_fourier_branch, w_out, norm_ffn2, w_ffn2_in, w_ffn2_out):
    nb, n_lat, _ = x.shape
    n_ctx = ctx.shape[1]
    depth = w_ada.shape[0]
    assert depth == 1, "the context stream is only carried through the (single) last layer"
    assert x.shape[2] == D_MODEL and ctx.shape == (nb, n_ctx, D_MODEL) and n_lat % GRID_W == 0
    assert w_in.shape[1:] == (D_MODEL, GF_OFF + D_MODEL) and w_ffn1_in.shape[1:] == (D_MODEL, 2 * D_FF)
    assert (nb * n_ctx) % TOKEN_TILE == 0, "context tokens are processed as one flat stream of tiles"
    cos2, sin2 = _rope_tables(n_lat)
    ctx_row = nb
    pad_rows = -(nb + 1) % V7X_SUBLANES
    c_all = jnp.concatenate([c, c_ctx[None, :], jnp.zeros((pad_rows, D_MODEL), F32)], axis=0)

    i = 0
    mod = _adaln(c_all, w_ada[i], b_ada[i][None, :]).reshape(c_all.shape[0], N_MOD, D_MODEL)
    own_row, shared_row = (lambda b: b), (lambda b: ctx_row)
    x, ctx_flat = _ffn([x, ctx.reshape(1, nb * n_ctx, D_MODEL)], mod, [own_row, shared_row], 0,
                       norm_ffn1[i][None, :], w_ffn1_in[i], w_ffn1_out[i])

    q, k, vx, f, sig_a, sig_f, k_c, vx_c = _inproj(
        x, ctx_flat, mod, [own_row, shared_row], norm_mix[i][None, :], w_in[i],
        q_norm[i][None, :], k_norm[i][None, :], cos2, sin2)
    y_attn = _attention(q, k, vx, k_c.reshape(nb, n_ctx, KV_WIDTH), vx_c.reshape(nb, n_ctx, VX_WIDTH))
    y_four = _fourier(f)
    x = _merge(x, mod, y_attn, y_four, sig_a, sig_f, w_attn_branch[i], w_fourier_branch[i], w_out[i])

    return _ffn([x], mod, [own_row], 6, norm_ffn2[i][None, :], w_ffn2_in[i], w_ffn2_out[i])[0]
```

```python
import functools

import numpy as np
import jax
import jax.numpy as jnp
from jax import lax
from jax.experimental import pallas as pl
from jax.experimental.pallas import tpu as pltpu

D_MODEL = 1024
GRID_W = 64
HEAD_DIM = 128
N_Q_HEADS = D_MODEL // HEAD_DIM
N_KV_HEADS = N_Q_HEADS // 4
GQA_GROUP = N_Q_HEADS // N_KV_HEADS
Q_WIDTH = N_Q_HEADS * HEAD_DIM
KV_WIDTH = N_KV_HEADS * HEAD_DIM
FOURIER_GROUP = 128
N_FOURIER_GROUPS = 4
FOURIER_WIDTH = N_FOURIER_GROUPS * FOURIER_GROUP
D_FF = 2816
AXIS_ROPE_DIM = HEAD_DIM // 2
ROPE_THETA = 10000.0
EPS = 1e-6
N_MOD = 9
ATTN_SCALE = HEAD_DIM ** -0.5
LOG2_E = 1.4426950408889634

K_OFF = Q_WIDTH
V_OFF = K_OFF + KV_WIDTH
F_OFF = V_OFF + KV_WIDTH
GA_OFF = F_OFF + FOURIER_WIDTH
GF_OFF = GA_OFF + D_MODEL

V7X_MXU_DIM = 256
V7X_LANES = 128
V7X_SUBLANES = 8
VX_WIDTH = N_KV_HEADS * V7X_MXU_DIM
V7X_VMEM_BYTES = 64 * 2**20
VMEM_LIMIT = V7X_VMEM_BYTES - 8 * 2**20

TOKEN_TILE = 512
WIDE_TOKEN_TILE = 1024
FF_CHUNK = V7X_MXU_DIM
ATTN_Q_TILE = 1024
ADALN_TILE = 256

F32 = jnp.float32
BF16 = jnp.bfloat16


def _params(*sem):
    return pltpu.CompilerParams(dimension_semantics=sem, vmem_limit_bytes=VMEM_LIMIT)


def _resident(shape):
    return pl.BlockSpec(shape, lambda *_: (0,) * len(shape), pipeline_mode=pl.Buffered(1))


def _rms(x, gain):
    return x * lax.rsqrt(jnp.mean(x * x, axis=-1, keepdims=True) + EPS) * gain


def _dot(a, b):
    return jnp.dot(a, b.astype(BF16), preferred_element_type=F32)


def _sigmoid(x):
    return 0.5 * jnp.tanh(0.5 * x) + 0.5


def _adaln_kernel(c_ref, w_ref, b_ref, o_ref, acc_ref):
    c = c_ref[...]
    part = _dot((c * _sigmoid(c)).astype(BF16), w_ref[...])

    @pl.when(pl.program_id(0) == 0)
    def _first():
        acc_ref[...] = part + b_ref[...]

    @pl.when(pl.program_id(0) > 0)
    def _rest():
        acc_ref[...] += part

    @pl.when(pl.program_id(0) == pl.num_programs(0) - 1)
    def _emit():
        for m in range(N_MOD):
            o_ref[:, m, :] = acc_ref[:, m * D_MODEL:(m + 1) * D_MODEL]


def _adaln(c_all, w, b):
    rows, n_out = c_all.shape[0], w.shape[1]
    assert rows % V7X_SUBLANES == 0 and D_MODEL % ADALN_TILE == 0 and n_out == N_MOD * D_MODEL
    return pl.pallas_call(
        _adaln_kernel,
        out_shape=jax.ShapeDtypeStruct((rows, N_MOD, D_MODEL), F32),
        grid=(D_MODEL // ADALN_TILE,),
        in_specs=[
            pl.BlockSpec((rows, ADALN_TILE), lambda k: (0, k)),
            pl.BlockSpec((ADALN_TILE, n_out), lambda k: (k, 0)),
            pl.BlockSpec((1, n_out), lambda k: (0, 0)),
        ],
        out_specs=pl.BlockSpec((rows, N_MOD, D_MODEL), lambda k: (0, 0, 0)),
        scratch_shapes=[pltpu.VMEM((rows, n_out), F32)],
        compiler_params=_params("arbitrary"),
        name="adaln",
    )(c_all, w, b)


class _TokenStreams:
    def __init__(self, arrays, tile):
        assert all(a.ndim == 3 and a.shape[1] % tile == 0 for a in arrays), [a.shape for a in arrays]
        self.tile = tile
        self.per_batch = [a.shape[1] // tile for a in arrays]
        counts = [a.shape[0] * p for a, p in zip(arrays, self.per_batch)]
        self.starts = [sum(counts[:s]) for s in range(len(arrays) + 1)]
        self.steps = self.starts[-1]

    def batch_and_tile(self, s, t):
        local = jnp.clip(t - self.starts[s], 0, self.starts[s + 1] - self.starts[s] - 1)
        return local // self.per_batch[s], local % self.per_batch[s]

    def token_spec(self, s, width):
        def index(t):
            b, i = self.batch_and_tile(s, t)
            return b, i, 0
        return pl.BlockSpec((1, self.tile, width), index)

    def mod_spec(self, mod_rows):
        def index(t):
            row = mod_rows[0](self.batch_and_tile(0, t)[0])
            for s in range(1, len(mod_rows)):
                row = jnp.where(t >= self.starts[s], mod_rows[s](self.batch_and_tile(s, t)[0]), row)
            return row, 0, 0
        return pl.BlockSpec((1, N_MOD, D_MODEL), index)

    def active(self, s):
        t = pl.program_id(0)
        return jnp.logical_and(t >= self.starts[s], t < self.starts[s + 1])


def _ffn_kernel(*refs, streams, mod_base):
    n = len(streams.per_batch)
    x_refs, (mod_ref, gain_ref, win_ref, wout_ref), o_refs, act_ref = (
        refs[:n], refs[n:n + 4], refs[n + 4:2 * n + 4], refs[2 * n + 4])

    def half_step(x_ref, o_ref):
        x = x_ref[0]
        shift = mod_ref[0, mod_base:mod_base + 1, :]
        scale = mod_ref[0, mod_base + 1:mod_base + 2, :]
        gate = mod_ref[0, mod_base + 2:mod_base + 3, :]
        h = (_rms(x, gain_ref[...]) * (1.0 + scale) + shift).astype(BF16)
        for j in range(D_FF // FF_CHUNK):
            lo = j * FF_CHUNK
            g = _dot(h, win_ref[:, lo:lo + FF_CHUNK])
            u = _dot(h, win_ref[:, D_FF + lo:D_FF + lo + FF_CHUNK])
            act_ref[:, lo:lo + FF_CHUNK] = (g * _sigmoid(g) * u).astype(BF16)
        y = _dot(act_ref[...], wout_ref[...])
        o_ref[0] = x + 0.5 * gate * y

    if n == 1:
        half_step(x_refs[0], o_refs[0])
    else:
        for s in range(n):
            pl.when(streams.active(s))(functools.partial(half_step, x_refs[s], o_refs[s]))


def _ffn(xs, mod, mod_rows, mod_base, gain, w_in, w_out):
    tile = TOKEN_TILE
    streams = _TokenStreams(xs, tile)
    tok = [streams.token_spec(s, D_MODEL) for s in range(len(xs))]
    return pl.pallas_call(
        functools.partial(_ffn_kernel, streams=streams, mod_base=mod_base),
        out_shape=[jax.ShapeDtypeStruct(x.shape, F32) for x in xs],
        grid=(streams.steps,),
        in_specs=tok + [
            streams.mod_spec(mod_rows),
            _resident((1, D_MODEL)),
            _resident((D_MODEL, 2 * D_FF)),
            _resident((D_FF, D_MODEL)),
        ],
        out_specs=tok,
        scratch_shapes=[pltpu.VMEM((tile, D_FF), BF16)],
        compiler_params=_params("arbitrary"),
        name="ffn",
    )(*xs, mod, gain, w_in, w_out)


def _head_norm(t, gain):
    return t * lax.rsqrt(jnp.mean(t * t, axis=-1, keepdims=True) + EPS) * gain


def _rope(t, cos2, sin2):
    return t * cos2 + pltpu.roll(t, AXIS_ROPE_DIM, 1) * sin2


def _store_widened_values(v_ref, v):
    ones = jnp.ones((v.shape[0], V7X_MXU_DIM - HEAD_DIM), BF16)
    for hd in range(N_KV_HEADS):
        lo = hd * V7X_MXU_DIM
        v_ref[0, :, lo:lo + HEAD_DIM] = v[:, hd * HEAD_DIM:(hd + 1) * HEAD_DIM].astype(BF16)
        v_ref[0, :, lo + HEAD_DIM:lo + V7X_MXU_DIM] = ones


def _inproj_kernel(x_ref, ctx_ref, mod_ref, gain_ref, w_ref, qn_ref, kn_ref, cos_ref, sin_ref,
                   q_ref, k_ref, v_ref, f_ref, ga_ref, gf_ref, kc_ref, vc_ref, *, streams):
    kn = kn_ref[...]
    heads_per_dot = V7X_MXU_DIM // HEAD_DIM

    def normed_input(tok_ref):
        shift = mod_ref[0, 3:4, :]
        scale = mod_ref[0, 4:5, :]
        return (_rms(tok_ref[0], gain_ref[...]) * (1.0 + scale) + shift).astype(BF16)

    @pl.when(streams.active(0))
    def _latent_tokens():
        h = normed_input(x_ref)

        def wide_dot(col):
            return _dot(h, w_ref[:, col:col + V7X_MXU_DIM])

        cos2 = cos_ref[...]
        sin2 = sin_ref[...]
        qn = qn_ref[...] * (ATTN_SCALE * LOG2_E)
        for c in range(Q_WIDTH // V7X_MXU_DIM):
            t = wide_dot(c * V7X_MXU_DIM)
            for j in range(heads_per_dot):
                tj = t[:, j * HEAD_DIM:(j + 1) * HEAD_DIM]
                q_ref[0, c * heads_per_dot + j] = _rope(_head_norm(tj, qn), cos2, sin2).astype(BF16)
        for c in range(KV_WIDTH // V7X_MXU_DIM):
            t = wide_dot(K_OFF + c * V7X_MXU_DIM)
            for j in range(heads_per_dot):
                lo = c * V7X_MXU_DIM + j * HEAD_DIM
                tj = t[:, j * HEAD_DIM:(j + 1) * HEAD_DIM]
                k_ref[0, :, lo:lo + HEAD_DIM] = _rope(_head_norm(tj, kn), cos2, sin2).astype(BF16)
        _store_widened_values(v_ref, _dot(h, w_ref[:, V_OFF:V_OFF + KV_WIDTH]))
        for c in range(FOURIER_WIDTH // V7X_MXU_DIM):
            lo = c * V7X_MXU_DIM
            f_ref[0, :, lo:lo + V7X_MXU_DIM] = wide_dot(F_OFF + lo).astype(BF16)
        for c in range(D_MODEL // V7X_MXU_DIM):
            lo = c * V7X_MXU_DIM
            ga_ref[0, :, lo:lo + V7X_MXU_DIM] = _sigmoid(wide_dot(GA_OFF + lo)).astype(BF16)
            gf_ref[0, :, lo:lo + V7X_MXU_DIM] = _sigmoid(wide_dot(GF_OFF + lo)).astype(BF16)

    @pl.when(streams.active(1))
    def _context_tokens():
        h = normed_input(ctx_ref)
        t = _dot(h, w_ref[:, K_OFF:K_OFF + KV_WIDTH])
        for hd in range(N_KV_HEADS):
            lo = hd * HEAD_DIM
            kc_ref[0, :, lo:lo + HEAD_DIM] = _head_norm(t[:, lo:lo + HEAD_DIM], kn).astype(BF16)
        _store_widened_values(vc_ref, _dot(h, w_ref[:, V_OFF:V_OFF + KV_WIDTH]))


def _inproj(x, ctx, mod, mod_rows, gain, w_in, q_norm, k_norm, cos2, sin2):
    nb, t, _ = x.shape
    streams = _TokenStreams([x, ctx], TOKEN_TILE)
    tm = streams.tile
    tok = lambda w: streams.token_spec(0, w)
    ctx_tok = lambda w: streams.token_spec(1, w)
    shp = lambda w: jax.ShapeDtypeStruct((nb, t, w), BF16)
    ctx_shp = lambda w: jax.ShapeDtypeStruct(ctx.shape[:2] + (w,), BF16)

    def q_index(step):
        b, i = streams.batch_and_tile(0, step)
        return b, 0, i, 0

    def rope_index(step):
        return streams.batch_and_tile(0, step)[1], 0

    return pl.pallas_call(
        functools.partial(_inproj_kernel, streams=streams),
        out_shape=[jax.ShapeDtypeStruct((nb, N_Q_HEADS, t, HEAD_DIM), BF16), shp(KV_WIDTH),
                   shp(VX_WIDTH), shp(FOURIER_WIDTH), shp(D_MODEL), shp(D_MODEL),
                   ctx_shp(KV_WIDTH), ctx_shp(VX_WIDTH)],
        grid=(streams.steps,),
        in_specs=[
            tok(D_MODEL),
            ctx_tok(D_MODEL),
            streams.mod_spec(mod_rows),
            _resident((1, D_MODEL)),
            _resident(w_in.shape),
            _resident((1, HEAD_DIM)),
            _resident((1, HEAD_DIM)),
            pl.BlockSpec((tm, HEAD_DIM), rope_index),
            pl.BlockSpec((tm, HEAD_DIM), rope_index),
        ],
        out_specs=[pl.BlockSpec((1, N_Q_HEADS, tm, HEAD_DIM), q_index),
                   tok(KV_WIDTH), tok(VX_WIDTH), tok(FOURIER_WIDTH), tok(D_MODEL), tok(D_MODEL),
                   ctx_tok(KV_WIDTH), ctx_tok(VX_WIDTH)],
        compiler_params=_params("arbitrary"),
        name="inproj",
    )(x, ctx, mod, gain, w_in, q_norm, k_norm, cos2, sin2)


_NT = (((1,), (1,)), ((), ()))


ATTN_SCRATCH_PER_HEAD = 3
ATTN_SCORE_BUFFERS = 3


def _attn_kernel(q_ref, kc_ref, k_ref, vcx_ref, vx_ref, o_ref, *scratch):
    n_head_refs = ATTN_SCRATCH_PER_HEAD * GQA_GROUP
    heads = [scratch[ATTN_SCRATCH_PER_HEAD * g:ATTN_SCRATCH_PER_HEAD * (g + 1)] for g in range(GQA_GROUP)]
    scores = [scratch[n_head_refs + 2 * b:n_head_refs + 2 * b + 2] for b in range(ATTN_SCORE_BUFFERS)]
    score_refs = lambda g: scores[g % ATTN_SCORE_BUFFERS]

    @pl.when(pl.program_id(0) == 0)
    def _no_previous_item():
        for pc_ref, pl_ref, _ in heads:
            pc_ref[...] = jnp.ones(pc_ref.shape, BF16)
            pl_ref[...] = jnp.ones(pl_ref.shape, BF16)
        for ref in (*score_refs(GQA_GROUP - 1), heads[-1][2]):
            ref[...] = jnp.zeros(ref.shape, F32)

    def weights_from_scores(g):
        pc_ref, pl_ref, m_ref = heads[g]
        m = m_ref[...]
        for s_ref, p_ref in zip(score_refs(g), (pc_ref, pl_ref)):
            for lo in range(0, s_ref.shape[1], V7X_MXU_DIM):
                p_ref[:, lo:lo + V7X_MXU_DIM] = jnp.exp2(s_ref[:, lo:lo + V7X_MXU_DIM] - m).astype(BF16)

    def values_then_scores(g):
        pc_ref, pl_ref, m_ref = heads[g]
        sc_ref, sl_ref = score_refs(g)
        oe = _dot(pc_ref[...], vcx_ref[0]) + _dot(pl_ref[...], vx_ref[0])
        o_ref[0, :, g * HEAD_DIM:(g + 1) * HEAD_DIM] = (oe[:, :HEAD_DIM] / oe[:, HEAD_DIM:]).astype(BF16)

        q = q_ref[0, g]
        s_c = lax.dot_general(q, kc_ref[0], _NT, preferred_element_type=F32)
        s_l = lax.dot_general(q, k_ref[0], _NT, preferred_element_type=F32)
        m_ref[...] = jnp.maximum(jnp.max(s_c, axis=-1, keepdims=True), jnp.max(s_l, axis=-1, keepdims=True))
        sc_ref[...] = s_c
        sl_ref[...] = s_l

    last = GQA_GROUP - 1
    weights_from_scores(last)
    for g in range(GQA_GROUP):
        if g >= ATTN_SCORE_BUFFERS:
            weights_from_scores(g - ATTN_SCORE_BUFFERS)
        values_then_scores(g)
    for g in range(GQA_GROUP - ATTN_SCORE_BUFFERS, last):
        weights_from_scores(g)


def _attention(q, k, vx, k_c, vx_c):
    nb, _, n, _ = q.shape
    n_ctx = k_c.shape[1]
    tq = ATTN_Q_TILE
    assert n % tq == 0 and n % V7X_MXU_DIM == 0 and n_ctx % V7X_MXU_DIM == 0, (n, n_ctx)
    tiles = n // tq
    items = nb * N_KV_HEADS * tiles

    def item(t):
        return t // (N_KV_HEADS * tiles), (t // tiles) % N_KV_HEADS, t % tiles

    def scored(t):
        return item(jnp.minimum(t, items - 1))

    def weighted(t):
        return item(jnp.maximum(t - 1, 0))

    def q_map(t):
        b, h, i = scored(t)
        return b, h, i, 0

    def k_map(t):
        b, h, _ = scored(t)
        return b, 0, h

    def v_map(t):
        b, h, _ = weighted(t)
        return b, 0, h

    def o_map(t):
        b, h, i = weighted(t)
        return b, i, h

    return pl.pallas_call(
        _attn_kernel,
        out_shape=jax.ShapeDtypeStruct((nb, n, Q_WIDTH), BF16),
        grid=(items + 1,),
        in_specs=[
            pl.BlockSpec((1, GQA_GROUP, tq, HEAD_DIM), q_map),
            pl.BlockSpec((1, n_ctx, HEAD_DIM), k_map),
            pl.BlockSpec((1, n, HEAD_DIM), k_map),
            pl.BlockSpec((1, n_ctx, V7X_MXU_DIM), v_map),
            pl.BlockSpec((1, n, V7X_MXU_DIM), v_map),
        ],
        out_specs=pl.BlockSpec((1, tq, GQA_GROUP * HEAD_DIM), o_map),
        scratch_shapes=(
            GQA_GROUP * [pltpu.VMEM((tq, n_ctx), BF16), pltpu.VMEM((tq, n), BF16), pltpu.VMEM((tq, 1), F32)]
            + ATTN_SCORE_BUFFERS * [pltpu.VMEM((tq, n_ctx), F32), pltpu.VMEM((tq, n), F32)]),
        compiler_params=pltpu.CompilerParams(dimension_semantics=("arbitrary",),
                                             vmem_limit_bytes=V7X_VMEM_BYTES - 3 * 2**20),
        name="attn",
    )(q, k_c, k, vx_c, vx)


DFT_RADIX = 8


def _dft_tables(n):
    r = n // DFT_RADIX
    k = np.arange(DFT_RADIX)[:, None, None] + DFT_RADIX * np.arange(r)[None, :, None]
    ang = 2.0 * np.pi * ((k * np.arange(r)[None, None, :]) % n) / n
    cos, sin = np.cos(ang), np.sin(ang)
    pos = np.concatenate([np.concatenate([cos, sin], axis=2),
                          np.concatenate([-sin, cos], axis=2)], axis=1)
    c = FOURIER_GROUP
    scale = 1.0 / np.sqrt(float(n * c))
    ang_c = 2.0 * np.pi * (np.outer(np.arange(c), np.arange(c)) % c) / c
    eye = np.eye(V7X_MXU_DIM // c)
    chan = np.concatenate([np.kron(eye, np.cos(ang_c) * scale),
                           np.kron(eye, np.sin(ang_c) * scale)], axis=0)
    return jnp.asarray(pos, F32).astype(BF16), jnp.asarray(chan, F32).astype(BF16)


def _fold_radix8(x):
    c = 0.5 ** 0.5
    e02p, e02m, e13p, e13m = x[0] + x[4], x[0] - x[4], x[2] + x[6], x[2] - x[6]
    o02p, o02m, o13p, o13m = x[1] + x[5], x[1] - x[5], x[3] + x[7], x[3] - x[7]
    e0, e2, o0, o2 = e02p + e13p, e02p - e13p, o02p + o13p, o02p - o13p
    p, q = c * (o02m - o13m), c * (o02m + o13m)
    re = [e0 + o0, e02m + p, e2, e02m - p, e0 - o0, e02m - p, e2, e02m + p]
    im1, im3 = -e13m - q, e13m - q
    im = [None, im1, -o2, im3, None, -im3, o2, -im1]
    return re, im


def _fourier_kernel(f_ref, pos_ref, chan_ref, o_ref, t_ref, zr_ref, zi_ref, y_ref):
    r = f_ref.shape[1] // DFT_RADIX
    w = FOURIER_WIDTH
    pair = V7X_MXU_DIM
    slab = y_ref.shape[2]
    for lo in range(0, w, pair):
        for sub in range(lo, lo + pair, FOURIER_GROUP):
            lanes = slice(sub, sub + FOURIER_GROUP)
            x = [f_ref[0, n1 * r:(n1 + 1) * r, lanes].astype(F32) for n1 in range(DFT_RADIX)]
            re, im = _fold_radix8(x)
            for k1 in range(DFT_RADIX):
                t_ref[k1, 0:r, lanes] = re[k1].astype(BF16)
                t_ref[k1, r:2 * r, lanes] = (
                    jnp.zeros((r, FOURIER_GROUP), BF16) if im[k1] is None else im[k1].astype(BF16))
        h = lo // pair
        for k1 in range(DFT_RADIX):
            z = _dot(pos_ref[k1], t_ref[k1, :, lo:lo + pair])
            zr_ref[h, k1 * r:(k1 + 1) * r, :] = z[0:r].astype(BF16)
            zi_ref[h, k1 * r:(k1 + 1) * r, :] = z[r:2 * r].astype(BF16)
        rows_per_dot = (DFT_RADIX // 2) * r
        for top in range(0, DFT_RADIX * r, rows_per_dot):
            rows = slice(top, top + rows_per_dot)
            y = (_dot(zr_ref[h, rows, :], chan_ref[0:pair, :])
                 + _dot(zi_ref[h, rows, :], chan_ref[pair:2 * pair, :]))
            for k1 in range(top // r, (top + rows_per_dot) // r):
                for j in range(pair // slab):
                    y_ref[(lo + j * slab) // slab, pl.ds(k1, r, stride=DFT_RADIX), :] = (
                        y[k1 * r - top:(k1 + 1) * r - top, j * slab:(j + 1) * slab])
    for j in range(w // slab):
        o_ref[0, :, j * slab:(j + 1) * slab] = y_ref[j].astype(BF16)


def _fourier(f):
    nb, n, w = f.shape
    assert w == FOURIER_WIDTH and n % (DFT_RADIX * V7X_MXU_DIM) == 0, f.shape
    pos, chan = _dft_tables(n)
    r = n // DFT_RADIX
    blk = pl.BlockSpec((1, n, w), lambda b: (b, 0, 0))
    lanes = V7X_LANES
    return pl.pallas_call(
        _fourier_kernel,
        out_shape=jax.ShapeDtypeStruct(f.shape, BF16),
        grid=(nb,),
        in_specs=[blk, _resident(pos.shape), _resident(chan.shape)],
        out_specs=blk,
        scratch_shapes=[pltpu.VMEM((DFT_RADIX, 2 * r, w), BF16),
                        pltpu.VMEM((w // V7X_MXU_DIM, n, V7X_MXU_DIM), BF16),
                        pltpu.VMEM((w // V7X_MXU_DIM, n, V7X_MXU_DIM), BF16),
                        pltpu.VMEM((w // lanes, n, lanes), F32)],
        compiler_params=_params("parallel"),
        name="fourier",
    )(f, pos, chan)


def _merge_kernel(x_ref, mod_ref, ya_ref, yf_ref, ga_ref, gf_ref, wab_ref, wfb_ref, wo_ref, o_ref,
                  merged_ref):
    gate = mod_ref[0, 5:6, :]
    for lo in range(0, D_MODEL, V7X_MXU_DIM):
        cols = slice(lo, lo + V7X_MXU_DIM)
        merged_ref[:, cols] = (ga_ref[0, :, cols].astype(F32) * _dot(ya_ref[0], wab_ref[:, cols])
                               + gf_ref[0, :, cols].astype(F32) * _dot(yf_ref[0], wfb_ref[:, cols])
                               ).astype(BF16)
    mix = _dot(merged_ref[...], wo_ref[...])
    o_ref[0] = x_ref[0] + gate * mix


def _merge(x, mod, y_attn, y_four, sig_a, sig_f, w_ab, w_fb, w_o):
    nb, t, _ = x.shape
    tm = WIDE_TOKEN_TILE
    assert t % tm == 0, x.shape
    tok = lambda w: pl.BlockSpec((1, tm, w), lambda b, i: (b, i, 0))
    return pl.pallas_call(
        _merge_kernel,
        out_shape=jax.ShapeDtypeStruct(x.shape, F32),
        grid=(nb, t // tm),
        in_specs=[
            tok(D_MODEL),
            pl.BlockSpec((1, N_MOD, D_MODEL), lambda b, i: (b, 0, 0)),
            tok(Q_WIDTH), tok(FOURIER_WIDTH), tok(D_MODEL), tok(D_MODEL),
            _resident(w_ab.shape), _resident(w_fb.shape), _resident(w_o.shape),
        ],
        out_specs=tok(D_MODEL),
        scratch_shapes=[pltpu.VMEM((tm, D_MODEL), BF16)],
        compiler_params=_params("parallel", "parallel"),
        name="merge",
    )(x, mod, y_attn, y_four, sig_a, sig_f, w_ab, w_fb, w_o)


def _rope_tables(n_tokens):
    f32 = np.float32
    rows = n_tokens // GRID_W
    row_ids = np.repeat(np.arange(rows, dtype=f32), GRID_W)
    col_ids = np.tile(np.arange(GRID_W, dtype=f32), rows)
    inv_freq = f32(ROPE_THETA) ** (-np.arange(0, AXIS_ROPE_DIM, 2, dtype=f32) / f32(AXIS_ROPE_DIM))
    ang = np.concatenate([row_ids[:, None] * inv_freq, col_ids[:, None] * inv_freq], axis=-1)
    cos, sin = np.cos(ang), np.sin(ang)
    assert ang.dtype == f32 and cos.dtype == f32
    return (jnp.asarray(np.concatenate([cos, cos], axis=-1)),
            jnp.asarray(np.concatenate([-sin, sin], axis=-1)))


def kernel(x, c, ctx, c_ctx, w_ada, b_ada, norm_ffn1, w_ffn1_in, w_ffn1_out, norm_mix, w_in,
           q_norm, k_norm, w_attn_branch, w_fourier_branch, w_out, norm_ffn2, w_ffn2_in, w_ffn2_out):
    nb, n_lat, _ = x.shape
    n_ctx = ctx.shape[1]
    depth = w_ada.shape[0]
    assert depth == 1, "the context stream is only carried through the (single) last layer"
    assert x.shape[2] == D_MODEL and ctx.shape == (nb, n_ctx, D_MODEL) and n_lat % GRID_W == 0
    assert w_in.shape[1:] == (D_MODEL, GF_OFF + D_MODEL) and w_ffn1_in.shape[1:] == (D_MODEL, 2 * D_FF)
    assert (nb * n_ctx) % TOKEN_TILE == 0, "context tokens are processed as one flat stream of tiles"
    cos2, sin2 = _rope_tables(n_lat)
    ctx_row = nb
    pad_rows = -(nb + 1) % V7X_SUBLANES
    c_all = jnp.concatenate([c, c_ctx[None, :], jnp.zeros((pad_rows, D_MODEL), F32)], axis=0)

    i = 0
    mod = _adaln(c_all, w_ada[i], b_ada[i][None, :])
    own_row, shared_row = (lambda b: b), (lambda b: ctx_row)
    x, ctx_flat = _ffn([x, ctx.reshape(1, nb * n_ctx, D_MODEL)], mod, [own_row, shared_row], 0,
                       norm_ffn1[i][None, :], w_ffn1_in[i], w_ffn1_out[i])

    q, k, vx, f, sig_a, sig_f, k_c, vx_c = _inproj(
        x, ctx_flat, mod, [own_row, shared_row], norm_mix[i][None, :], w_in[i],
        q_norm[i][None, :], k_norm[i][None, :], cos2, sin2)
    y_attn = _attention(q, k, vx, k_c.reshape(nb, n_ctx, KV_WIDTH), vx_c.reshape(nb, n_ctx, VX_WIDTH))
    y_four = _fourier(f)
    x = _merge(x, mod, y_attn, y_four, sig_a, sig_f, w_attn_branch[i], w_fourier_branch[i], w_out[i])

    return _ffn([x], mod, [own_row], 6, norm_ffn2[i][None, :], w_ffn2_in[i], w_ffn2_out[i])[0]
```

```python
import functools

import numpy as np
import jax
import jax.numpy as jnp
from jax import lax
from jax.experimental import pallas as pl
from jax.experimental.pallas import tpu as pltpu

D_MODEL = 1024
GRID_W = 64
HEAD_DIM = 128
N_Q_HEADS = D_MODEL // HEAD_DIM
N_KV_HEADS = N_Q_HEADS // 4
GQA_GROUP = N_Q_HEADS // N_KV_HEADS
Q_WIDTH = N_Q_HEADS * HEAD_DIM
KV_WIDTH = N_KV_HEADS * HEAD_DIM
FOURIER_GROUP = 128
N_FOURIER_GROUPS = 4
FOURIER_WIDTH = N_FOURIER_GROUPS * FOURIER_GROUP
D_FF = 2816
AXIS_ROPE_DIM = HEAD_DIM // 2
ROPE_THETA = 10000.0
EPS = 1e-6
N_MOD = 9
ATTN_SCALE = HEAD_DIM ** -0.5
LOG2_E = 1.4426950408889634

K_OFF = Q_WIDTH
V_OFF = K_OFF + KV_WIDTH
F_OFF = V_OFF + KV_WIDTH
GA_OFF = F_OFF + FOURIER_WIDTH
GF_OFF = GA_OFF + D_MODEL

V7X_MXU_DIM = 256
V7X_LANES = 128
V7X_SUBLANES = 8
VX_WIDTH = N_KV_HEADS * V7X_MXU_DIM
V7X_VMEM_BYTES = 64 * 2**20
VMEM_LIMIT = V7X_VMEM_BYTES - 8 * 2**20

TOKEN_TILE = 512
WIDE_TOKEN_TILE = 1024
FF_CHUNK = V7X_MXU_DIM
ATTN_Q_TILE = 1024
ADALN_TILE = 256

F32 = jnp.float32
BF16 = jnp.bfloat16


def _params(*sem):
    return pltpu.CompilerParams(dimension_semantics=sem, vmem_limit_bytes=VMEM_LIMIT)


def _resident(shape):
    return pl.BlockSpec(shape, lambda *_: (0,) * len(shape), pipeline_mode=pl.Buffered(1))


def _rms(x, gain):
    return x * lax.rsqrt(jnp.mean(x * x, axis=-1, keepdims=True) + EPS) * gain


def _dot(a, b):
    return jnp.dot(a, b.astype(BF16), preferred_element_type=F32)


def _sigmoid(x):
    return 0.5 * jnp.tanh(0.5 * x) + 0.5


def _adaln_kernel(c_ref, cctx_ref, w_ref, b_ref, o_ref, acc_ref):
    c, c_ctx = c_ref[...], cctx_ref[...]
    tail_rows = acc_ref.shape[0] - c.shape[0]
    first_tail_row = lax.broadcasted_iota(jnp.int32, (tail_rows, c.shape[1]), 0) == 0
    silu = lambda v: v * _sigmoid(v)
    a = jnp.concatenate([silu(c), jnp.where(first_tail_row, silu(c_ctx), 0.0)], axis=0)
    part = _dot(a.astype(BF16), w_ref[...])

    @pl.when(pl.program_id(0) == 0)
    def _first():
        acc_ref[...] = part + b_ref[...]

    @pl.when(pl.program_id(0) > 0)
    def _rest():
        acc_ref[...] += part

    @pl.when(pl.program_id(0) == pl.num_programs(0) - 1)
    def _emit():
        for m in range(N_MOD):
            o_ref[:, m, :] = acc_ref[:, m * D_MODEL:(m + 1) * D_MODEL]


def _adaln(c, c_ctx, w, b):
    nb, n_out = c.shape[0], w.shape[1]
    assert nb % V7X_SUBLANES == 0 and D_MODEL % ADALN_TILE == 0 and n_out == N_MOD * D_MODEL
    rows = nb + V7X_SUBLANES
    return pl.pallas_call(
        _adaln_kernel,
        out_shape=jax.ShapeDtypeStruct((rows, N_MOD, D_MODEL), F32),
        grid=(D_MODEL // ADALN_TILE,),
        in_specs=[
            pl.BlockSpec((nb, ADALN_TILE), lambda k: (0, k)),
            pl.BlockSpec((1, ADALN_TILE), lambda k: (0, k)),
            pl.BlockSpec((ADALN_TILE, n_out), lambda k: (k, 0)),
            pl.BlockSpec((1, n_out), lambda k: (0, 0)),
        ],
        out_specs=pl.BlockSpec((rows, N_MOD, D_MODEL), lambda k: (0, 0, 0)),
        scratch_shapes=[pltpu.VMEM((rows, n_out), F32)],
        compiler_params=_params("arbitrary"),
        name="adaln",
    )(c, c_ctx[None, :], w, b)


class _TokenStreams:
    def __init__(self, arrays, tile):
        assert all(a.ndim == 3 and a.shape[1] % tile == 0 for a in arrays), [a.shape for a in arrays]
        self.tile = tile
        self.per_batch = [a.shape[1] // tile for a in arrays]
        counts = [a.shape[0] * p for a, p in zip(arrays, self.per_batch)]
        self.starts = [sum(counts[:s]) for s in range(len(arrays) + 1)]
        self.steps = self.starts[-1]

    def batch_and_tile(self, s, t):
        local = jnp.clip(t - self.starts[s], 0, self.starts[s + 1] - self.starts[s] - 1)
        return local // self.per_batch[s], local % self.per_batch[s]

    def token_spec(self, s, width):
        def index(t):
            b, i = self.batch_and_tile(s, t)
            return b, i, 0
        return pl.BlockSpec((1, self.tile, width), index)

    def mod_spec(self, mod_rows):
        def index(t):
            row = mod_rows[0](self.batch_and_tile(0, t)[0])
            for s in range(1, len(mod_rows)):
                row = jnp.where(t >= self.starts[s], mod_rows[s](self.batch_and_tile(s, t)[0]), row)
            return row, 0, 0
        return pl.BlockSpec((1, N_MOD, D_MODEL), index)

    def active(self, s):
        t = pl.program_id(0)
        return jnp.logical_and(t >= self.starts[s], t < self.starts[s + 1])


def _ffn_kernel(*refs, streams, mod_base):
    n = len(streams.per_batch)
    x_refs, (mod_ref, gain_ref, win_ref, wout_ref), o_refs, act_ref = (
        refs[:n], refs[n:n + 4], refs[n + 4:2 * n + 4], refs[2 * n + 4])

    def half_step(x_ref, o_ref):
        x = x_ref[0]
        shift = mod_ref[0, mod_base:mod_base + 1, :]
        scale = mod_ref[0, mod_base + 1:mod_base + 2, :]
        gate = mod_ref[0, mod_base + 2:mod_base + 3, :]
        h = (_rms(x, gain_ref[...]) * (1.0 + scale) + shift).astype(BF16)
        for j in range(D_FF // FF_CHUNK):
            lo = j * FF_CHUNK
            g = _dot(h, win_ref[:, lo:lo + FF_CHUNK])
            u = _dot(h, win_ref[:, D_FF + lo:D_FF + lo + FF_CHUNK])
            act_ref[:, lo:lo + FF_CHUNK] = (g * _sigmoid(g) * u).astype(BF16)
        y = _dot(act_ref[...], wout_ref[...])
        o_ref[0] = x + 0.5 * gate * y

    if n == 1:
        half_step(x_refs[0], o_refs[0])
    else:
        for s in range(n):
            pl.when(streams.active(s))(functools.partial(half_step, x_refs[s], o_refs[s]))


def _ffn(xs, mod, mod_rows, mod_base, gain, w_in, w_out):
    tile = TOKEN_TILE
    streams = _TokenStreams(xs, tile)
    tok = [streams.token_spec(s, D_MODEL) for s in range(len(xs))]
    return pl.pallas_call(
        functools.partial(_ffn_kernel, streams=streams, mod_base=mod_base),
        out_shape=[jax.ShapeDtypeStruct(x.shape, F32) for x in xs],
        grid=(streams.steps,),
        in_specs=tok + [
            streams.mod_spec(mod_rows),
            _resident((1, D_MODEL)),
            _resident((D_MODEL, 2 * D_FF)),
            _resident((D_FF, D_MODEL)),
        ],
        out_specs=tok,
        scratch_shapes=[pltpu.VMEM((tile, D_FF), BF16)],
        compiler_params=_params("arbitrary"),
        name="ffn",
    )(*xs, mod, gain, w_in, w_out)


def _head_norm(t, gain):
    return t * lax.rsqrt(jnp.mean(t * t, axis=-1, keepdims=True) + EPS) * gain


def _rope(t, cos2, sin2):
    return t * cos2 + pltpu.roll(t, AXIS_ROPE_DIM, 1) * sin2


def _store_widened_values(v_ref, v):
    ones = jnp.ones((v.shape[0], V7X_MXU_DIM - HEAD_DIM), BF16)
    for hd in range(N_KV_HEADS):
        lo = hd * V7X_MXU_DIM
        v_ref[0, :, lo:lo + HEAD_DIM] = v[:, hd * HEAD_DIM:(hd + 1) * HEAD_DIM].astype(BF16)
        v_ref[0, :, lo + HEAD_DIM:lo + V7X_MXU_DIM] = ones


def _inproj_kernel(x_ref, ctx_ref, mod_ref, gain_ref, w_ref, qn_ref, kn_ref, cos_ref, sin_ref,
                   q_ref, k_ref, v_ref, f_ref, ga_ref, gf_ref, kc_ref, vc_ref, *, streams):
    kn = kn_ref[...]
    heads_per_dot = V7X_MXU_DIM // HEAD_DIM

    def normed_input(tok_ref):
        shift = mod_ref[0, 3:4, :]
        scale = mod_ref[0, 4:5, :]
        return (_rms(tok_ref[0], gain_ref[...]) * (1.0 + scale) + shift).astype(BF16)

    @pl.when(streams.active(0))
    def _latent_tokens():
        h = normed_input(x_ref)

        def wide_dot(col):
            return _dot(h, w_ref[:, col:col + V7X_MXU_DIM])

        cos2 = cos_ref[...]
        sin2 = sin_ref[...]
        qn = qn_ref[...] * (ATTN_SCALE * LOG2_E)
        for c in range(Q_WIDTH // V7X_MXU_DIM):
            t = wide_dot(c * V7X_MXU_DIM)
            for j in range(heads_per_dot):
                tj = t[:, j * HEAD_DIM:(j + 1) * HEAD_DIM]
                q_ref[0, c * heads_per_dot + j] = _rope(_head_norm(tj, qn), cos2, sin2).astype(BF16)
        for c in range(KV_WIDTH // V7X_MXU_DIM):
            t = wide_dot(K_OFF + c * V7X_MXU_DIM)
            for j in range(heads_per_dot):
                lo = c * V7X_MXU_DIM + j * HEAD_DIM
                tj = t[:, j * HEAD_DIM:(j + 1) * HEAD_DIM]
                k_ref[0, :, lo:lo + HEAD_DIM] = _rope(_head_norm(tj, kn), cos2, sin2).astype(BF16)
        _store_widened_values(v_ref, _dot(h, w_ref[:, V_OFF:V_OFF + KV_WIDTH]))
        for c in range(FOURIER_WIDTH // V7X_MXU_DIM):
            lo = c * V7X_MXU_DIM
            f_ref[0, :, lo:lo + V7X_MXU_DIM] = wide_dot(F_OFF + lo).astype(BF16)
        for c in range(D_MODEL // V7X_MXU_DIM):
            lo = c * V7X_MXU_DIM
            ga_ref[0, :, lo:lo + V7X_MXU_DIM] = _sigmoid(wide_dot(GA_OFF + lo)).astype(BF16)
            gf_ref[0, :, lo:lo + V7X_MXU_DIM] = _sigmoid(wide_dot(GF_OFF + lo)).astype(BF16)

    @pl.when(streams.active(1))
    def _context_tokens():
        h = normed_input(ctx_ref)
        t = _dot(h, w_ref[:, K_OFF:K_OFF + KV_WIDTH])
        for hd in range(N_KV_HEADS):
            lo = hd * HEAD_DIM
            kc_ref[0, :, lo:lo + HEAD_DIM] = _head_norm(t[:, lo:lo + HEAD_DIM], kn).astype(BF16)
        _store_widened_values(vc_ref, _dot(h, w_ref[:, V_OFF:V_OFF + KV_WIDTH]))


def _inproj(x, ctx, mod, mod_rows, gain, w_in, q_norm, k_norm, cos2, sin2):
    nb, t, _ = x.shape
    streams = _TokenStreams([x, ctx], TOKEN_TILE)
    tm = streams.tile
    tok = lambda w: streams.token_spec(0, w)
    ctx_tok = lambda w: streams.token_spec(1, w)
    shp = lambda w: jax.ShapeDtypeStruct((nb, t, w), BF16)
    ctx_shp = lambda w: jax.ShapeDtypeStruct(ctx.shape[:2] + (w,), BF16)

    def q_index(step):
        b, i = streams.batch_and_tile(0, step)
        return b, 0, i, 0

    def rope_index(step):
        return streams.batch_and_tile(0, step)[1], 0

    return pl.pallas_call(
        functools.partial(_inproj_kernel, streams=streams),
        out_shape=[jax.ShapeDtypeStruct((nb, N_Q_HEADS, t, HEAD_DIM), BF16), shp(KV_WIDTH),
                   shp(VX_WIDTH), shp(FOURIER_WIDTH), shp(D_MODEL), shp(D_MODEL),
                   ctx_shp(KV_WIDTH), ctx_shp(VX_WIDTH)],
        grid=(streams.steps,),
        in_specs=[
            tok(D_MODEL),
            ctx_tok(D_MODEL),
            streams.mod_spec(mod_rows),
            _resident((1, D_MODEL)),
            _resident(w_in.shape),
            _resident((1, HEAD_DIM)),
            _resident((1, HEAD_DIM)),
            pl.BlockSpec((tm, HEAD_DIM), rope_index),
            pl.BlockSpec((tm, HEAD_DIM), rope_index),
        ],
        out_specs=[pl.BlockSpec((1, N_Q_HEADS, tm, HEAD_DIM), q_index),
                   tok(KV_WIDTH), tok(VX_WIDTH), tok(FOURIER_WIDTH), tok(D_MODEL), tok(D_MODEL),
                   ctx_tok(KV_WIDTH), ctx_tok(VX_WIDTH)],
        compiler_params=_params("arbitrary"),
        name="inproj",
    )(x, ctx, mod, gain, w_in, q_norm, k_norm, cos2, sin2)


_NT = (((1,), (1,)), ((), ()))


ATTN_SCRATCH_PER_HEAD = 3
ATTN_SCORE_BUFFERS = 3


def _attn_kernel(q_ref, kc_ref, k_ref, vcx_ref, vx_ref, o_ref, *scratch):
    n_head_refs = ATTN_SCRATCH_PER_HEAD * GQA_GROUP
    heads = [scratch[ATTN_SCRATCH_PER_HEAD * g:ATTN_SCRATCH_PER_HEAD * (g + 1)] for g in range(GQA_GROUP)]
    scores = [scratch[n_head_refs + 2 * b:n_head_refs + 2 * b + 2] for b in range(ATTN_SCORE_BUFFERS)]
    score_refs = lambda g: scores[g % ATTN_SCORE_BUFFERS]

    @pl.when(pl.program_id(0) == 0)
    def _no_previous_item():
        for pc_ref, pl_ref, _ in heads:
            pc_ref[...] = jnp.ones(pc_ref.shape, BF16)
            pl_ref[...] = jnp.ones(pl_ref.shape, BF16)
        for ref in (*score_refs(GQA_GROUP - 1), heads[-1][2]):
            ref[...] = jnp.zeros(ref.shape, F32)

    def weights_from_scores(g):
        pc_ref, pl_ref, m_ref = heads[g]
        m = m_ref[...]
        for s_ref, p_ref in zip(score_refs(g), (pc_ref, pl_ref)):
            for lo in range(0, s_ref.shape[1], V7X_MXU_DIM):
                p_ref[:, lo:lo + V7X_MXU_DIM] = jnp.exp2(s_ref[:, lo:lo + V7X_MXU_DIM] - m).astype(BF16)

    def values_then_scores(g):
        pc_ref, pl_ref, m_ref = heads[g]
        sc_ref, sl_ref = score_refs(g)
        oe = _dot(pc_ref[...], vcx_ref[0]) + _dot(pl_ref[...], vx_ref[0])
        o_ref[0, :, g * HEAD_DIM:(g + 1) * HEAD_DIM] = (oe[:, :HEAD_DIM] / oe[:, HEAD_DIM:]).astype(BF16)

        q = q_ref[0, g]
        s_c = lax.dot_general(q, kc_ref[0], _NT, preferred_element_type=F32)
        s_l = lax.dot_general(q, k_ref[0], _NT, preferred_element_type=F32)
        m_ref[...] = jnp.maximum(jnp.max(s_c, axis=-1, keepdims=True), jnp.max(s_l, axis=-1, keepdims=True))
        sc_ref[...] = s_c
        sl_ref[...] = s_l

    last = GQA_GROUP - 1
    weights_from_scores(last)
    for g in range(GQA_GROUP):
        if g >= ATTN_SCORE_BUFFERS:
            weights_from_scores(g - ATTN_SCORE_BUFFERS)
        values_then_scores(g)
    for g in range(GQA_GROUP - ATTN_SCORE_BUFFERS, last):
        weights_from_scores(g)


def _attention(q, k, vx, k_c, vx_c):
    nb, _, n, _ = q.shape
    n_ctx = k_c.shape[1]
    tq = ATTN_Q_TILE
    assert n % tq == 0 and n % V7X_MXU_DIM == 0 and n_ctx % V7X_MXU_DIM == 0, (n, n_ctx)
    tiles = n // tq
    items = nb * N_KV_HEADS * tiles

    def item(t):
        return t // (N_KV_HEADS * tiles), (t // tiles) % N_KV_HEADS, t % tiles

    def scored(t):
        return item(jnp.minimum(t, items - 1))

    def weighted(t):
        return item(jnp.maximum(t - 1, 0))

    def q_map(t):
        b, h, i = scored(t)
        return b, h, i, 0

    def k_map(t):
        b, h, _ = scored(t)
        return b, 0, h

    def v_map(t):
        b, h, _ = weighted(t)
        return b, 0, h

    def o_map(t):
        b, h, i = weighted(t)
        return b, i, h

    return pl.pallas_call(
        _attn_kernel,
        out_shape=jax.ShapeDtypeStruct((nb, n, Q_WIDTH), BF16),
        grid=(items + 1,),
        in_specs=[
            pl.BlockSpec((1, GQA_GROUP, tq, HEAD_DIM), q_map),
            pl.BlockSpec((1, n_ctx, HEAD_DIM), k_map),
            pl.BlockSpec((1, n, HEAD_DIM), k_map),
            pl.BlockSpec((1, n_ctx, V7X_MXU_DIM), v_map),
            pl.BlockSpec((1, n, V7X_MXU_DIM), v_map),
        ],
        out_specs=pl.BlockSpec((1, tq, GQA_GROUP * HEAD_DIM), o_map),
        scratch_shapes=(
            GQA_GROUP * [pltpu.VMEM((tq, n_ctx), BF16), pltpu.VMEM((tq, n), BF16), pltpu.VMEM((tq, 1), F32)]
            + ATTN_SCORE_BUFFERS * [pltpu.VMEM((tq, n_ctx), F32), pltpu.VMEM((tq, n), F32)]),
        compiler_params=pltpu.CompilerParams(dimension_semantics=("arbitrary",),
                                             vmem_limit_bytes=V7X_VMEM_BYTES - 3 * 2**20),
        name="attn",
    )(q, k_c, k, vx_c, vx)


DFT_RADIX = 8


def _dft_tables(n):
    r = n // DFT_RADIX
    k = np.arange(DFT_RADIX)[:, None, None] + DFT_RADIX * np.arange(r)[None, :, None]
    ang = 2.0 * np.pi * ((k * np.arange(r)[None, None, :]) % n) / n
    cos, sin = np.cos(ang), np.sin(ang)
    pos = np.concatenate([np.concatenate([cos, sin], axis=2),
                          np.concatenate([-sin, cos], axis=2)], axis=1)
    c = FOURIER_GROUP
    scale = 1.0 / np.sqrt(float(n * c))
    ang_c = 2.0 * np.pi * (np.outer(np.arange(c), np.arange(c)) % c) / c
    eye = np.eye(V7X_MXU_DIM // c)
    chan = np.concatenate([np.kron(eye, np.cos(ang_c) * scale),
                           np.kron(eye, np.sin(ang_c) * scale)], axis=0)
    return jnp.asarray(pos, F32).astype(BF16), jnp.asarray(chan, F32).astype(BF16)


def _fold_radix8(x):
    c = 0.5 ** 0.5
    e02p, e02m, e13p, e13m = x[0] + x[4], x[0] - x[4], x[2] + x[6], x[2] - x[6]
    o02p, o02m, o13p, o13m = x[1] + x[5], x[1] - x[5], x[3] + x[7], x[3] - x[7]
    e0, e2, o0, o2 = e02p + e13p, e02p - e13p, o02p + o13p, o02p - o13p
    p, q = c * (o02m - o13m), c * (o02m + o13m)
    re = [e0 + o0, e02m + p, e2, e02m - p, e0 - o0, e02m - p, e2, e02m + p]
    im1, im3 = -e13m - q, e13m - q
    im = [None, im1, -o2, im3, None, -im3, o2, -im1]
    return re, im


def _fourier_kernel(f_ref, pos_ref, chan_ref, o_ref, t_ref, zr_ref, zi_ref, y_ref):
    r = f_ref.shape[1] // DFT_RADIX
    w = FOURIER_WIDTH
    pair = V7X_MXU_DIM
    slab = y_ref.shape[2]
    for lo in range(0, w, pair):
        for sub in range(lo, lo + pair, FOURIER_GROUP):
            lanes = slice(sub, sub + FOURIER_GROUP)
            x = [f_ref[0, n1 * r:(n1 + 1) * r, lanes].astype(F32) for n1 in range(DFT_RADIX)]
            re, im = _fold_radix8(x)
            for k1 in range(DFT_RADIX):
                t_ref[k1, 0:r, lanes] = re[k1].astype(BF16)
                t_ref[k1, r:2 * r, lanes] = (
                    jnp.zeros((r, FOURIER_GROUP), BF16) if im[k1] is None else im[k1].astype(BF16))
        h = lo // pair
        for k1 in range(DFT_RADIX):
            z = _dot(pos_ref[k1], t_ref[k1, :, lo:lo + pair])
            zr_ref[h, k1 * r:(k1 + 1) * r, :] = z[0:r].astype(BF16)
            zi_ref[h, k1 * r:(k1 + 1) * r, :] = z[r:2 * r].astype(BF16)
        rows_per_dot = (DFT_RADIX // 2) * r
        for top in range(0, DFT_RADIX * r, rows_per_dot):
            rows = slice(top, top + rows_per_dot)
            y = (_dot(zr_ref[h, rows, :], chan_ref[0:pair, :])
                 + _dot(zi_ref[h, rows, :], chan_ref[pair:2 * pair, :]))
            for k1 in range(top // r, (top + rows_per_dot) // r):
                for j in range(pair // slab):
                    y_ref[(lo + j * slab) // slab, pl.ds(k1, r, stride=DFT_RADIX), :] = (
                        y[k1 * r - top:(k1 + 1) * r - top, j * slab:(j + 1) * slab])
    for j in range(w // slab):
        o_ref[0, :, j * slab:(j + 1) * slab] = y_ref[j].astype(BF16)


def _fourier(f):
    nb, n, w = f.shape
    assert w == FOURIER_WIDTH and n % (DFT_RADIX * V7X_MXU_DIM) == 0, f.shape
    pos, chan = _dft_tables(n)
    r = n // DFT_RADIX
    blk = pl.BlockSpec((1, n, w), lambda b: (b, 0, 0))
    lanes = V7X_LANES
    return pl.pallas_call(
        _fourier_kernel,
        out_shape=jax.ShapeDtypeStruct(f.shape, BF16),
        grid=(nb,),
        in_specs=[blk, _resident(pos.shape), _resident(chan.shape)],
        out_specs=blk,
        scratch_shapes=[pltpu.VMEM((DFT_RADIX, 2 * r, w), BF16),
                        pltpu.VMEM((w // V7X_MXU_DIM, n, V7X_MXU_DIM), BF16),
                        pltpu.VMEM((w // V7X_MXU_DIM, n, V7X_MXU_DIM), BF16),
                        pltpu.VMEM((w // lanes, n, lanes), F32)],
        compiler_params=_params("parallel"),
        name="fourier",
    )(f, pos, chan)


def _merge_kernel(x_ref, mod_ref, ya_ref, yf_ref, ga_ref, gf_ref, wab_ref, wfb_ref, wo_ref, o_ref,
                  merged_ref):
    gate = mod_ref[0, 5:6, :]
    for lo in range(0, D_MODEL, V7X_MXU_DIM):
        cols = slice(lo, lo + V7X_MXU_DIM)
        merged_ref[:, cols] = (ga_ref[0, :, cols].astype(F32) * _dot(ya_ref[0], wab_ref[:, cols])
                               + gf_ref[0, :, cols].astype(F32) * _dot(yf_ref[0], wfb_ref[:, cols])
                               ).astype(BF16)
    mix = _dot(merged_ref[...], wo_ref[...])
    o_ref[0] = x_ref[0] + gate * mix


def _merge(x, mod, y_attn, y_four, sig_a, sig_f, w_ab, w_fb, w_o):
    nb, t, _ = x.shape
    tm = WIDE_TOKEN_TILE
    assert t % tm == 0, x.shape
    tok = lambda w: pl.BlockSpec((1, tm, w), lambda b, i: (b, i, 0))
    return pl.pallas_call(
        _merge_kernel,
        out_shape=jax.ShapeDtypeStruct(x.shape, F32),
        grid=(nb, t // tm),
        in_specs=[
            tok(D_MODEL),
            pl.BlockSpec((1, N_MOD, D_MODEL), lambda b, i: (b, 0, 0)),
            tok(Q_WIDTH), tok(FOURIER_WIDTH), tok(D_MODEL), tok(D_MODEL),
            _resident(w_ab.shape), _resident(w_fb.shape), _resident(w_o.shape),
        ],
        out_specs=tok(D_MODEL),
        scratch_shapes=[pltpu.VMEM((tm, D_MODEL), BF16)],
        compiler_params=_params("parallel", "parallel"),
        name="merge",
    )(x, mod, y_attn, y_four, sig_a, sig_f, w_ab, w_fb, w_o)


def _rope_tables(n_tokens):
    f32 = np.float32
    rows = n_tokens // GRID_W
    row_ids = np.repeat(np.arange(rows, dtype=f32), GRID_W)
    col_ids = np.tile(np.arange(GRID_W, dtype=f32), rows)
    inv_freq = f32(ROPE_THETA) ** (-np.arange(0, AXIS_ROPE_DIM, 2, dtype=f32) / f32(AXIS_ROPE_DIM))
    ang = np.concatenate([row_ids[:, None] * inv_freq, col_ids[:, None] * inv_freq], axis=-1)
    cos, sin = np.cos(ang), np.sin(ang)
    assert ang.dtype == f32 and cos.dtype == f32
    return (jnp.asarray(np.concatenate([cos, cos], axis=-1)),
            jnp.asarray(np.concatenate([-sin, sin], axis=-1)))


def kernel(x, c, ctx, c_ctx, w_ada, b_ada, norm_ffn1, w_ffn1_in, w_ffn1_out, norm_mix, w_in,
           q_norm, k_norm, w_attn_branch, w_fourier_branch, w_out, norm_ffn2, w_ffn2_in, w_ffn2_out):
    nb, n_lat, _ = x.shape
    n_ctx = ctx.shape[1]
    depth = w_ada.shape[0]
    assert depth == 1, "the context stream is only carried through the (single) last layer"
    assert x.shape[2] == D_MODEL and ctx.shape == (nb, n_ctx, D_MODEL) and n_lat % GRID_W == 0
    assert w_in.shape[1:] == (D_MODEL, GF_OFF + D_MODEL) and w_ffn1_in.shape[1:] == (D_MODEL, 2 * D_FF)
    assert (nb * n_ctx) % TOKEN_TILE == 0, "context tokens are processed as one flat stream of tiles"
    cos2, sin2 = _rope_tables(n_lat)
    ctx_row = nb

    i = 0
    mod = _adaln(c, c_ctx, w_ada[i], b_ada[i][None, :])
    own_row, shared_row = (lambda b: b), (lambda b: ctx_row)
    x, ctx_flat = _ffn([x, ctx.reshape(1, nb * n_ctx, D_MODEL)], mod, [own_row, shared_row], 0,
                       norm_ffn1[i][None, :], w_ffn1_in[i], w_ffn1_out[i])

    q, k, vx, f, sig_a, sig_f, k_c, vx_c = _inproj(
        x, ctx_flat, mod, [own_row, shared_row], norm_mix[i][None, :], w_in[i],
        q_norm[i][None, :], k_norm[i][None, :], cos2, sin2)
    y_attn = _attention(q, k, vx, k_c.reshape(nb, n_ctx, KV_WIDTH), vx_c.reshape(nb, n_ctx, VX_WIDTH))
    y_four = _fourier(f)
    x = _merge(x, mod, y_attn, y_four, sig_a, sig_f, w_attn_branch[i], w_fourier_branch[i], w_out[i])

    return _ffn([x], mod, [own_row], 6, norm_ffn2[i][None, :], w_ffn2_in[i], w_ffn2_out[i])[0]
```

```python
import functools

import numpy as np
import jax
import jax.numpy as jnp
from jax import lax
from jax.experimental import pallas as pl
from jax.experimental.pallas import tpu as pltpu

D_MODEL = 1024
GRID_W = 64
HEAD_DIM = 128
N_Q_HEADS = D_MODEL // HEAD_DIM
N_KV_HEADS = N_Q_HEADS // 4
GQA_GROUP = N_Q_HEADS // N_KV_HEADS
Q_WIDTH = N_Q_HEADS * HEAD_DIM
KV_WIDTH = N_KV_HEADS * HEAD_DIM
FOURIER_GROUP = 128
N_FOURIER_GROUPS = 4
FOURIER_WIDTH = N_FOURIER_GROUPS * FOURIER_GROUP
D_FF = 2816
AXIS_ROPE_DIM = HEAD_DIM // 2
ROPE_THETA = 10000.0
EPS = 1e-6
N_MOD = 9
ATTN_SCALE = HEAD_DIM ** -0.5
LOG2_E = 1.4426950408889634

K_OFF = Q_WIDTH
V_OFF = K_OFF + KV_WIDTH
F_OFF = V_OFF + KV_WIDTH
GA_OFF = F_OFF + FOURIER_WIDTH
GF_OFF = GA_OFF + D_MODEL

V7X_MXU_DIM = 256
V7X_LANES = 128
V7X_SUBLANES = 8
VX_WIDTH = N_KV_HEADS * V7X_MXU_DIM
V7X_VMEM_BYTES = 64 * 2**20
VMEM_LIMIT = V7X_VMEM_BYTES - 3 * 2**20

TOKEN_TILE = 512
WIDE_TOKEN_TILE = 1024
FF_CHUNK = V7X_MXU_DIM
ATTN_Q_TILE = 1024
ADALN_TILE = 256

F32 = jnp.float32
BF16 = jnp.bfloat16


def _params(*sem):
    return pltpu.CompilerParams(dimension_semantics=sem, vmem_limit_bytes=VMEM_LIMIT)


def _resident(shape):
    return pl.BlockSpec(shape, lambda *_: (0,) * len(shape), pipeline_mode=pl.Buffered(1))


def _rms(x, gain):
    return x * lax.rsqrt(jnp.mean(x * x, axis=-1, keepdims=True) + EPS) * gain


def _dot(a, b):
    return jnp.dot(a, b.astype(BF16), preferred_element_type=F32)


def _sigmoid(x):
    return 0.5 * jnp.tanh(0.5 * x) + 0.5


def _adaln_kernel(c_ref, cctx_ref, w_ref, b_ref, o_ref, acc_ref):
    c, c_ctx = c_ref[...], cctx_ref[...]
    tail_rows = acc_ref.shape[0] - c.shape[0]
    first_tail_row = lax.broadcasted_iota(jnp.int32, (tail_rows, c.shape[1]), 0) == 0
    silu = lambda v: v * _sigmoid(v)
    a = jnp.concatenate([silu(c), jnp.where(first_tail_row, silu(c_ctx), 0.0)], axis=0)
    part = _dot(a.astype(BF16), w_ref[...])

    @pl.when(pl.program_id(0) == 0)
    def _first():
        acc_ref[...] = part + b_ref[...]

    @pl.when(pl.program_id(0) > 0)
    def _rest():
        acc_ref[...] += part

    @pl.when(pl.program_id(0) == pl.num_programs(0) - 1)
    def _emit():
        for m in range(N_MOD):
            o_ref[:, m, :] = acc_ref[:, m * D_MODEL:(m + 1) * D_MODEL]


def _adaln(c, c_ctx, w, b):
    nb, n_out = c.shape[0], w.shape[1]
    assert nb % V7X_SUBLANES == 0 and D_MODEL % ADALN_TILE == 0 and n_out == N_MOD * D_MODEL
    rows = nb + V7X_SUBLANES
    return pl.pallas_call(
        _adaln_kernel,
        out_shape=jax.ShapeDtypeStruct((rows, N_MOD, D_MODEL), F32),
        grid=(D_MODEL // ADALN_TILE,),
        in_specs=[
            pl.BlockSpec((nb, ADALN_TILE), lambda k: (0, k)),
            pl.BlockSpec((1, ADALN_TILE), lambda k: (0, k)),
            pl.BlockSpec((ADALN_TILE, n_out), lambda k: (k, 0)),
            pl.BlockSpec((1, n_out), lambda k: (0, 0)),
        ],
        out_specs=pl.BlockSpec((rows, N_MOD, D_MODEL), lambda k: (0, 0, 0)),
        scratch_shapes=[pltpu.VMEM((rows, n_out), F32)],
        compiler_params=_params("arbitrary"),
        name="adaln",
    )(c, c_ctx[None, :], w, b)


class _TokenStreams:
    def __init__(self, arrays, tile):
        assert all(a.ndim == 3 and a.shape[1] % tile == 0 for a in arrays), [a.shape for a in arrays]
        self.tile = tile
        self.per_batch = [a.shape[1] // tile for a in arrays]
        counts = [a.shape[0] * p for a, p in zip(arrays, self.per_batch)]
        self.starts = [sum(counts[:s]) for s in range(len(arrays) + 1)]
        self.steps = self.starts[-1]

    def batch_and_tile(self, s, t):
        local = jnp.clip(t - self.starts[s], 0, self.starts[s + 1] - self.starts[s] - 1)
        return local // self.per_batch[s], local % self.per_batch[s]

    def token_spec(self, s, width):
        def index(t):
            b, i = self.batch_and_tile(s, t)
            return b, i, 0
        return pl.BlockSpec((1, self.tile, width), index)

    def mod_spec(self, mod_rows):
        def index(t):
            row = mod_rows[0](self.batch_and_tile(0, t)[0])
            for s in range(1, len(mod_rows)):
                row = jnp.where(t >= self.starts[s], mod_rows[s](self.batch_and_tile(s, t)[0]), row)
            return row, 0, 0
        return pl.BlockSpec((1, N_MOD, D_MODEL), index)

    def active(self, s):
        t = pl.program_id(0)
        return jnp.logical_and(t >= self.starts[s], t < self.starts[s + 1])


def _ffn_kernel(*refs, streams, mod_base):
    n = len(streams.per_batch)
    x_refs, (mod_ref, gain_ref, win_ref, wout_ref), o_refs, act_ref = (
        refs[:n], refs[n:n + 4], refs[n + 4:2 * n + 4], refs[2 * n + 4])

    def half_step(x_ref, o_ref):
        x = x_ref[0]
        shift = mod_ref[0, mod_base:mod_base + 1, :]
        scale = mod_ref[0, mod_base + 1:mod_base + 2, :]
        gate = mod_ref[0, mod_base + 2:mod_base + 3, :]
        h = (_rms(x, gain_ref[...]) * (1.0 + scale) + shift).astype(BF16)
        for j in range(D_FF // FF_CHUNK):
            lo = j * FF_CHUNK
            g = _dot(h, win_ref[:, lo:lo + FF_CHUNK])
            u = _dot(h, win_ref[:, D_FF + lo:D_FF + lo + FF_CHUNK])
            act_ref[:, lo:lo + FF_CHUNK] = (g * _sigmoid(g) * u).astype(BF16)
        y = _dot(act_ref[...], wout_ref[...])
        o_ref[0] = x + 0.5 * gate * y

    if n == 1:
        half_step(x_refs[0], o_refs[0])
    else:
        for s in range(n):
            pl.when(streams.active(s))(functools.partial(half_step, x_refs[s], o_refs[s]))


def _ffn(xs, mod, mod_rows, mod_base, gain, w_in, w_out):
    tile = TOKEN_TILE
    streams = _TokenStreams(xs, tile)
    tok = [streams.token_spec(s, D_MODEL) for s in range(len(xs))]
    return pl.pallas_call(
        functools.partial(_ffn_kernel, streams=streams, mod_base=mod_base),
        out_shape=[jax.ShapeDtypeStruct(x.shape, F32) for x in xs],
        grid=(streams.steps,),
        in_specs=tok + [
            streams.mod_spec(mod_rows),
            _resident((1, D_MODEL)),
            _resident((D_MODEL, 2 * D_FF)),
            _resident((D_FF, D_MODEL)),
        ],
        out_specs=tok,
        scratch_shapes=[pltpu.VMEM((tile, D_FF), BF16)],
        compiler_params=_params("arbitrary"),
        name="ffn",
    )(*xs, mod, gain, w_in, w_out)


def _head_norm(t, gain):
    return t * lax.rsqrt(jnp.mean(t * t, axis=-1, keepdims=True) + EPS) * gain


def _rope(t, cos2, sin2):
    return t * cos2 + pltpu.roll(t, AXIS_ROPE_DIM, 1) * sin2


def _store_widened_values(v_ref, v):
    ones = jnp.ones((v.shape[0], V7X_MXU_DIM - HEAD_DIM), BF16)
    for hd in range(N_KV_HEADS):
        lo = hd * V7X_MXU_DIM
        v_ref[0, :, lo:lo + HEAD_DIM] = v[:, hd * HEAD_DIM:(hd + 1) * HEAD_DIM].astype(BF16)
        v_ref[0, :, lo + HEAD_DIM:lo + V7X_MXU_DIM] = ones


def _inproj_kernel(x_ref, ctx_ref, mod_ref, gain_ref, w_ref, qn_ref, kn_ref, cos_ref, sin_ref,
                   q_ref, k_ref, v_ref, f_ref, ga_ref, gf_ref, kc_ref, vc_ref, *, streams):
    kn = kn_ref[...]
    heads_per_dot = V7X_MXU_DIM // HEAD_DIM

    def normed_input(tok_ref):
        shift = mod_ref[0, 3:4, :]
        scale = mod_ref[0, 4:5, :]
        return (_rms(tok_ref[0], gain_ref[...]) * (1.0 + scale) + shift).astype(BF16)

    @pl.when(streams.active(0))
    def _latent_tokens():
        h = normed_input(x_ref)

        def wide_dot(col):
            return _dot(h, w_ref[:, col:col + V7X_MXU_DIM])

        cos2 = cos_ref[...]
        sin2 = sin_ref[...]
        qn = qn_ref[...] * (ATTN_SCALE * LOG2_E)
        for c in range(Q_WIDTH // V7X_MXU_DIM):
            t = wide_dot(c * V7X_MXU_DIM)
            for j in range(heads_per_dot):
                tj = t[:, j * HEAD_DIM:(j + 1) * HEAD_DIM]
                q_ref[0, c * heads_per_dot + j] = _rope(_head_norm(tj, qn), cos2, sin2).astype(BF16)
        for c in range(KV_WIDTH // V7X_MXU_DIM):
            t = wide_dot(K_OFF + c * V7X_MXU_DIM)
            for j in range(heads_per_dot):
                lo = c * V7X_MXU_DIM + j * HEAD_DIM
                tj = t[:, j * HEAD_DIM:(j + 1) * HEAD_DIM]
                k_ref[0, :, lo:lo + HEAD_DIM] = _rope(_head_norm(tj, kn), cos2, sin2).astype(BF16)
        _store_widened_values(v_ref, _dot(h, w_ref[:, V_OFF:V_OFF + KV_WIDTH]))
        for c in range(FOURIER_WIDTH // V7X_MXU_DIM):
            lo = c * V7X_MXU_DIM
            f_ref[0, :, lo:lo + V7X_MXU_DIM] = wide_dot(F_OFF + lo).astype(BF16)
        for c in range(D_MODEL // V7X_MXU_DIM):
            lo = c * V7X_MXU_DIM
            ga_ref[0, :, lo:lo + V7X_MXU_DIM] = _sigmoid(wide_dot(GA_OFF + lo)).astype(BF16)
            gf_ref[0, :, lo:lo + V7X_MXU_DIM] = _sigmoid(wide_dot(GF_OFF + lo)).astype(BF16)

    @pl.when(streams.active(1))
    def _context_tokens():
        h = normed_input(ctx_ref)
        t = _dot(h, w_ref[:, K_OFF:K_OFF + KV_WIDTH])
        for hd in range(N_KV_HEADS):
            lo = hd * HEAD_DIM
            kc_ref[0, :, lo:lo + HEAD_DIM] = _head_norm(t[:, lo:lo + HEAD_DIM], kn).astype(BF16)
        _store_widened_values(vc_ref, _dot(h, w_ref[:, V_OFF:V_OFF + KV_WIDTH]))


def _inproj(x, ctx, mod, mod_rows, gain, w_in, q_norm, k_norm, cos2, sin2):
    nb, t, _ = x.shape
    streams = _TokenStreams([x, ctx], TOKEN_TILE)
    tm = streams.tile
    tok = lambda w: streams.token_spec(0, w)
    ctx_tok = lambda w: streams.token_spec(1, w)
    shp = lambda w: jax.ShapeDtypeStruct((nb, t, w), BF16)
    ctx_shp = lambda w: jax.ShapeDtypeStruct(ctx.shape[:2] + (w,), BF16)

    def q_index(step):
        b, i = streams.batch_and_tile(0, step)
        return b, 0, i, 0

    def rope_index(step):
        return streams.batch_and_tile(0, step)[1], 0

    return pl.pallas_call(
        functools.partial(_inproj_kernel, streams=streams),
        out_shape=[jax.ShapeDtypeStruct((nb, N_Q_HEADS, t, HEAD_DIM), BF16), shp(KV_WIDTH),
                   shp(VX_WIDTH), shp(FOURIER_WIDTH), shp(D_MODEL), shp(D_MODEL),
                   ctx_shp(KV_WIDTH), ctx_shp(VX_WIDTH)],
        grid=(streams.steps,),
        in_specs=[
            tok(D_MODEL),
            ctx_tok(D_MODEL),
            streams.mod_spec(mod_rows),
            _resident((1, D_MODEL)),
            _resident(w_in.shape),
            _resident((1, HEAD_DIM)),
            _resident((1, HEAD_DIM)),
            pl.BlockSpec((tm, HEAD_DIM), rope_index),
            pl.BlockSpec((tm, HEAD_DIM), rope_index),
        ],
        out_specs=[pl.BlockSpec((1, N_Q_HEADS, tm, HEAD_DIM), q_index),
                   tok(KV_WIDTH), tok(VX_WIDTH), tok(FOURIER_WIDTH), tok(D_MODEL), tok(D_MODEL),
                   ctx_tok(KV_WIDTH), ctx_tok(VX_WIDTH)],
        compiler_params=_params("arbitrary"),
        name="inproj",
    )(x, ctx, mod, gain, w_in, q_norm, k_norm, cos2, sin2)


_NT = (((1,), (1,)), ((), ()))


ATTN_SCRATCH_PER_HEAD = 3
ATTN_SCORE_BUFFERS = 3


def _attn_kernel(q_ref, kc_ref, k_ref, vcx_ref, vx_ref, o_ref, *scratch):
    n_head_refs = ATTN_SCRATCH_PER_HEAD * GQA_GROUP
    heads = [scratch[ATTN_SCRATCH_PER_HEAD * g:ATTN_SCRATCH_PER_HEAD * (g + 1)] for g in range(GQA_GROUP)]
    scores = [scratch[n_head_refs + 2 * b:n_head_refs + 2 * b + 2] for b in range(ATTN_SCORE_BUFFERS)]
    score_refs = lambda g: scores[g % ATTN_SCORE_BUFFERS]

    @pl.when(pl.program_id(0) == 0)
    def _no_previous_item():
        for pc_ref, pl_ref, _ in heads:
            pc_ref[...] = jnp.ones(pc_ref.shape, BF16)
            pl_ref[...] = jnp.ones(pl_ref.shape, BF16)
        for ref in (*score_refs(GQA_GROUP - 1), heads[-1][2]):
            ref[...] = jnp.zeros(ref.shape, F32)

    def weights_from_scores(g):
        pc_ref, pl_ref, m_ref = heads[g]
        m = m_ref[...]
        for s_ref, p_ref in zip(score_refs(g), (pc_ref, pl_ref)):
            for lo in range(0, s_ref.shape[1], V7X_MXU_DIM):
                p_ref[:, lo:lo + V7X_MXU_DIM] = jnp.exp2(s_ref[:, lo:lo + V7X_MXU_DIM] - m).astype(BF16)

    def values_then_scores(g):
        pc_ref, pl_ref, m_ref = heads[g]
        sc_ref, sl_ref = score_refs(g)
        oe = _dot(pc_ref[...], vcx_ref[0]) + _dot(pl_ref[...], vx_ref[0])
        o_ref[0, :, g * HEAD_DIM:(g + 1) * HEAD_DIM] = (oe[:, :HEAD_DIM] / oe[:, HEAD_DIM:]).astype(BF16)

        q = q_ref[0, g]
        s_c = lax.dot_general(q, kc_ref[0], _NT, preferred_element_type=F32)
        s_l = lax.dot_general(q, k_ref[0], _NT, preferred_element_type=F32)
        m_ref[...] = jnp.maximum(jnp.max(s_c, axis=-1, keepdims=True), jnp.max(s_l, axis=-1, keepdims=True))
        sc_ref[...] = s_c
        sl_ref[...] = s_l

    last = GQA_GROUP - 1
    weights_from_scores(last)
    for g in range(GQA_GROUP):
        if g >= ATTN_SCORE_BUFFERS:
            weights_from_scores(g - ATTN_SCORE_BUFFERS)
        values_then_scores(g)
    for g in range(GQA_GROUP - ATTN_SCORE_BUFFERS, last):
        weights_from_scores(g)


def _attention(q, k, vx, k_c, vx_c):
    nb, _, n, _ = q.shape
    n_ctx = k_c.shape[1]
    tq = ATTN_Q_TILE
    assert n % tq == 0 and n % V7X_MXU_DIM == 0 and n_ctx % V7X_MXU_DIM == 0, (n, n_ctx)
    tiles = n // tq
    items = nb * N_KV_HEADS * tiles

    def item(t):
        return t // (N_KV_HEADS * tiles), (t // tiles) % N_KV_HEADS, t % tiles

    def scored(t):
        return item(jnp.minimum(t, items - 1))

    def weighted(t):
        return item(jnp.maximum(t - 1, 0))

    def q_map(t):
        b, h, i = scored(t)
        return b, h, i, 0

    def k_map(t):
        b, h, _ = scored(t)
        return b, 0, h

    def v_map(t):
        b, h, _ = weighted(t)
        return b, 0, h

    def o_map(t):
        b, h, i = weighted(t)
        return b, i, h

    return pl.pallas_call(
        _attn_kernel,
        out_shape=jax.ShapeDtypeStruct((nb, n, Q_WIDTH), BF16),
        grid=(items + 1,),
        in_specs=[
            pl.BlockSpec((1, GQA_GROUP, tq, HEAD_DIM), q_map),
            pl.BlockSpec((1, n_ctx, HEAD_DIM), k_map),
            pl.BlockSpec((1, n, HEAD_DIM), k_map),
            pl.BlockSpec((1, n_ctx, V7X_MXU_DIM), v_map),
            pl.BlockSpec((1, n, V7X_MXU_DIM), v_map),
        ],
        out_specs=pl.BlockSpec((1, tq, GQA_GROUP * HEAD_DIM), o_map),
        scratch_shapes=(
            GQA_GROUP * [pltpu.VMEM((tq, n_ctx), BF16), pltpu.VMEM((tq, n), BF16), pltpu.VMEM((tq, 1), F32)]
            + ATTN_SCORE_BUFFERS * [pltpu.VMEM((tq, n_ctx), F32), pltpu.VMEM((tq, n), F32)]),
        compiler_params=_params("arbitrary"),
        name="attn",
    )(q, k_c, k, vx_c, vx)


DFT_RADIX = 8


def _dft_tables(n):
    r = n // DFT_RADIX
    k = np.arange(DFT_RADIX)[:, None, None] + DFT_RADIX * np.arange(r)[None, :, None]
    ang = 2.0 * np.pi * ((k * np.arange(r)[None, None, :]) % n) / n
    cos, sin = np.cos(ang), np.sin(ang)
    pos = np.concatenate([np.concatenate([cos, sin], axis=2),
                          np.concatenate([-sin, cos], axis=2)], axis=1)
    c = FOURIER_GROUP
    scale = 1.0 / np.sqrt(float(n * c))
    ang_c = 2.0 * np.pi * (np.outer(np.arange(c), np.arange(c)) % c) / c
    eye = np.eye(V7X_MXU_DIM // c)
    chan = np.concatenate([np.kron(eye, np.cos(ang_c) * scale),
                           np.kron(eye, np.sin(ang_c) * scale)], axis=0)
    return jnp.asarray(pos, F32).astype(BF16), jnp.asarray(chan, F32).astype(BF16)


def _fold_radix8(x):
    c = 0.5 ** 0.5
    e02p, e02m, e13p, e13m = x[0] + x[4], x[0] - x[4], x[2] + x[6], x[2] - x[6]
    o02p, o02m, o13p, o13m = x[1] + x[5], x[1] - x[5], x[3] + x[7], x[3] - x[7]
    e0, e2, o0, o2 = e02p + e13p, e02p - e13p, o02p + o13p, o02p - o13p
    p, q = c * (o02m - o13m), c * (o02m + o13m)
    re = [e0 + o0, e02m + p, e2, e02m - p, e0 - o0, e02m - p, e2, e02m + p]
    im1, im3 = -e13m - q, e13m - q
    im = [None, im1, -o2, im3, None, -im3, o2, -im1]
    return re, im


def _fourier_kernel(f_ref, pos_ref, chan_ref, o_ref, t_ref, zr_ref, zi_ref, y_ref):
    r = f_ref.shape[1] // DFT_RADIX
    w = FOURIER_WIDTH
    pair = V7X_MXU_DIM
    slab = y_ref.shape[2]
    for lo in range(0, w, pair):
        for sub in range(lo, lo + pair, FOURIER_GROUP):
            lanes = slice(sub, sub + FOURIER_GROUP)
            x = [f_ref[0, n1 * r:(n1 + 1) * r, lanes].astype(F32) for n1 in range(DFT_RADIX)]
            re, im = _fold_radix8(x)
            for k1 in range(DFT_RADIX):
                t_ref[k1, 0:r, lanes] = re[k1].astype(BF16)
                t_ref[k1, r:2 * r, lanes] = (
                    jnp.zeros((r, FOURIER_GROUP), BF16) if im[k1] is None else im[k1].astype(BF16))
        h = lo // pair
        for k1 in range(DFT_RADIX):
            z = _dot(pos_ref[k1], t_ref[k1, :, lo:lo + pair])
            zr_ref[h, k1 * r:(k1 + 1) * r, :] = z[0:r].astype(BF16)
            zi_ref[h, k1 * r:(k1 + 1) * r, :] = z[r:2 * r].astype(BF16)
        rows_per_dot = (DFT_RADIX // 2) * r
        for top in range(0, DFT_RADIX * r, rows_per_dot):
            rows = slice(top, top + rows_per_dot)
            y = (_dot(zr_ref[h, rows, :], chan_ref[0:pair, :])
                 + _dot(zi_ref[h, rows, :], chan_ref[pair:2 * pair, :]))
            for k1 in range(top // r, (top + rows_per_dot) // r):
                for j in range(pair // slab):
                    y_ref[(lo + j * slab) // slab, pl.ds(k1, r, stride=DFT_RADIX), :] = (
                        y[k1 * r - top:(k1 + 1) * r - top, j * slab:(j + 1) * slab])
    for j in range(w // slab):
        o_ref[0, :, j * slab:(j + 1) * slab] = y_ref[j].astype(BF16)


def _fourier(f):
    nb, n, w = f.shape
    assert w == FOURIER_WIDTH and n % (DFT_RADIX * V7X_MXU_DIM) == 0, f.shape
    pos, chan = _dft_tables(n)
    r = n // DFT_RADIX
    blk = pl.BlockSpec((1, n, w), lambda b: (b, 0, 0))
    lanes = V7X_LANES
    return pl.pallas_call(
        _fourier_kernel,
        out_shape=jax.ShapeDtypeStruct(f.shape, BF16),
        grid=(nb,),
        in_specs=[blk, _resident(pos.shape), _resident(chan.shape)],
        out_specs=blk,
        scratch_shapes=[pltpu.VMEM((DFT_RADIX, 2 * r, w), BF16),
                        pltpu.VMEM((w // V7X_MXU_DIM, n, V7X_MXU_DIM), BF16),
                        pltpu.VMEM((w // V7X_MXU_DIM, n, V7X_MXU_DIM), BF16),
                        pltpu.VMEM((w // lanes, n, lanes), F32)],
        compiler_params=_params("parallel"),
        name="fourier",
    )(f, pos, chan)


def _merge_kernel(x_ref, mod_ref, ya_ref, yf_ref, ga_ref, gf_ref, wab_ref, wfb_ref, wo_ref, o_ref,
                  merged_ref):
    gate = mod_ref[0, 5:6, :]
    for lo in range(0, D_MODEL, V7X_MXU_DIM):
        cols = slice(lo, lo + V7X_MXU_DIM)
        merged_ref[:, cols] = (ga_ref[0, :, cols].astype(F32) * _dot(ya_ref[0], wab_ref[:, cols])
                               + gf_ref[0, :, cols].astype(F32) * _dot(yf_ref[0], wfb_ref[:, cols])
                               ).astype(BF16)
    mix = _dot(merged_ref[...], wo_ref[...])
    o_ref[0] = x_ref[0] + gate * mix


def _merge(x, mod, y_attn, y_four, sig_a, sig_f, w_ab, w_fb, w_o):
    nb, t, _ = x.shape
    tm = WIDE_TOKEN_TILE
    assert t % tm == 0, x.shape
    tok = lambda w: pl.BlockSpec((1, tm, w), lambda b, i: (b, i, 0))
    return pl.pallas_call(
        _merge_kernel,
        out_shape=jax.ShapeDtypeStruct(x.shape, F32),
        grid=(nb, t // tm),
        in_specs=[
            tok(D_MODEL),
            pl.BlockSpec((1, N_MOD, D_MODEL), lambda b, i: (b, 0, 0)),
            tok(Q_WIDTH), tok(FOURIER_WIDTH), tok(D_MODEL), tok(D_MODEL),
            _resident(w_ab.shape), _resident(w_fb.shape), _resident(w_o.shape),
        ],
        out_specs=tok(D_MODEL),
        scratch_shapes=[pltpu.VMEM((tm, D_MODEL), BF16)],
        compiler_params=_params("parallel", "parallel"),
        name="merge",
    )(x, mod, y_attn, y_four, sig_a, sig_f, w_ab, w_fb, w_o)


def _rope_tables(n_tokens):
    f32 = np.float32
    rows = n_tokens // GRID_W
    row_ids = np.repeat(np.arange(rows, dtype=f32), GRID_W)
    col_ids = np.tile(np.arange(GRID_W, dtype=f32), rows)
    inv_freq = f32(ROPE_THETA) ** (-np.arange(0, AXIS_ROPE_DIM, 2, dtype=f32) / f32(AXIS_ROPE_DIM))
    ang = np.concatenate([row_ids[:, None] * inv_freq, col_ids[:, None] * inv_freq], axis=-1)
    cos, sin = np.cos(ang), np.sin(ang)
    assert ang.dtype == f32 and cos.dtype == f32
    return (jnp.asarray(np.concatenate([cos, cos], axis=-1)),
            jnp.asarray(np.concatenate([-sin, sin], axis=-1)))


def kernel(x, c, ctx, c_ctx, w_ada, b_ada, norm_ffn1, w_ffn1_in, w_ffn1_out, norm_mix, w_in,
           q_norm, k_norm, w_attn_branch, w_fourier_branch, w_out, norm_ffn2, w_ffn2_in, w_ffn2_out):
    nb, n_lat, _ = x.shape
    n_ctx = ctx.shape[1]
    depth = w_ada.shape[0]
    assert depth == 1, "the context stream is only carried through the (single) last layer"
    assert x.shape[2] == D_MODEL and ctx.shape == (nb, n_ctx, D_MODEL) and n_lat % GRID_W == 0
    assert w_in.shape[1:] == (D_MODEL, GF_OFF + D_MODEL) and w_ffn1_in.shape[1:] == (D_MODEL, 2 * D_FF)
    assert (nb * n_ctx) % TOKEN_TILE == 0, "context tokens are processed as one flat stream of tiles"
    cos2, sin2 = _rope_tables(n_lat)
    ctx_row = nb

    i = 0
    mod = _adaln(c, c_ctx, w_ada[i], b_ada[i][None, :])
    own_row, shared_row = (lambda b: b), (lambda b: ctx_row)
    x, ctx_flat = _ffn([x, ctx.reshape(1, nb * n_ctx, D_MODEL)], mod, [own_row, shared_row], 0,
                       norm_ffn1[i][None, :], w_ffn1_in[i], w_ffn1_out[i])

    q, k, vx, f, sig_a, sig_f, k_c, vx_c = _inproj(
        x, ctx_flat, mod, [own_row, shared_row], norm_mix[i][None, :], w_in[i],
        q_norm[i][None, :], k_norm[i][None, :], cos2, sin2)
    y_attn = _attention(q, k, vx, k_c.reshape(nb, n_ctx, KV_WIDTH), vx_c.reshape(nb, n_ctx, VX_WIDTH))
    y_four = _fourier(f)
    x = _merge(x, mod, y_attn, y_four, sig_a, sig_f, w_attn_branch[i], w_fourier_branch[i], w_out[i])

    return _ffn([x], mod, [own_row], 6, norm_ffn2[i][None, :], w_ffn2_in[i], w_ffn2_out[i])[0]
```

```python
import functools

import numpy as np
import jax
import jax.numpy as jnp
from jax import lax
from jax.experimental import pallas as pl
from jax.experimental.pallas import tpu as pltpu

D_MODEL = 1024
GRID_W = 64
HEAD_DIM = 128
N_Q_HEADS = D_MODEL // HEAD_DIM
N_KV_HEADS = N_Q_HEADS // 4
GQA_GROUP = N_Q_HEADS // N_KV_HEADS
Q_WIDTH = N_Q_HEADS * HEAD_DIM
KV_WIDTH = N_KV_HEADS * HEAD_DIM
FOURIER_GROUP = 128
N_FOURIER_GROUPS = 4
FOURIER_WIDTH = N_FOURIER_GROUPS * FOURIER_GROUP
D_FF = 2816
AXIS_ROPE_DIM = HEAD_DIM // 2
ROPE_THETA = 10000.0
EPS = 1e-6
N_MOD = 9
ATTN_SCALE = HEAD_DIM ** -0.5
LOG2_E = 1.4426950408889634

K_OFF = Q_WIDTH
V_OFF = K_OFF + KV_WIDTH
F_OFF = V_OFF + KV_WIDTH
GA_OFF = F_OFF + FOURIER_WIDTH
GF_OFF = GA_OFF + D_MODEL

V7X_MXU_DIM = 256
V7X_LANES = 128
V7X_SUBLANES = 8
VX_WIDTH = N_KV_HEADS * V7X_MXU_DIM
V7X_VMEM_BYTES = 64 * 2**20
VMEM_LIMIT = V7X_VMEM_BYTES - 3 * 2**20

TOKEN_TILE = 512
WIDE_TOKEN_TILE = 1024
FF_CHUNK = V7X_MXU_DIM
ATTN_Q_TILE = 1024
ADALN_TILE = 256

F32 = jnp.float32
BF16 = jnp.bfloat16


def _params(*sem):
    return pltpu.CompilerParams(dimension_semantics=sem, vmem_limit_bytes=VMEM_LIMIT)


def _resident(shape):
    return pl.BlockSpec(shape, lambda *_: (0,) * len(shape), pipeline_mode=pl.Buffered(1))


def _rms(x, gain):
    return x * lax.rsqrt(jnp.mean(x * x, axis=-1, keepdims=True) + EPS) * gain


def _dot(a, b):
    return jnp.dot(a, b.astype(BF16), preferred_element_type=F32)


def _sigmoid(x):
    return 0.5 * jnp.tanh(0.5 * x) + 0.5


def _adaln_kernel(c_ref, cctx_ref, w_ref, b_ref, o_ref, acc_ref):
    c, c_ctx = c_ref[...], cctx_ref[...]
    tail_rows = acc_ref.shape[0] - c.shape[0]
    first_tail_row = lax.broadcasted_iota(jnp.int32, (tail_rows, c.shape[1]), 0) == 0
    silu = lambda v: v * _sigmoid(v)
    a = jnp.concatenate([silu(c), jnp.where(first_tail_row, silu(c_ctx), 0.0)], axis=0)
    part = _dot(a.astype(BF16), w_ref[...])

    @pl.when(pl.program_id(0) == 0)
    def _first():
        acc_ref[...] = part + b_ref[...]

    @pl.when(pl.program_id(0) > 0)
    def _rest():
        acc_ref[...] += part

    @pl.when(pl.program_id(0) == pl.num_programs(0) - 1)
    def _emit():
        for m in range(N_MOD):
            o_ref[:, m, :] = acc_ref[:, m * D_MODEL:(m + 1) * D_MODEL]


def _adaln(c, c_ctx, w, b):
    nb, n_out = c.shape[0], w.shape[1]
    assert nb % V7X_SUBLANES == 0 and D_MODEL % ADALN_TILE == 0 and n_out == N_MOD * D_MODEL
    rows = nb + V7X_SUBLANES
    return pl.pallas_call(
        _adaln_kernel,
        out_shape=jax.ShapeDtypeStruct((rows, N_MOD, D_MODEL), F32),
        grid=(D_MODEL // ADALN_TILE,),
        in_specs=[
            pl.BlockSpec((nb, ADALN_TILE), lambda k: (0, k)),
            pl.BlockSpec((1, ADALN_TILE), lambda k: (0, k)),
            pl.BlockSpec((ADALN_TILE, n_out), lambda k: (k, 0)),
            pl.BlockSpec((1, n_out), lambda k: (0, 0)),
        ],
        out_specs=pl.BlockSpec((rows, N_MOD, D_MODEL), lambda k: (0, 0, 0)),
        scratch_shapes=[pltpu.VMEM((rows, n_out), F32)],
        compiler_params=_params("arbitrary"),
        name="adaln",
    )(c, c_ctx[None, :], w, b)


class _TokenStreams:
    def __init__(self, arrays, tile):
        assert all(a.ndim == 3 and a.shape[1] % tile == 0 for a in arrays), [a.shape for a in arrays]
        self.tile = tile
        self.per_batch = [a.shape[1] // tile for a in arrays]
        counts = [a.shape[0] * p for a, p in zip(arrays, self.per_batch)]
        self.starts = [sum(counts[:s]) for s in range(len(arrays) + 1)]
        self.steps = self.starts[-1]

    def batch_and_tile(self, s, t):
        local = jnp.clip(t - self.starts[s], 0, self.starts[s + 1] - self.starts[s] - 1)
        return local // self.per_batch[s], local % self.per_batch[s]

    def token_spec(self, s, width):
        def index(t):
            b, i = self.batch_and_tile(s, t)
            return b, i, 0
        return pl.BlockSpec((1, self.tile, width), index)

    def mod_spec(self, mod_rows):
        def index(t):
            row = mod_rows[0](self.batch_and_tile(0, t)[0])
            for s in range(1, len(mod_rows)):
                row = jnp.where(t >= self.starts[s], mod_rows[s](self.batch_and_tile(s, t)[0]), row)
            return row, 0, 0
        return pl.BlockSpec((1, N_MOD, D_MODEL), index)

    def active(self, s):
        t = pl.program_id(0)
        return jnp.logical_and(t >= self.starts[s], t < self.starts[s + 1])


def _ffn_kernel(*refs, streams, mod_base):
    n = len(streams.per_batch)
    x_refs, (mod_ref, gain_ref, win_ref, wout_ref), o_refs, act_ref = (
        refs[:n], refs[n:n + 4], refs[n + 4:2 * n + 4], refs[2 * n + 4])

    def half_step(x_ref, o_ref):
        x = x_ref[0]
        shift = mod_ref[0, mod_base:mod_base + 1, :]
        scale = mod_ref[0, mod_base + 1:mod_base + 2, :]
        gate = mod_ref[0, mod_base + 2:mod_base + 3, :]
        h = (_rms(x, gain_ref[...]) * (1.0 + scale) + shift).astype(BF16)
        for j in range(D_FF // FF_CHUNK):
            lo = j * FF_CHUNK
            g = _dot(h, win_ref[:, lo:lo + FF_CHUNK])
            u = _dot(h, win_ref[:, D_FF + lo:D_FF + lo + FF_CHUNK])
            act_ref[:, lo:lo + FF_CHUNK] = (g * _sigmoid(g) * u).astype(BF16)
        y = _dot(act_ref[...], wout_ref[...])
        o_ref[0] = x + 0.5 * gate * y

    if n == 1:
        half_step(x_refs[0], o_refs[0])
    else:
        for s in range(n):
            pl.when(streams.active(s))(functools.partial(half_step, x_refs[s], o_refs[s]))


def _ffn(xs, mod, mod_rows, mod_base, gain, w_in, w_out):
    tile = WIDE_TOKEN_TILE if len(xs) == 1 else TOKEN_TILE
    streams = _TokenStreams(xs, tile)
    tok = [streams.token_spec(s, D_MODEL) for s in range(len(xs))]
    return pl.pallas_call(
        functools.partial(_ffn_kernel, streams=streams, mod_base=mod_base),
        out_shape=[jax.ShapeDtypeStruct(x.shape, F32) for x in xs],
        grid=(streams.steps,),
        in_specs=tok + [
            streams.mod_spec(mod_rows),
            _resident((1, D_MODEL)),
            _resident((D_MODEL, 2 * D_FF)),
            _resident((D_FF, D_MODEL)),
        ],
        out_specs=tok,
        scratch_shapes=[pltpu.VMEM((tile, D_FF), BF16)],
        compiler_params=_params("arbitrary"),
        name="ffn",
    )(*xs, mod, gain, w_in, w_out)


def _head_norm(t, gain):
    return t * lax.rsqrt(jnp.mean(t * t, axis=-1, keepdims=True) + EPS) * gain


def _rope(t, cos2, sin2):
    return t * cos2 + pltpu.roll(t, AXIS_ROPE_DIM, 1) * sin2


def _store_widened_values(v_ref, v):
    ones = jnp.ones((v.shape[0], V7X_MXU_DIM - HEAD_DIM), BF16)
    for hd in range(N_KV_HEADS):
        lo = hd * V7X_MXU_DIM
        v_ref[0, :, lo:lo + HEAD_DIM] = v[:, hd * HEAD_DIM:(hd + 1) * HEAD_DIM].astype(BF16)
        v_ref[0, :, lo + HEAD_DIM:lo + V7X_MXU_DIM] = ones


def _inproj_kernel(x_ref, ctx_ref, mod_ref, gain_ref, w_ref, qn_ref, kn_ref, cos_ref, sin_ref,
                   q_ref, k_ref, v_ref, f_ref, ga_ref, gf_ref, kc_ref, vc_ref, *, streams):
    kn = kn_ref[...]
    heads_per_dot = V7X_MXU_DIM // HEAD_DIM

    def normed_input(tok_ref):
        shift = mod_ref[0, 3:4, :]
        scale = mod_ref[0, 4:5, :]
        return (_rms(tok_ref[0], gain_ref[...]) * (1.0 + scale) + shift).astype(BF16)

    @pl.when(streams.active(0))
    def _latent_tokens():
        h = normed_input(x_ref)

        def wide_dot(col):
            return _dot(h, w_ref[:, col:col + V7X_MXU_DIM])

        cos2 = cos_ref[...]
        sin2 = sin_ref[...]
        qn = qn_ref[...] * (ATTN_SCALE * LOG2_E)
        for c in range(Q_WIDTH // V7X_MXU_DIM):
            t = wide_dot(c * V7X_MXU_DIM)
            for j in range(heads_per_dot):
                tj = t[:, j * HEAD_DIM:(j + 1) * HEAD_DIM]
                q_ref[0, c * heads_per_dot + j] = _rope(_head_norm(tj, qn), cos2, sin2).astype(BF16)
        for c in range(KV_WIDTH // V7X_MXU_DIM):
            t = wide_dot(K_OFF + c * V7X_MXU_DIM)
            for j in range(heads_per_dot):
                lo = c * V7X_MXU_DIM + j * HEAD_DIM
                tj = t[:, j * HEAD_DIM:(j + 1) * HEAD_DIM]
                k_ref[0, :, lo:lo + HEAD_DIM] = _rope(_head_norm(tj, kn), cos2, sin2).astype(BF16)
        _store_widened_values(v_ref, _dot(h, w_ref[:, V_OFF:V_OFF + KV_WIDTH]))
        for c in range(FOURIER_WIDTH // V7X_MXU_DIM):
            lo = c * V7X_MXU_DIM
            f_ref[0, :, lo:lo + V7X_MXU_DIM] = wide_dot(F_OFF + lo).astype(BF16)
        for c in range(D_MODEL // V7X_MXU_DIM):
            lo = c * V7X_MXU_DIM
            ga_ref[0, :, lo:lo + V7X_MXU_DIM] = _sigmoid(wide_dot(GA_OFF + lo)).astype(BF16)
            gf_ref[0, :, lo:lo + V7X_MXU_DIM] = _sigmoid(wide_dot(GF_OFF + lo)).astype(BF16)

    @pl.when(streams.active(1))
    def _context_tokens():
        h = normed_input(ctx_ref)
        t = _dot(h, w_ref[:, K_OFF:K_OFF + KV_WIDTH])
        for hd in range(N_KV_HEADS):
            lo = hd * HEAD_DIM
            kc_ref[0, :, lo:lo + HEAD_DIM] = _head_norm(t[:, lo:lo + HEAD_DIM], kn).astype(BF16)
        _store_widened_values(vc_ref, _dot(h, w_ref[:, V_OFF:V_OFF + KV_WIDTH]))


def _inproj(x, ctx, mod, mod_rows, gain, w_in, q_norm, k_norm, cos2, sin2):
    nb, t, _ = x.shape
    streams = _TokenStreams([x, ctx], TOKEN_TILE)
    tm = streams.tile
    tok = lambda w: streams.token_spec(0, w)
    ctx_tok = lambda w: streams.token_spec(1, w)
    shp = lambda w: jax.ShapeDtypeStruct((nb, t, w), BF16)
    ctx_shp = lambda w: jax.ShapeDtypeStruct(ctx.shape[:2] + (w,), BF16)

    def q_index(step):
        b, i = streams.batch_and_tile(0, step)
        return b, 0, i, 0

    def rope_index(step):
        return streams.batch_and_tile(0, step)[1], 0

    return pl.pallas_call(
        functools.partial(_inproj_kernel, streams=streams),
        out_shape=[jax.ShapeDtypeStruct((nb, N_Q_HEADS, t, HEAD_DIM), BF16), shp(KV_WIDTH),
                   shp(VX_WIDTH), shp(FOURIER_WIDTH), shp(D_MODEL), shp(D_MODEL),
                   ctx_shp(KV_WIDTH), ctx_shp(VX_WIDTH)],
        grid=(streams.steps,),
        in_specs=[
            tok(D_MODEL),
            ctx_tok(D_MODEL),
            streams.mod_spec(mod_rows),
            _resident((1, D_MODEL)),
            _resident(w_in.shape),
            _resident((1, HEAD_DIM)),
            _resident((1, HEAD_DIM)),
            pl.BlockSpec((tm, HEAD_DIM), rope_index),
            pl.BlockSpec((tm, HEAD_DIM), rope_index),
        ],
        out_specs=[pl.BlockSpec((1, N_Q_HEADS, tm, HEAD_DIM), q_index),
                   tok(KV_WIDTH), tok(VX_WIDTH), tok(FOURIER_WIDTH), tok(D_MODEL), tok(D_MODEL),
                   ctx_tok(KV_WIDTH), ctx_tok(VX_WIDTH)],
        compiler_params=_params("arbitrary"),
        name="inproj",
    )(x, ctx, mod, gain, w_in, q_norm, k_norm, cos2, sin2)


_NT = (((1,), (1,)), ((), ()))


ATTN_SCRATCH_PER_HEAD = 3
ATTN_SCORE_BUFFERS = 3


def _attn_kernel(q_ref, kc_ref, k_ref, vcx_ref, vx_ref, o_ref, *scratch):
    n_head_refs = ATTN_SCRATCH_PER_HEAD * GQA_GROUP
    heads = [scratch[ATTN_SCRATCH_PER_HEAD * g:ATTN_SCRATCH_PER_HEAD * (g + 1)] for g in range(GQA_GROUP)]
    scores = [scratch[n_head_refs + 2 * b:n_head_refs + 2 * b + 2] for b in range(ATTN_SCORE_BUFFERS)]
    score_refs = lambda g: scores[g % ATTN_SCORE_BUFFERS]

    @pl.when(pl.program_id(0) == 0)
    def _no_previous_item():
        for pc_ref, pl_ref, _ in heads:
            pc_ref[...] = jnp.ones(pc_ref.shape, BF16)
            pl_ref[...] = jnp.ones(pl_ref.shape, BF16)
        for ref in (*score_refs(GQA_GROUP - 1), heads[-1][2]):
            ref[...] = jnp.zeros(ref.shape, F32)

    def weights_from_scores(g):
        pc_ref, pl_ref, m_ref = heads[g]
        m = m_ref[...]
        for s_ref, p_ref in zip(score_refs(g), (pc_ref, pl_ref)):
            for lo in range(0, s_ref.shape[1], V7X_MXU_DIM):
                p_ref[:, lo:lo + V7X_MXU_DIM] = jnp.exp2(s_ref[:, lo:lo + V7X_MXU_DIM] - m).astype(BF16)

    def values_then_scores(g):
        pc_ref, pl_ref, m_ref = heads[g]
        sc_ref, sl_ref = score_refs(g)
        oe = _dot(pc_ref[...], vcx_ref[0]) + _dot(pl_ref[...], vx_ref[0])
        o_ref[0, :, g * HEAD_DIM:(g + 1) * HEAD_DIM] = (oe[:, :HEAD_DIM] / oe[:, HEAD_DIM:]).astype(BF16)

        q = q_ref[0, g]
        s_c = lax.dot_general(q, kc_ref[0], _NT, preferred_element_type=F32)
        s_l = lax.dot_general(q, k_ref[0], _NT, preferred_element_type=F32)
        m_ref[...] = jnp.maximum(jnp.max(s_c, axis=-1, keepdims=True), jnp.max(s_l, axis=-1, keepdims=True))
        sc_ref[...] = s_c
        sl_ref[...] = s_l

    last = GQA_GROUP - 1
    weights_from_scores(last)
    for g in range(GQA_GROUP):
        if g >= ATTN_SCORE_BUFFERS:
            weights_from_scores(g - ATTN_SCORE_BUFFERS)
        values_then_scores(g)
    for g in range(GQA_GROUP - ATTN_SCORE_BUFFERS, last):
        weights_from_scores(g)


def _attention(q, k, vx, k_c, vx_c):
    nb, _, n, _ = q.shape
    n_ctx = k_c.shape[1]
    tq = ATTN_Q_TILE
    assert n % tq == 0 and n % V7X_MXU_DIM == 0 and n_ctx % V7X_MXU_DIM == 0, (n, n_ctx)
    tiles = n // tq
    items = nb * N_KV_HEADS * tiles

    def item(t):
        return t // (N_KV_HEADS * tiles), (t // tiles) % N_KV_HEADS, t % tiles

    def scored(t):
        return item(jnp.minimum(t, items - 1))

    def weighted(t):
        return item(jnp.maximum(t - 1, 0))

    def q_map(t):
        b, h, i = scored(t)
        return b, h, i, 0

    def k_map(t):
        b, h, _ = scored(t)
        return b, 0, h

    def v_map(t):
        b, h, _ = weighted(t)
        return b, 0, h

    def o_map(t):
        b, h, i = weighted(t)
        return b, i, h

    return pl.pallas_call(
        _attn_kernel,
        out_shape=jax.ShapeDtypeStruct((nb, n, Q_WIDTH), BF16),
        grid=(items + 1,),
        in_specs=[
            pl.BlockSpec((1, GQA_GROUP, tq, HEAD_DIM), q_map),
            pl.BlockSpec((1, n_ctx, HEAD_DIM), k_map),
            pl.BlockSpec((1, n, HEAD_DIM), k_map),
            pl.BlockSpec((1, n_ctx, V7X_MXU_DIM), v_map),
            pl.BlockSpec((1, n, V7X_MXU_DIM), v_map),
        ],
        out_specs=pl.BlockSpec((1, tq, GQA_GROUP * HEAD_DIM), o_map),
        scratch_shapes=(
            GQA_GROUP * [pltpu.VMEM((tq, n_ctx), BF16), pltpu.VMEM((tq, n), BF16), pltpu.VMEM((tq, 1), F32)]
            + ATTN_SCORE_BUFFERS * [pltpu.VMEM((tq, n_ctx), F32), pltpu.VMEM((tq, n), F32)]),
        compiler_params=_params("arbitrary"),
        name="attn",
    )(q, k_c, k, vx_c, vx)


DFT_RADIX = 8


def _dft_tables(n):
    r = n // DFT_RADIX
    k = np.arange(DFT_RADIX)[:, None, None] + DFT_RADIX * np.arange(r)[None, :, None]
    ang = 2.0 * np.pi * ((k * np.arange(r)[None, None, :]) % n) / n
    cos, sin = np.cos(ang), np.sin(ang)
    pos = np.concatenate([np.concatenate([cos, sin], axis=2),
                          np.concatenate([-sin, cos], axis=2)], axis=1)
    c = FOURIER_GROUP
    scale = 1.0 / np.sqrt(float(n * c))
    ang_c = 2.0 * np.pi * (np.outer(np.arange(c), np.arange(c)) % c) / c
    eye = np.eye(V7X_MXU_DIM // c)
    chan = np.concatenate([np.kron(eye, np.cos(ang_c) * scale),
                           np.kron(eye, np.sin(ang_c) * scale)], axis=0)
    return jnp.asarray(pos, F32).astype(BF16), jnp.asarray(chan, F32).astype(BF16)


def _fold_radix8(x):
    c = 0.5 ** 0.5
    e02p, e02m, e13p, e13m = x[0] + x[4], x[0] - x[4], x[2] + x[6], x[2] - x[6]
    o02p, o02m, o13p, o13m = x[1] + x[5], x[1] - x[5], x[3] + x[7], x[3] - x[7]
    e0, e2, o0, o2 = e02p + e13p, e02p - e13p, o02p + o13p, o02p - o13p
    p, q = c * (o02m - o13m), c * (o02m + o13m)
    re = [e0 + o0, e02m + p, e2, e02m - p, e0 - o0, e02m - p, e2, e02m + p]
    im1, im3 = -e13m - q, e13m - q
    im = [None, im1, -o2, im3, None, -im3, o2, -im1]
    return re, im


def _fourier_kernel(f_ref, pos_ref, chan_ref, o_ref, t_ref, zr_ref, zi_ref, y_ref):
    r = f_ref.shape[1] // DFT_RADIX
    w = FOURIER_WIDTH
    pair = V7X_MXU_DIM
    slab = y_ref.shape[2]
    for lo in range(0, w, pair):
        for sub in range(lo, lo + pair, FOURIER_GROUP):
            lanes = slice(sub, sub + FOURIER_GROUP)
            x = [f_ref[0, n1 * r:(n1 + 1) * r, lanes].astype(F32) for n1 in range(DFT_RADIX)]
            re, im = _fold_radix8(x)
            for k1 in range(DFT_RADIX):
                t_ref[k1, 0:r, lanes] = re[k1].astype(BF16)
                t_ref[k1, r:2 * r, lanes] = (
                    jnp.zeros((r, FOURIER_GROUP), BF16) if im[k1] is None else im[k1].astype(BF16))
        h = lo // pair
        for k1 in range(DFT_RADIX):
            z = _dot(pos_ref[k1], t_ref[k1, :, lo:lo + pair])
            zr_ref[h, k1 * r:(k1 + 1) * r, :] = z[0:r].astype(BF16)
            zi_ref[h, k1 * r:(k1 + 1) * r, :] = z[r:2 * r].astype(BF16)
        rows_per_dot = (DFT_RADIX // 2) * r
        for top in range(0, DFT_RADIX * r, rows_per_dot):
            rows = slice(top, top + rows_per_dot)
            y = (_dot(zr_ref[h, rows, :], chan_ref[0:pair, :])
                 + _dot(zi_ref[h, rows, :], chan_ref[pair:2 * pair, :]))
            for k1 in range(top // r, (top + rows_per_dot) // r):
                for j in range(pair // slab):
                    y_ref[(lo + j * slab) // slab, pl.ds(k1, r, stride=DFT_RADIX), :] = (
                        y[k1 * r - top:(k1 + 1) * r - top, j * slab:(j + 1) * slab])
    for j in range(w // slab):
        o_ref[0, :, j * slab:(j + 1) * slab] = y_ref[j].astype(BF16)


def _fourier(f):
    nb, n, w = f.shape
    assert w == FOURIER_WIDTH and n % (DFT_RADIX * V7X_MXU_DIM) == 0, f.shape
    pos, chan = _dft_tables(n)
    r = n // DFT_RADIX
    blk = pl.BlockSpec((1, n, w), lambda b: (b, 0, 0))
    lanes = V7X_LANES
    return pl.pallas_call(
        _fourier_kernel,
        out_shape=jax.ShapeDtypeStruct(f.shape, BF16),
        grid=(nb,),
        in_specs=[blk, _resident(pos.shape), _resident(chan.shape)],
        out_specs=blk,
        scratch_shapes=[pltpu.VMEM((DFT_RADIX, 2 * r, w), BF16),
                        pltpu.VMEM((w // V7X_MXU_DIM, n, V7X_MXU_DIM), BF16),
                        pltpu.VMEM((w // V7X_MXU_DIM, n, V7X_MXU_DIM), BF16),
                        pltpu.VMEM((w // lanes, n, lanes), F32)],
        compiler_params=_params("parallel"),
        name="fourier",
    )(f, pos, chan)


def _merge_kernel(x_ref, mod_ref, ya_ref, yf_ref, ga_ref, gf_ref, wab_ref, wfb_ref, wo_ref, o_ref,
                  merged_ref):
    gate = mod_ref[0, 5:6, :]
    for lo in range(0, D_MODEL, V7X_MXU_DIM):
        cols = slice(lo, lo + V7X_MXU_DIM)
        merged_ref[:, cols] = (ga_ref[0, :, cols].astype(F32) * _dot(ya_ref[0], wab_ref[:, cols])
                               + gf_ref[0, :, cols].astype(F32) * _dot(yf_ref[0], wfb_ref[:, cols])
                               ).astype(BF16)
    mix = _dot(merged_ref[...], wo_ref[...])
    o_ref[0] = x_ref[0] + gate * mix


def _merge(x, mod, y_attn, y_four, sig_a, sig_f, w_ab, w_fb, w_o):
    nb, t, _ = x.shape
    tm = WIDE_TOKEN_TILE
    assert t % tm == 0, x.shape
    tok = lambda w: pl.BlockSpec((1, tm, w), lambda b, i: (b, i, 0))
    return pl.pallas_call(
        _merge_kernel,
        out_shape=jax.ShapeDtypeStruct(x.shape, F32),
        grid=(nb, t // tm),
        in_specs=[
            tok(D_MODEL),
            pl.BlockSpec((1, N_MOD, D_MODEL), lambda b, i: (b, 0, 0)),
            tok(Q_WIDTH), tok(FOURIER_WIDTH), tok(D_MODEL), tok(D_MODEL),
            _resident(w_ab.shape), _resident(w_fb.shape), _resident(w_o.shape),
        ],
        out_specs=tok(D_MODEL),
        scratch_shapes=[pltpu.VMEM((tm, D_MODEL), BF16)],
        compiler_params=_params("parallel", "parallel"),
        name="merge",
    )(x, mod, y_attn, y_four, sig_a, sig_f, w_ab, w_fb, w_o)


def _rope_tables(n_tokens):
    f32 = np.float32
    rows = n_tokens // GRID_W
    row_ids = np.repeat(np.arange(rows, dtype=f32), GRID_W)
    col_ids = np.tile(np.arange(GRID_W, dtype=f32), rows)
    inv_freq = f32(ROPE_THETA) ** (-np.arange(0, AXIS_ROPE_DIM, 2, dtype=f32) / f32(AXIS_ROPE_DIM))
    ang = np.concatenate([row_ids[:, None] * inv_freq, col_ids[:, None] * inv_freq], axis=-1)
    cos, sin = np.cos(ang), np.sin(ang)
    assert ang.dtype == f32 and cos.dtype == f32
    return (jnp.asarray(np.concatenate([cos, cos], axis=-1)),
            jnp.asarray(np.concatenate([-sin, sin], axis=-1)))


def kernel(x, c, ctx, c_ctx, w_ada, b_ada, norm_ffn1, w_ffn1_in, w_ffn1_out, norm_mix, w_in,
           q_norm, k_norm, w_attn_branch, w_fourier_branch, w_out, norm_ffn2, w_ffn2_in, w_ffn2_out):
    nb, n_lat, _ = x.shape
    n_ctx = ctx.shape[1]
    depth = w_ada.shape[0]
    assert depth == 1, "the context stream is only carried through the (single) last layer"
    assert x.shape[2] == D_MODEL and ctx.shape == (nb, n_ctx, D_MODEL) and n_lat % GRID_W == 0
    assert w_in.shape[1:] == (D_MODEL, GF_OFF + D_MODEL) and w_ffn1_in.shape[1:] == (D_MODEL, 2 * D_FF)
    assert (nb * n_ctx) % TOKEN_TILE == 0, "context tokens are processed as one flat stream of tiles"
    cos2, sin2 = _rope_tables(n_lat)
    ctx_row = nb

    i = 0
    mod = _adaln(c, c_ctx, w_ada[i], b_ada[i][None, :])
    own_row, shared_row = (lambda b: b), (lambda b: ctx_row)
    x, ctx_flat = _ffn([x, ctx.reshape(1, nb * n_ctx, D_MODEL)], mod, [own_row, shared_row], 0,
                       norm_ffn1[i][None, :], w_ffn1_in[i], w_ffn1_out[i])

    q, k, vx, f, sig_a, sig_f, k_c, vx_c = _inproj(
        x, ctx_flat, mod, [own_row, shared_row], norm_mix[i][None, :], w_in[i],
        q_norm[i][None, :], k_norm[i][None, :], cos2, sin2)
    y_attn = _attention(q, k, vx, k_c.reshape(nb, n_ctx, KV_WIDTH), vx_c.reshape(nb, n_ctx, VX_WIDTH))
    y_four = _fourier(f)
    x = _merge(x, mod, y_attn, y_four, sig_a, sig_f, w_attn_branch[i], w_fourier_branch[i], w_out[i])

    return _ffn([x], mod, [own_row], 6, norm_ffn2[i][None, :], w_ffn2_in[i], w_ffn2_out[i])[0]
```

```python
import functools

import numpy as np
import jax
import jax.numpy as jnp
from jax import lax
from jax.experimental import pallas as pl
from jax.experimental.pallas import tpu as pltpu

D_MODEL = 1024
GRID_W = 64
HEAD_DIM = 128
N_Q_HEADS = D_MODEL // HEAD_DIM
N_KV_HEADS = N_Q_HEADS // 4
GQA_GROUP = N_Q_HEADS // N_KV_HEADS
Q_WIDTH = N_Q_HEADS * HEAD_DIM
KV_WIDTH = N_KV_HEADS * HEAD_DIM
FOURIER_GROUP = 128
N_FOURIER_GROUPS = 4
FOURIER_WIDTH = N_FOURIER_GROUPS * FOURIER_GROUP
D_FF = 2816
AXIS_ROPE_DIM = HEAD_DIM // 2
ROPE_THETA = 10000.0
EPS = 1e-6
N_MOD = 9
ATTN_SCALE = HEAD_DIM ** -0.5
LOG2_E = 1.4426950408889634

K_OFF = Q_WIDTH
V_OFF = K_OFF + KV_WIDTH
F_OFF = V_OFF + KV_WIDTH
GA_OFF = F_OFF + FOURIER_WIDTH
GF_OFF = GA_OFF + D_MODEL

V7X_MXU_DIM = 256
V7X_LANES = 128
V7X_SUBLANES = 8
VX_WIDTH = N_KV_HEADS * V7X_MXU_DIM
V7X_VMEM_BYTES = 64 * 2**20
VMEM_LIMIT = V7X_VMEM_BYTES - 3 * 2**20

TOKEN_TILE = 512
WIDE_TOKEN_TILE = 1024
FF_CHUNK = V7X_MXU_DIM
ATTN_Q_TILE = 1024
ADALN_TILE = 256

F32 = jnp.float32
BF16 = jnp.bfloat16


def _params(*sem):
    return pltpu.CompilerParams(dimension_semantics=sem, vmem_limit_bytes=VMEM_LIMIT)


def _resident(shape):
    return pl.BlockSpec(shape, lambda *_: (0,) * len(shape), pipeline_mode=pl.Buffered(1))


def _rms(x, gain):
    return x * lax.rsqrt(jnp.mean(x * x, axis=-1, keepdims=True) + EPS) * gain


def _dot(a, b):
    return jnp.dot(a, b.astype(BF16), preferred_element_type=F32)


def _sigmoid(x):
    return 0.5 * jnp.tanh(0.5 * x) + 0.5


def _adaln_kernel(c_ref, cctx_ref, w_ref, b_ref, o_ref, acc_ref):
    c, c_ctx = c_ref[...], cctx_ref[...]
    tail_rows = acc_ref.shape[0] - c.shape[0]
    first_tail_row = lax.broadcasted_iota(jnp.int32, (tail_rows, c.shape[1]), 0) == 0
    silu = lambda v: v * _sigmoid(v)
    a = jnp.concatenate([silu(c), jnp.where(first_tail_row, silu(c_ctx), 0.0)], axis=0)
    part = _dot(a.astype(BF16), w_ref[...])

    @pl.when(pl.program_id(0) == 0)
    def _first():
        acc_ref[...] = part + b_ref[...]

    @pl.when(pl.program_id(0) > 0)
    def _rest():
        acc_ref[...] += part

    @pl.when(pl.program_id(0) == pl.num_programs(0) - 1)
    def _emit():
        for m in range(N_MOD):
            o_ref[:, m, :] = acc_ref[:, m * D_MODEL:(m + 1) * D_MODEL]


def _adaln(c, c_ctx, w, b):
    nb, n_out = c.shape[0], w.shape[1]
    assert nb % V7X_SUBLANES == 0 and D_MODEL % ADALN_TILE == 0 and n_out == N_MOD * D_MODEL
    rows = nb + V7X_SUBLANES
    return pl.pallas_call(
        _adaln_kernel,
        out_shape=jax.ShapeDtypeStruct((rows, N_MOD, D_MODEL), F32),
        grid=(D_MODEL // ADALN_TILE,),
        in_specs=[
            pl.BlockSpec((nb, ADALN_TILE), lambda k: (0, k)),
            pl.BlockSpec((1, ADALN_TILE), lambda k: (0, k)),
            pl.BlockSpec((ADALN_TILE, n_out), lambda k: (k, 0)),
            pl.BlockSpec((1, n_out), lambda k: (0, 0)),
        ],
        out_specs=pl.BlockSpec((rows, N_MOD, D_MODEL), lambda k: (0, 0, 0)),
        scratch_shapes=[pltpu.VMEM((rows, n_out), F32)],
        compiler_params=_params("arbitrary"),
        name="adaln",
    )(c, c_ctx[None, :], w, b)


class _TokenStreams:
    def __init__(self, arrays, tile):
        assert all(a.ndim == 3 and a.shape[1] % tile == 0 for a in arrays), [a.shape for a in arrays]
        self.tile = tile
        self.per_batch = [a.shape[1] // tile for a in arrays]
        counts = [a.shape[0] * p for a, p in zip(arrays, self.per_batch)]
        self.starts = [sum(counts[:s]) for s in range(len(arrays) + 1)]
        self.steps = self.starts[-1]

    def batch_and_tile(self, s, t):
        local = jnp.clip(t - self.starts[s], 0, self.starts[s + 1] - self.starts[s] - 1)
        return local // self.per_batch[s], local % self.per_batch[s]

    def token_spec(self, s, width):
        def index(t):
            b, i = self.batch_and_tile(s, t)
            return b, i, 0
        return pl.BlockSpec((1, self.tile, width), index)

    def mod_spec(self, mod_rows):
        def index(t):
            row = mod_rows[0](self.batch_and_tile(0, t)[0])
            for s in range(1, len(mod_rows)):
                row = jnp.where(t >= self.starts[s], mod_rows[s](self.batch_and_tile(s, t)[0]), row)
            return row, 0, 0
        return pl.BlockSpec((1, N_MOD, D_MODEL), index)

    def active(self, s):
        t = pl.program_id(0)
        return jnp.logical_and(t >= self.starts[s], t < self.starts[s + 1])


def _ffn_kernel(*refs, streams, mod_base):
    n = len(streams.per_batch)
    x_refs, (mod_ref, gain_ref, win_ref, wout_ref), o_refs, act_ref = (
        refs[:n], refs[n:n + 4], refs[n + 4:2 * n + 4], refs[2 * n + 4])

    def half_step(x_ref, o_ref):
        x = x_ref[0]
        shift = mod_ref[0, mod_base:mod_base + 1, :]
        scale = mod_ref[0, mod_base + 1:mod_base + 2, :]
        gate = mod_ref[0, mod_base + 2:mod_base + 3, :]
        h = (_rms(x, gain_ref[...]) * (1.0 + scale) + shift).astype(BF16)
        for j in range(D_FF // FF_CHUNK):
            lo = j * FF_CHUNK
            g = _dot(h, win_ref[:, lo:lo + FF_CHUNK])
            u = _dot(h, win_ref[:, D_FF + lo:D_FF + lo + FF_CHUNK])
            act_ref[:, lo:lo + FF_CHUNK] = (g * _sigmoid(g) * u).astype(BF16)
        y = _dot(act_ref[...], wout_ref[...])
        o_ref[0] = x + 0.5 * gate * y

    if n == 1:
        half_step(x_refs[0], o_refs[0])
    else:
        for s in range(n):
            pl.when(streams.active(s))(functools.partial(half_step, x_refs[s], o_refs[s]))


def _ffn(xs, mod, mod_rows, mod_base, gain, w_in, w_out):
    tile = WIDE_TOKEN_TILE if len(xs) == 1 else TOKEN_TILE
    streams = _TokenStreams(xs, tile)
    tok = [streams.token_spec(s, D_MODEL) for s in range(len(xs))]
    return pl.pallas_call(
        functools.partial(_ffn_kernel, streams=streams, mod_base=mod_base),
        out_shape=[jax.ShapeDtypeStruct(x.shape, F32) for x in xs],
        grid=(streams.steps,),
        in_specs=tok + [
            streams.mod_spec(mod_rows),
            _resident((1, D_MODEL)),
            _resident((D_MODEL, 2 * D_FF)),
            _resident((D_FF, D_MODEL)),
        ],
        out_specs=tok,
        scratch_shapes=[pltpu.VMEM((tile, D_FF), BF16)],
        compiler_params=_params("arbitrary"),
        name="ffn",
    )(*xs, mod, gain, w_in, w_out)


def _head_norm(t, gain):
    return t * lax.rsqrt(jnp.mean(t * t, axis=-1, keepdims=True) + EPS) * gain


def _rope(t, cos2, sin2):
    return t * cos2 + pltpu.roll(t, AXIS_ROPE_DIM, 1) * sin2


def _store_widened_values(v_ref, v):
    ones = jnp.ones((v.shape[0], V7X_MXU_DIM - HEAD_DIM), BF16)
    for hd in range(N_KV_HEADS):
        lo = hd * V7X_MXU_DIM
        v_ref[0, :, lo:lo + HEAD_DIM] = v[:, hd * HEAD_DIM:(hd + 1) * HEAD_DIM].astype(BF16)
        v_ref[0, :, lo + HEAD_DIM:lo + V7X_MXU_DIM] = ones


def _inproj_kernel(x_ref, ctx_ref, mod_ref, gain_ref, w_ref, qn_ref, kn_ref, cos_ref, sin_ref,
                   q_ref, k_ref, v_ref, f_ref, ga_ref, gf_ref, kc_ref, vc_ref, *, streams):
    kn = kn_ref[...]
    heads_per_dot = V7X_MXU_DIM // HEAD_DIM

    def normed_input(tok_ref):
        shift = mod_ref[0, 3:4, :]
        scale = mod_ref[0, 4:5, :]
        return (_rms(tok_ref[0], gain_ref[...]) * (1.0 + scale) + shift).astype(BF16)

    @pl.when(streams.active(0))
    def _latent_tokens():
        h = normed_input(x_ref)

        def wide_dot(col):
            return _dot(h, w_ref[:, col:col + V7X_MXU_DIM])

        cos2 = cos_ref[...]
        sin2 = sin_ref[...]
        qn = qn_ref[...] * (ATTN_SCALE * LOG2_E)
        for c in range(Q_WIDTH // V7X_MXU_DIM):
            t = wide_dot(c * V7X_MXU_DIM)
            for j in range(heads_per_dot):
                tj = t[:, j * HEAD_DIM:(j + 1) * HEAD_DIM]
                q_ref[0, c * heads_per_dot + j] = _rope(_head_norm(tj, qn), cos2, sin2).astype(BF16)
        for c in range(KV_WIDTH // V7X_MXU_DIM):
            t = wide_dot(K_OFF + c * V7X_MXU_DIM)
            for j in range(heads_per_dot):
                lo = c * V7X_MXU_DIM + j * HEAD_DIM
                tj = t[:, j * HEAD_DIM:(j + 1) * HEAD_DIM]
                k_ref[0, :, lo:lo + HEAD_DIM] = _rope(_head_norm(tj, kn), cos2, sin2).astype(BF16)
        _store_widened_values(v_ref, _dot(h, w_ref[:, V_OFF:V_OFF + KV_WIDTH]))
        for c in range(FOURIER_WIDTH // V7X_MXU_DIM):
            lo = c * V7X_MXU_DIM
            f_ref[0, :, lo:lo + V7X_MXU_DIM] = wide_dot(F_OFF + lo).astype(BF16)
        for c in range(D_MODEL // V7X_MXU_DIM):
            lo = c * V7X_MXU_DIM
            ga_ref[0, :, lo:lo + V7X_MXU_DIM] = _sigmoid(wide_dot(GA_OFF + lo)).astype(BF16)
            gf_ref[0, :, lo:lo + V7X_MXU_DIM] = _sigmoid(wide_dot(GF_OFF + lo)).astype(BF16)

    @pl.when(streams.active(1))
    def _context_tokens():
        h = normed_input(ctx_ref)
        t = _dot(h, w_ref[:, K_OFF:K_OFF + KV_WIDTH])
        for hd in range(N_KV_HEADS):
            lo = hd * HEAD_DIM
            kc_ref[0, :, lo:lo + HEAD_DIM] = _head_norm(t[:, lo:lo + HEAD_DIM], kn).astype(BF16)
        _store_widened_values(vc_ref, _dot(h, w_ref[:, V_OFF:V_OFF + KV_WIDTH]))


def _inproj(x, ctx, mod, mod_rows, gain, w_in, q_norm, k_norm, cos2, sin2):
    nb, t, _ = x.shape
    streams = _TokenStreams([x, ctx], TOKEN_TILE)
    tm = streams.tile
    tok = lambda w: streams.token_spec(0, w)
    ctx_tok = lambda w: streams.token_spec(1, w)
    shp = lambda w: jax.ShapeDtypeStruct((nb, t, w), BF16)
    ctx_shp = lambda w: jax.ShapeDtypeStruct(ctx.shape[:2] + (w,), BF16)

    def q_index(step):
        b, i = streams.batch_and_tile(0, step)
        return b, 0, i, 0

    def rope_index(step):
        return streams.batch_and_tile(0, step)[1], 0

    return pl.pallas_call(
        functools.partial(_inproj_kernel, streams=streams),
        out_shape=[jax.ShapeDtypeStruct((nb, N_Q_HEADS, t, HEAD_DIM), BF16), shp(KV_WIDTH),
                   shp(VX_WIDTH), shp(FOURIER_WIDTH), shp(D_MODEL), shp(D_MODEL),
                   ctx_shp(KV_WIDTH), ctx_shp(VX_WIDTH)],
        grid=(streams.steps,),
        in_specs=[
            tok(D_MODEL),
            ctx_tok(D_MODEL),
            streams.mod_spec(mod_rows),
            _resident((1, D_MODEL)),
            _resident(w_in.shape),
            _resident((1, HEAD_DIM)),
            _resident((1, HEAD_DIM)),
            pl.BlockSpec((tm, HEAD_DIM), rope_index),
            pl.BlockSpec((tm, HEAD_DIM), rope_index),
        ],
        out_specs=[pl.BlockSpec((1, N_Q_HEADS, tm, HEAD_DIM), q_index),
                   tok(KV_WIDTH), tok(VX_WIDTH), tok(FOURIER_WIDTH), tok(D_MODEL), tok(D_MODEL),
                   ctx_tok(KV_WIDTH), ctx_tok(VX_WIDTH)],
        compiler_params=_params("arbitrary"),
        name="inproj",
    )(x, ctx, mod, gain, w_in, q_norm, k_norm, cos2, sin2)


_NT = (((1,), (1,)), ((), ()))


ATTN_SCRATCH_PER_HEAD = 3
ATTN_SCORE_BUFFERS = 3


def _attn_kernel(q_ref, kc_ref, k_ref, vcx_ref, vx_ref, o_ref, *scratch):
    n_head_refs = ATTN_SCRATCH_PER_HEAD * GQA_GROUP
    heads = [scratch[ATTN_SCRATCH_PER_HEAD * g:ATTN_SCRATCH_PER_HEAD * (g + 1)] for g in range(GQA_GROUP)]
    scores = [scratch[n_head_refs + 2 * b:n_head_refs + 2 * b + 2] for b in range(ATTN_SCORE_BUFFERS)]
    score_refs = lambda g: scores[g % ATTN_SCORE_BUFFERS]

    @pl.when(pl.program_id(0) == 0)
    def _no_previous_item():
        for pc_ref, pl_ref, _ in heads:
            pc_ref[...] = jnp.ones(pc_ref.shape, BF16)
            pl_ref[...] = jnp.ones(pl_ref.shape, BF16)
        for ref in (*score_refs(GQA_GROUP - 1), heads[-1][2]):
            ref[...] = jnp.zeros(ref.shape, F32)

    def weights_from_scores(g):
        pc_ref, pl_ref, m_ref = heads[g]
        m = m_ref[...]
        for s_ref, p_ref in zip(score_refs(g), (pc_ref, pl_ref)):
            for lo in range(0, s_ref.shape[1], V7X_MXU_DIM):
                p_ref[:, lo:lo + V7X_MXU_DIM] = jnp.exp2(s_ref[:, lo:lo + V7X_MXU_DIM] - m).astype(BF16)

    def values_then_scores(g):
        pc_ref, pl_ref, m_ref = heads[g]
        sc_ref, sl_ref = score_refs(g)
        oe = _dot(pc_ref[...], vcx_ref[0]) + _dot(pl_ref[...], vx_ref[0])
        o_ref[0, :, g * HEAD_DIM:(g + 1) * HEAD_DIM] = (oe[:, :HEAD_DIM] / oe[:, HEAD_DIM:]).astype(BF16)

        q = q_ref[0, g]
        s_c = lax.dot_general(q, kc_ref[0], _NT, preferred_element_type=F32)
        s_l = lax.dot_general(q, k_ref[0], _NT, preferred_element_type=F32)
        m_ref[...] = jnp.maximum(jnp.max(s_c, axis=-1, keepdims=True), jnp.max(s_l, axis=-1, keepdims=True))
        sc_ref[...] = s_c
        sl_ref[...] = s_l

    last = GQA_GROUP - 1
    weights_from_scores(last)
    for g in range(GQA_GROUP):
        if g >= ATTN_SCORE_BUFFERS:
            weights_from_scores(g - ATTN_SCORE_BUFFERS)
        values_then_scores(g)
    for g in range(GQA_GROUP - ATTN_SCORE_BUFFERS, last):
        weights_from_scores(g)


def _attention(q, k, vx, k_c, vx_c):
    nb, _, n, _ = q.shape
    n_ctx = k_c.shape[1]
    tq = ATTN_Q_TILE
    assert n % tq == 0 and n % V7X_MXU_DIM == 0 and n_ctx % V7X_MXU_DIM == 0, (n, n_ctx)
    tiles = n // tq
    items = nb * N_KV_HEADS * tiles

    def item(t):
        return t // (N_KV_HEADS * tiles), (t // tiles) % N_KV_HEADS, t % tiles

    def scored(t):
        return item(jnp.minimum(t, items - 1))

    def weighted(t):
        return item(jnp.maximum(t - 1, 0))

    def q_map(t):
        b, h, i = scored(t)
        return b, h, i, 0

    def k_map(t):
        b, h, _ = scored(t)
        return b, 0, h

    def v_map(t):
        b, h, _ = weighted(t)
        return b, 0, h

    def o_map(t):
        b, h, i = weighted(t)
        return b, i, h

    return pl.pallas_call(
        _attn_kernel,
        out_shape=jax.ShapeDtypeStruct((nb, n, Q_WIDTH), BF16),
        grid=(items + 1,),
        in_specs=[
            pl.BlockSpec((1, GQA_GROUP, tq, HEAD_DIM), q_map),
            pl.BlockSpec((1, n_ctx, HEAD_DIM), k_map),
            pl.BlockSpec((1, n, HEAD_DIM), k_map),
            pl.BlockSpec((1, n_ctx, V7X_MXU_DIM), v_map),
            pl.BlockSpec((1, n, V7X_MXU_DIM), v_map),
        ],
        out_specs=pl.BlockSpec((1, tq, GQA_GROUP * HEAD_DIM), o_map),
        scratch_shapes=(
            GQA_GROUP * [pltpu.VMEM((tq, n_ctx), BF16), pltpu.VMEM((tq, n), BF16), pltpu.VMEM((tq, 1), F32)]
            + ATTN_SCORE_BUFFERS * [pltpu.VMEM((tq, n_ctx), F32), pltpu.VMEM((tq, n), F32)]),
        compiler_params=_params("arbitrary"),
        name="attn",
    )(q, k_c, k, vx_c, vx)


DFT_RADIX = 8


def _dft_tables(n):
    r = n // DFT_RADIX
    k = np.arange(DFT_RADIX)[:, None, None] + DFT_RADIX * np.arange(r)[None, :, None]
    ang = 2.0 * np.pi * ((k * np.arange(r)[None, None, :]) % n) / n
    cos, sin = np.cos(ang), np.sin(ang)
    pos = np.concatenate([np.concatenate([cos, sin], axis=2),
                          np.concatenate([-sin, cos], axis=2)], axis=1)
    c = FOURIER_GROUP
    scale = 1.0 / np.sqrt(float(n * c))
    ang_c = 2.0 * np.pi * (np.outer(np.arange(c), np.arange(c)) % c) / c
    eye = np.eye(V7X_MXU_DIM // c)
    chan = np.concatenate([np.kron(eye, np.cos(ang_c) * scale),
                           np.kron(eye, np.sin(ang_c) * scale)], axis=0)
    return jnp.asarray(pos, F32).astype(BF16), jnp.asarray(chan, F32).astype(BF16)


def _fold_radix8(x):
    c = 0.5 ** 0.5
    e02p, e02m, e13p, e13m = x[0] + x[4], x[0] - x[4], x[2] + x[6], x[2] - x[6]
    o02p, o02m, o13p, o13m = x[1] + x[5], x[1] - x[5], x[3] + x[7], x[3] - x[7]
    e0, e2, o0, o2 = e02p + e13p, e02p - e13p, o02p + o13p, o02p - o13p
    p, q = c * (o02m - o13m), c * (o02m + o13m)
    re = [e0 + o0, e02m + p, e2, e02m - p, e0 - o0, e02m - p, e2, e02m + p]
    im1, im3 = -e13m - q, e13m - q
    im = [None, im1, -o2, im3, None, -im3, o2, -im1]
    return re, im


def _fourier_kernel(f_ref, pos_ref, chan_ref, o_ref, t_ref, zr_ref, zi_ref, y_ref):
    r = f_ref.shape[1] // DFT_RADIX
    w = FOURIER_WIDTH
    pair = V7X_MXU_DIM
    slab = y_ref.shape[2]
    for lo in range(0, w, pair):
        for sub in range(lo, lo + pair, FOURIER_GROUP):
            lanes = slice(sub, sub + FOURIER_GROUP)
            x = [f_ref[0, n1 * r:(n1 + 1) * r, lanes].astype(F32) for n1 in range(DFT_RADIX)]
            re, im = _fold_radix8(x)
            for k1 in range(DFT_RADIX):
                t_ref[k1, 0:r, lanes] = re[k1].astype(BF16)
                t_ref[k1, r:2 * r, lanes] = (
                    jnp.zeros((r, FOURIER_GROUP), BF16) if im[k1] is None else im[k1].astype(BF16))
        h = lo // pair
        for k1 in range(DFT_RADIX):
            z = _dot(pos_ref[k1], t_ref[k1, :, lo:lo + pair])
            zr_ref[h, k1 * r:(k1 + 1) * r, :] = z[0:r].astype(BF16)
            zi_ref[h, k1 * r:(k1 + 1) * r, :] = z[r:2 * r].astype(BF16)
        rows_per_dot = (DFT_RADIX // 2) * r
        for top in range(0, DFT_RADIX * r, rows_per_dot):
            rows = slice(top, top + rows_per_dot)
            y = (_dot(zr_ref[h, rows, :], chan_ref[0:pair, :])
                 + _dot(zi_ref[h, rows, :], chan_ref[pair:2 * pair, :]))
            for k1 in range(top // r, (top + rows_per_dot) // r):
                for j in range(pair // slab):
                    y_ref[(lo + j * slab) // slab, pl.ds(k1, r, stride=DFT_RADIX), :] = (
                        y[k1 * r - top:(k1 + 1) * r - top, j * slab:(j + 1) * slab])
    for j in range(w // slab):
        o_ref[0, :, j * slab:(j + 1) * slab] = y_ref[j].astype(BF16)


def _fourier(f):
    nb, n, w = f.shape
    assert w == FOURIER_WIDTH and n % (DFT_RADIX * V7X_MXU_DIM) == 0, f.shape
    pos, chan = _dft_tables(n)
    r = n // DFT_RADIX
    blk = pl.BlockSpec((1, n, w), lambda b: (b, 0, 0))
    lanes = V7X_LANES
    return pl.pallas_call(
        _fourier_kernel,
        out_shape=jax.ShapeDtypeStruct(f.shape, BF16),
        grid=(nb,),
        in_specs=[blk, _resident(pos.shape), _resident(chan.shape)],
        out_specs=blk,
        scratch_shapes=[pltpu.VMEM((DFT_RADIX, 2 * r, w), BF16),
                        pltpu.VMEM((w // V7X_MXU_DIM, n, V7X_MXU_DIM), BF16),
                        pltpu.VMEM((w // V7X_MXU_DIM, n, V7X_MXU_DIM), BF16),
                        pltpu.VMEM((w // lanes, n, lanes), F32)],
        compiler_params=_params("parallel"),
        name="fourier",
    )(f, pos, chan)


def _merge_kernel(x_ref, mod_ref, ya_ref, yf_ref, ga_ref, gf_ref, wab_ref, wfb_ref, wo_ref, o_ref,
                  merged_ref):
    gate = mod_ref[0, 5:6, :]
    for lo in range(0, D_MODEL, V7X_MXU_DIM):
        cols = slice(lo, lo + V7X_MXU_DIM)
        merged_ref[:, cols] = (ga_ref[0, :, cols].astype(F32) * _dot(ya_ref[0], wab_ref[:, cols])
                               + gf_ref[0, :, cols].astype(F32) * _dot(yf_ref[0], wfb_ref[:, cols])
                               ).astype(BF16)
    mix = _dot(merged_ref[...], wo_ref[...])
    o_ref[0] = x_ref[0] + gate * mix


def _merge(x, mod, y_attn, y_four, sig_a, sig_f, w_ab, w_fb, w_o):
    nb, t, _ = x.shape
    tm = WIDE_TOKEN_TILE
    assert t % tm == 0, x.shape
    tok = lambda w: pl.BlockSpec((1, tm, w), lambda b, i: (b, i, 0))
    return pl.pallas_call(
        _merge_kernel,
        out_shape=jax.ShapeDtypeStruct(x.shape, F32),
        grid=(nb, t // tm),
        in_specs=[
            tok(D_MODEL),
            pl.BlockSpec((1, N_MOD, D_MODEL), lambda b, i: (b, 0, 0)),
            tok(Q_WIDTH), tok(FOURIER_WIDTH), tok(D_MODEL), tok(D_MODEL),
            _resident(w_ab.shape), _resident(w_fb.shape), _resident(w_o.shape),
        ],
        out_specs=tok(D_MODEL),
        scratch_shapes=[pltpu.VMEM((tm, D_MODEL), BF16)],
        compiler_params=_params("parallel", "parallel"),
        name="merge",
    )(x, mod, y_attn, y_four, sig_a, sig_f, w_ab, w_fb, w_o)


def _rope_tables(n_tokens):
    f32 = np.float32
    rows = n_tokens // GRID_W
    row_ids = np.repeat(np.arange(rows, dtype=f32), GRID_W)
    col_ids = np.tile(np.arange(GRID_W, dtype=f32), rows)
    inv_freq = f32(ROPE_THETA) ** (-np.arange(0, AXIS_ROPE_DIM, 2, dtype=f32) / f32(AXIS_ROPE_DIM))
    ang = np.concatenate([row_ids[:, None] * inv_freq, col_ids[:, None] * inv_freq], axis=-1)
    cos, sin = np.cos(ang), np.sin(ang)
    assert ang.dtype == f32 and cos.dtype == f32
    return (jnp.asarray(np.concatenate([cos, cos], axis=-1)),
            jnp.asarray(np.concatenate([-sin, sin], axis=-1)))


def kernel(x, c, ctx, c_ctx, w_ada, b_ada, norm_ffn1, w_ffn1_in, w_ffn1_out, norm_mix, w_in,
           q_norm, k_norm, w_attn_branch, w_fourier_branch, w_out, norm_ffn2, w_ffn2_in, w_ffn2_out):
    nb, n_lat, _ = x.shape
    n_ctx = ctx.shape[1]
    depth = w_ada.shape[0]
    assert depth == 1, "the context stream is only carried through the (single) last layer"
    assert x.shape[2] == D_MODEL and ctx.shape == (nb, n_ctx, D_MODEL) and n_lat % GRID_W == 0
    assert w_in.shape[1:] == (D_MODEL, GF_OFF + D_MODEL) and w_ffn1_in.shape[1:] == (D_MODEL, 2 * D_FF)
    assert (nb * n_ctx) % TOKEN_TILE == 0, "context tokens are processed as one flat stream of tiles"
    cos2, sin2 = _rope_tables(n_lat)
    ctx_row = nb

    i = 0
    mod = _adaln(c, c_ctx, w_ada[i], b_ada[i][None, :])
    own_row, shared_row = (lambda b: b), (lambda b: ctx_row)
    x, = _ffn([x], mod, [own_row], 0, norm_ffn1[i][None, :], w_ffn1_in[i], w_ffn1_out[i])
    ctx_flat, = _ffn([ctx.reshape(1, nb * n_ctx, D_MODEL)], mod, [shared_row], 0,
                     norm_ffn1[i][None, :], w_ffn1_in[i], w_ffn1_out[i])

    q, k, vx, f, sig_a, sig_f, k_c, vx_c = _inproj(
        x, ctx_flat, mod, [own_row, shared_row], norm_mix[i][None, :], w_in[i],
        q_norm[i][None, :], k_norm[i][None, :], cos2, sin2)
    y_attn = _attention(q, k, vx, k_c.reshape(nb, n_ctx, KV_WIDTH), vx_c.reshape(nb, n_ctx, VX_WIDTH))
    y_four = _fourier(f)
    x = _merge(x, mod, y_attn, y_four, sig_a, sig_f, w_attn_branch[i], w_fourier_branch[i], w_out[i])

    return _ffn([x], mod, [own_row], 6, norm_ffn2[i][None, :], w_ffn2_in[i], w_ffn2_out[i])[0]
```
